```python
import jax, jax.numpy as jnp
from jax import lax
import numpy as np

D_MODEL = 4096
BATCH = 4
SEQ = 2048
DEPTH = 2
DEC_BATCH = 32
DEC_SEQ = 8
PAST_LEN = 16384
PAGE_SIZE = 128

HEAD_DIM = 64
N_HEADS = D_MODEL // 128
N_KV_HEADS = 4
KV_GROUP = N_HEADS // N_KV_HEADS
ATTN_DIM = N_HEADS * HEAD_DIM
KV_DIM = N_KV_HEADS * HEAD_DIM
WINDOW = 128
ROPE_THETA = 10000.0
CONV_DIM = D_MODEL // 2
CONV_WIDTH = 3
FFN_CONV_WIDTH = 3
D_FF = ((8 * D_MODEL // 3 + 127) // 128) * 128
IN_COLS = ATTN_DIM + 2 * KV_DIM + 3 * CONV_DIM + 2 * D_MODEL
LN_EPS = 1e-5
ALPHA = (2 * DEPTH) ** 0.25
BETA = (8 * DEPTH) ** -0.25

kernel_name = 'hybrid_swa_shortconv_convffn_decode_step'


def _split_in(z):
    sizes = (ATTN_DIM, KV_DIM, KV_DIM, CONV_DIM, CONV_DIM, CONV_DIM, D_MODEL, D_MODEL)
    idx, acc = [], 0
    for s in sizes[:-1]:
        acc += s
        idx.append(acc)
    return jnp.split(z, idx, axis=-1)


def _layernorm(x, g, b):
    xf = x.astype(jnp.float32)
    mu = xf.mean(-1, keepdims=True)
    var = jnp.square(xf - mu).mean(-1, keepdims=True)
    y = (xf - mu) * lax.rsqrt(var + LN_EPS) * g.astype(jnp.float32) + b.astype(jnp.float32)
    return y.astype(x.dtype)


def _modulate(x, shift, scale):
    return x * (1 + scale[:, None, :]) + shift[:, None, :]


def _rope(x, pos):
    inv = ROPE_THETA ** (-jnp.arange(0, HEAD_DIM, 2, dtype=jnp.float32) / HEAD_DIM)
    ang = pos.astype(jnp.float32)[:, None] * inv[None, :]
    cos = jnp.cos(ang)[None, :, None, :]
    sin = jnp.sin(ang)[None, :, None, :]
    xf = x.astype(jnp.float32)
    x1, x2 = jnp.split(xf, 2, axis=-1)
    out = jnp.concatenate([x1 * cos - x2 * sin, x2 * cos + x1 * sin], axis=-1)
    return out.astype(x.dtype)


def _causal_dwconv(x, past, w):
    width = w.shape[0]
    t = x.shape[1]
    xc = jnp.concatenate([past.astype(x.dtype), x], axis=1)
    y = w[0] * xc[:, 0:t]
    for j in range(1, width):
        y = y + w[j] * xc[:, j:j + t]
    return y, xc[:, -(width - 1):]


def _sink_window_attend(q, k, v, q_pos, k_pos, sinks):
    s = jnp.einsum('bnqhgd,bnkhd->bnhgqk', q, k, preferred_element_type=jnp.float32) * (HEAD_DIM ** -0.5)
    diff = q_pos[:, :, None] - k_pos[:, None, :]
    valid = (diff >= 0) & (diff <= WINDOW) & (k_pos[:, None, :] >= 0)
    s = jnp.where(valid[None, :, None, None], s, -jnp.inf)
    sink = sinks.astype(jnp.float32).reshape(N_KV_HEADS, KV_GROUP)[None, None, :, :, None, None]
    m = jnp.maximum(s.max(-1, keepdims=True), sink)
    e = jnp.exp(s - m)
    p = e / (e.sum(-1, keepdims=True) + jnp.exp(sink - m))
    return jnp.einsum('bnhgqk,bnkhd->bnqhgd', p.astype(v.dtype), v)


def _swa_prompt(q, k, v, pos, sinks):
    b, t = q.shape[0], q.shape[1]
    nb = t // WINDOW
    qb = q.reshape(b, nb, WINDOW, N_KV_HEADS, KV_GROUP, HEAD_DIM)
    kb = k.reshape(b, nb, WINDOW, N_KV_HEADS, HEAD_DIM)
    vb = v.reshape(b, nb, WINDOW, N_KV_HEADS, HEAD_DIM)
    kk = jnp.concatenate([jnp.concatenate([jnp.zeros_like(kb[:, :1]), kb[:, :-1]], axis=1), kb], axis=2)
    vv = jnp.concatenate([jnp.concatenate([jnp.zeros_like(vb[:, :1]), vb[:, :-1]], axis=1), vb], axis=2)
    q_pos = pos.reshape(nb, WINDOW)
    k_pos = jnp.concatenate([q_pos - WINDOW, q_pos], axis=1)
    o = _sink_window_attend(qb, kk, vv, q_pos, k_pos, sinks)
    return o.reshape(b, t, ATTN_DIM)


def _swa_sample(q, kk, vv, pos, sinks):
    b, t = q.shape[0], q.shape[1]
    qb = q.reshape(b, 1, t, N_KV_HEADS, KV_GROUP, HEAD_DIM)
    k_pos = (PAST_LEN - WINDOW + jnp.arange(WINDOW + t, dtype=jnp.int32))[None]
    o = _sink_window_attend(qb, kk[:, None], vv[:, None], pos[None], k_pos, sinks)
    return o.reshape(b, t, ATTN_DIM)


def _layer(x, c, pos, past, w_ada, b_ada, w_in, sinks, conv_w, w_attn_br, w_conv_br, w_out,
           ln1_g, ln1_b, w_up, ffn_conv_w, w_down, ln2_g, ln2_b):
    b, t, _ = x.shape
    ada = jax.nn.silu(c) @ w_ada + b_ada
    sh1, sc1, g1, sh2, sc2, g2 = jnp.split(ada, 6, axis=-1)
    h = _modulate(x, sh1, sc1)
    q, k, v, cb, cc, cx, ga, gc = _split_in(h @ w_in)
    q = _rope(q.reshape(b, t, N_HEADS, HEAD_DIM), pos)
    k = _rope(k.reshape(b, t, N_KV_HEADS, HEAD_DIM), pos)
    v = v.reshape(b, t, N_KV_HEADS, HEAD_DIM)
    if past is None:
        attn = _swa_prompt(q, k, v, pos, sinks)
        k_state, v_state = k[:, -WINDOW:], v[:, -WINDOW:]
        conv_past = jnp.zeros((b, CONV_WIDTH - 1, CONV_DIM), x.dtype)
        ffn_past = jnp.zeros((b, FFN_CONV_WIDTH - 1, 2 * D_FF), x.dtype)
    else:
        ck, cv, conv_past, ffn_past = past
        kk = jnp.concatenate([ck.astype(k.dtype), k], axis=1)
        vv = jnp.concatenate([cv.astype(v.dtype), v], axis=1)
        attn = _swa_sample(q, kk, vv, pos, sinks)
        k_state, v_state = kk[:, -WINDOW:], vv[:, -WINDOW:]
    conv_out, conv_state = _causal_dwconv(cc * cx, conv_past, conv_w)
    s = cb * conv_out
    merged = jax.nn.sigmoid(ga) * (attn @ w_attn_br) + jax.nn.sigmoid(gc) * (s @ w_conv_br)
    x = _layernorm(ALPHA * x + g1[:, None, :] * (merged @ w_out), ln1_g, ln1_b)
    h2 = _modulate(x, sh2, sc2)
    up, ffn_state = _causal_dwconv(h2 @ w_up, ffn_past, ffn_conv_w)
    gate, val = jnp.split(up, 2, axis=-1)
    f = (jax.nn.silu(gate) * val) @ w_down
    x = _layernorm(ALPHA * x + g2[:, None, :] * f, ln2_g, ln2_b)
    return x, k_state, v_state, conv_state, ffn_state


def setup_inputs(seed: int = 0) -> dict:
    key = jax.random.key(seed)
    ks = jax.random.split(key, 26)

    def nrm(k, shape, scale):
        return jax.random.normal(k, shape, jnp.float32) * scale

    col_scale = jnp.concatenate([
        jnp.ones((ATTN_DIM + KV_DIM,), jnp.float32), jnp.full((KV_DIM,), BETA, jnp.float32),
        jnp.ones((2 * CONV_DIM,), jnp.float32), jnp.full((CONV_DIM,), BETA, jnp.float32),
        jnp.ones((2 * D_MODEL,), jnp.float32)])
    return {
        'x_prompt': nrm(ks[0], (BATCH, SEQ, D_MODEL), 1.0),
        'x_sample': nrm(ks[1], (DEC_BATCH, DEC_SEQ, D_MODEL), 1.0),
        'cache_attn_k': nrm(ks[2], (DEPTH, DEC_BATCH, WINDOW, N_KV_HEADS, HEAD_DIM), 1.0),
        'cache_attn_v': nrm(ks[3], (DEPTH, DEC_BATCH, WINDOW, N_KV_HEADS, HEAD_DIM), BETA),
        'state_conv': nrm(ks[4], (DEPTH, DEC_BATCH, CONV_WIDTH - 1, CONV_DIM), 0.5),
        'state_ffn_conv': nrm(ks[5], (DEPTH, DEC_BATCH, FFN_CONV_WIDTH - 1, 2 * D_FF), 1.0),
        'c_prompt': nrm(ks[6], (BATCH, D_MODEL), 1.0),
        'c_sample': nrm(ks[7], (DEC_BATCH, D_MODEL), 1.0),
        'w_ada': nrm(ks[8], (DEPTH, D_MODEL, 6 * D_MODEL), 0.5 * D_MODEL ** -0.5),
        'b_ada': nrm(ks[9], (DEPTH, 6 * D_MODEL), 0.01),
        'w_in': nrm(ks[10], (DEPTH, D_MODEL, IN_COLS), D_MODEL ** -0.5) * col_scale,
        'attn_sinks': nrm(ks[11], (DEPTH, N_HEADS), 0.5),
        'conv_w': nrm(ks[12], (DEPTH, CONV_WIDTH, CONV_DIM), CONV_WIDTH ** -0.5),
        'w_attn_br': nrm(ks[13], (DEPTH, ATTN_DIM, D_MODEL), BETA * ATTN_DIM ** -0.5),
        'w_conv_br': nrm(ks[14], (DEPTH, CONV_DIM, D_MODEL), BETA * CONV_DIM ** -0.5),
        'w_out': nrm(ks[15], (DEPTH, D_MODEL, D_MODEL), BETA * D_MODEL ** -0.5),
        'ln1_g': 1.0 + nrm(ks[16], (DEPTH, D_MODEL), 0.02),
        'ln1_b': nrm(ks[17], (DEPTH, D_MODEL), 0.02),
        'w_up': nrm(ks[18], (DEPTH, D_MODEL, 2 * D_FF), D_MODEL ** -0.5),
        'ffn_conv_w': nrm(ks[19], (DEPTH, FFN_CONV_WIDTH, 2 * D_FF), FFN_CONV_WIDTH ** -0.5),
        'w_down': nrm(ks[20], (DEPTH, D_FF, D_MODEL), BETA * D_FF ** -0.5),
        'ln2_g': 1.0 + nrm(ks[21], (DEPTH, D_MODEL), 0.02),
        'ln2_b': nrm(ks[22], (DEPTH, D_MODEL), 0.02),
    }


def reference(x_prompt, x_sample, cache_attn_k, cache_attn_v, state_conv, state_ffn_conv, c_prompt, c_sample,
              w_ada, b_ada, w_in, attn_sinks, conv_w, w_attn_br, w_conv_br, w_out, ln1_g, ln1_b,
              w_up, ffn_conv_w, w_down, ln2_g, ln2_b):
    pos_p = jnp.arange(x_prompt.shape[1], dtype=jnp.int32)
    pos_s = PAST_LEN + jnp.arange(x_sample.shape[1], dtype=jnp.int32)
    xp, xs = x_prompt, x_sample
    kp, vp, cp, fp = [], [], [], []
    ks_, vs_, cs_, fs_ = [], [], [], []
    for l in range(DEPTH):
        w = (w_ada[l], b_ada[l], w_in[l], attn_sinks[l], conv_w[l], w_attn_br[l], w_conv_br[l], w_out[l],
             ln1_g[l], ln1_b[l], w_up[l], ffn_conv_w[l], w_down[l], ln2_g[l], ln2_b[l])
        xp, k_, v_, c_, f_ = _layer(xp, c_prompt, pos_p, None, *w)
        kp.append(k_); vp.append(v_); cp.append(c_); fp.append(f_)
        past = (cache_attn_k[l], cache_attn_v[l], state_conv[l], state_ffn_conv[l])
        xs, k_, v_, c_, f_ = _layer(xs, c_sample, pos_s, past, *w)
        ks_.append(k_); vs_.append(v_); cs_.append(c_); fs_.append(f_)
    return (xp, xs, jnp.stack(kp), jnp.stack(vp), jnp.stack(cp), jnp.stack(fp),
            jnp.stack(ks_), jnp.stack(vs_), jnp.stack(cs_), jnp.stack(fs_))
```

```python
import functools

import jax
import jax.numpy as jnp
from jax import lax
from jax.experimental import pallas as pl
from jax.experimental.pallas import tpu as pltpu

D_MODEL = 4096
DEPTH = 2
HEAD_DIM = 64
N_HEADS = 32
N_KV_HEADS = 4
ATTN_DIM = N_HEADS * HEAD_DIM
KV_DIM = N_KV_HEADS * HEAD_DIM
WINDOW = 128
ROPE_THETA = 10000.0
CONV_DIM = D_MODEL // 2
D_FF = 11008
PAST_LEN = 16384
LN_EPS = 1e-5
ALPHA = (2 * DEPTH) ** 0.25

COL_Q = 0
COL_KV = ATTN_DIM
COL_CB = ATTN_DIM + 2 * KV_DIM
COL_CC = COL_CB + CONV_DIM
COL_CX = COL_CC + CONV_DIM
COL_GA = COL_CX + CONV_DIM
COL_GC = COL_GA + D_MODEL

SH1, SC1, G1, SH2, SC2, G2 = range(6)

LANES = 128
SUBLANES = 8
ADA_ROWS = 40
ADA_SAMPLE_ROW0 = 8
VMEM_LIMIT = 56 * 2**20
NEG = -1e30

BF = jnp.bfloat16
F32 = jnp.float32


def _cp(n_axes):
    return pltpu.CompilerParams(dimension_semantics=("arbitrary",) * n_axes, vmem_limit_bytes=VMEM_LIMIT)


def _bf(x):
    return x if x.dtype == BF else x.astype(BF)


def _dot(a, b):
    return jnp.dot(_bf(a), _bf(b), preferred_element_type=F32)


def _ada_body(c_ref, w_ref, b_ref, o_ref):
    c = c_ref[...]
    o_ref[...] = _dot(c * jax.nn.sigmoid(c), w_ref[...]) + b_ref[...]


def _mod_body(x_ref, sc_ref, sh_ref, o_ref):
    o_ref[...] = (x_ref[...] * (1.0 + sc_ref[...]) + sh_ref[...]).astype(o_ref.dtype)


def _proj_rope_body(h_ref, w_ref, cos_ref, sin_ref, o_ref, *, n_rope, scale):
    acc = _dot(h_ref[...], w_ref[...])
    tm, tn = acc.shape
    lane = lax.broadcasted_iota(jnp.int32, (tm, LANES), 1)
    first_half = (lane & (HEAD_DIM - 1)) < HEAD_DIM // 2
    cos = cos_ref[...]
    sin = sin_ref[...]
    for c in range(tn // LANES):
        ch = acc[:, c * LANES:(c + 1) * LANES]
        if c < n_rope:
            rot = jnp.where(first_half,
                            pltpu.roll(ch, LANES - HEAD_DIM // 2, axis=1),
                            pltpu.roll(ch, HEAD_DIM // 2, axis=1))
            ch = ch * cos + rot * sin
        if scale != 1.0:
            ch = ch * scale
        o_ref[:, c * LANES:(c + 1) * LANES] = ch.astype(o_ref.dtype)


def _attend(q_ref, k2, v2, valid, sink_ref, o_ref):
    tq = q_ref.shape[0]
    nk = k2.shape[0]
    lo = lax.broadcasted_iota(jnp.int32, (nk, LANES), 1) < HEAD_DIM
    lo_q = lax.broadcasted_iota(jnp.int32, (tq, LANES), 1) < HEAD_DIM
    group = N_HEADS // N_KV_HEADS
    for m in range(KV_DIM // LANES):
        kc = k2[:, m * LANES:(m + 1) * LANES]
        vc = v2[:, m * LANES:(m + 1) * LANES]
        kr = pltpu.roll(kc, HEAD_DIM, axis=1)
        vr = pltpu.roll(vc, HEAD_DIM, axis=1)
        for hh in range(2):
            kv_head = 2 * m + hh
            k_lo, k_hi = (kc, kr) if hh == 0 else (kr, kc)
            v_lo, v_hi = (vc, vr) if hh == 0 else (vr, vc)
            kbd = jnp.concatenate([jnp.where(lo, k_lo, 0.0), jnp.where(lo, 0.0, k_hi)], axis=0).astype(BF)
            vbd = jnp.concatenate([jnp.where(lo, v_lo, 0.0), jnp.where(lo, 0.0, v_hi)], axis=0).astype(BF)
            for pp in range(group // 2):
                p = kv_head * (group // 2) + pp
                qp = _bf(q_ref[:, p * LANES:(p + 1) * LANES])
                s = lax.dot_general(qp, kbd, (((1,), (1,)), ((), ())), preferred_element_type=F32)
                es, ds = [], []
                for half in range(2):
                    sh = jnp.where(valid, s[:, half * nk:(half + 1) * nk], NEG)
                    sink = sink_ref[2 * p + half]
                    mx = jnp.maximum(jnp.max(sh, axis=-1, keepdims=True), sink)
                    e = jnp.exp(sh - mx)
                    es.append(e)
                    ds.append(jnp.sum(e, axis=-1, keepdims=True) + jnp.exp(sink - mx))
                pm = jnp.concatenate(es, axis=1).astype(BF)
                o = jnp.dot(pm, vbd, preferred_element_type=F32)
                o = o * jnp.where(lo_q, 1.0 / ds[0], 1.0 / ds[1])
                o_ref[:, p * LANES:(p + 1) * LANES] = o.astype(o_ref.dtype)


def _attn_prompt_body(sink_ref, q_ref, kvp_ref, kvc_ref, o_ref):
    n = pl.program_id(1)
    kvp = kvp_ref[...]
    kvc = kvc_ref[...]
    k2 = jnp.concatenate([kvp[:, :KV_DIM], kvc[:, :KV_DIM]], axis=0)
    v2 = jnp.concatenate([kvp[:, KV_DIM:], kvc[:, KV_DIM:]], axis=0)
    i = lax.broadcasted_iota(jnp.int32, (WINDOW, 2 * WINDOW), 0)
    j = lax.broadcasted_iota(jnp.int32, (WINDOW, 2 * WINDOW), 1)
    valid = (j >= i) & (j <= i + WINDOW) & ((j >= WINDOW) | (n > 0))
    _attend(q_ref, k2, v2, valid, sink_ref, o_ref)


def _attn_sample_body(sink_ref, q_ref, kvn_ref, ck_ref, cv_ref, o_ref, ks_ref, vs_ref):
    t = q_ref.shape[0]
    ck = ck_ref[...]
    cv = cv_ref[...]
    kvn = kvn_ref[...]
    pad = jnp.zeros((WINDOW - t, KV_DIM), F32)
    k2 = jnp.concatenate([ck, kvn[:, :KV_DIM], pad], axis=0)
    v2 = jnp.concatenate([cv, kvn[:, KV_DIM:], pad], axis=0)
    i = lax.broadcasted_iota(jnp.int32, (t, 2 * WINDOW), 0)
    j = lax.broadcasted_iota(jnp.int32, (t, 2 * WINDOW), 1)
    valid = (j >= i) & (j <= i + WINDOW)
    _attend(q_ref, k2, v2, valid, sink_ref, o_ref)
    ks_ref[0:WINDOW - t, :] = ck[t:, :]
    ks_ref[WINDOW - t:, :] = kvn[:, :KV_DIM]
    vs_ref[0:WINDOW - t, :] = cv[t:, :]
    vs_ref[WINDOW - t:, :] = kvn[:, KV_DIM:]


def _conv3_carry(u, w_ref, prev):
    row = lax.broadcasted_iota(jnp.int32, u.shape, 0)
    p1 = prev[SUBLANES - 1:SUBLANES, :]
    p2 = prev[SUBLANES - 2:SUBLANES - 1, :]
    x1 = jnp.where(row == 0, p1, pltpu.roll(u, 1, axis=0))
    x2 = jnp.where(row == 0, p2, jnp.where(row == 1, p1, pltpu.roll(u, 2, axis=0)))
    return w_ref[0:1, :] * x2 + w_ref[1:2, :] * x1 + w_ref[2:3, :] * u


def _conv3_segments(u, w_ref, e, seg):
    t = lax.broadcasted_iota(jnp.int32, u.shape, 0) & (seg - 1)
    x1 = jnp.where(t >= 1, pltpu.roll(u, 1, axis=0), pltpu.roll(e, u.shape[0] - 1, axis=0))
    x2 = jnp.where(t >= 2, pltpu.roll(u, 2, axis=0), e)
    return w_ref[0:1, :] * x2 + w_ref[1:2, :] * x1 + w_ref[2:3, :] * u


def _carry_in(carry_ref, tiles_per_seq):
    i = pl.program_id(0)
    j = pl.program_id(1)

    @pl.when(i % tiles_per_seq == 0)
    def _():
        carry_ref[j] = jnp.zeros(carry_ref.shape[1:], F32)

    return carry_ref[j]


def _inconv_prompt_body(h_ref, wb_ref, wc_ref, wx_ref, cw_ref, s_ref, tail_ref, carry_ref, *, tiles_per_seq):
    j = pl.program_id(1)
    h = h_ref[...]
    u = _dot(h, wc_ref[...]) * _dot(h, wx_ref[...])
    y = _conv3_carry(u, cw_ref, _carry_in(carry_ref, tiles_per_seq))
    tail = u[u.shape[0] - SUBLANES:, :]
    carry_ref[j] = tail
    tail_ref[...] = tail
    s_ref[...] = (_dot(h, wb_ref[...]) * y).astype(s_ref.dtype)


def _inconv_sample_body(h_ref, wb_ref, wc_ref, wx_ref, cw_ref, e_ref, s_ref, u_ref, *, seg):
    h = h_ref[...]
    u = _dot(h, wc_ref[...]) * _dot(h, wx_ref[...])
    u_ref[...] = u
    y = _conv3_segments(u, cw_ref, e_ref[...], seg)
    s_ref[...] = (_dot(h, wb_ref[...]) * y).astype(s_ref.dtype)


def _merged_body(h_ref, a_ref, s_ref, wga_ref, wgc_ref, wab_ref, wcb_ref, o_ref):
    h = h_ref[...]
    attn_br = jax.nn.sigmoid(_dot(h, wga_ref[...])) * _dot(a_ref[...], wab_ref[...])
    conv_br = jax.nn.sigmoid(_dot(h, wgc_ref[...])) * _dot(s_ref[...], wcb_ref[...])
    o_ref[...] = (attn_br + conv_br).astype(o_ref.dtype)


def _mm_resid_body(a_ref, w_ref, x_ref, g_ref, o_ref):
    o_ref[...] = ALPHA * x_ref[...] + g_ref[...] * _dot(a_ref[...], w_ref[...])


def _layernorm(r, g, b):
    mu = jnp.mean(r, axis=-1, keepdims=True)
    d = r - mu
    var = jnp.mean(d * d, axis=-1, keepdims=True)
    return d * lax.rsqrt(var + LN_EPS) * g + b


def _ln_mod_body(r_ref, g_ref, b_ref, sc_ref, sh_ref, x_ref, h_ref):
    y = _layernorm(r_ref[...], g_ref[...], b_ref[...])
    x_ref[...] = y
    h_ref[...] = (y * (1.0 + sc_ref[...]) + sh_ref[...]).astype(h_ref.dtype)


def _ln_body(r_ref, g_ref, b_ref, x_ref):
    x_ref[...] = _layernorm(r_ref[...], g_ref[...], b_ref[...])


def _silu_mul(g, v):
    return g * jax.nn.sigmoid(g) * v


def _ffnup_prompt_body(h_ref, wg_ref, wv_ref, cwg_ref, cwv_ref, act_ref, tg_ref, tv_ref, cg_ref, cv_ref,
                       *, tiles_per_seq):
    j = pl.program_id(1)
    h = h_ref[...]
    ug = _dot(h, wg_ref[...])
    uv = _dot(h, wv_ref[...])
    yg = _conv3_carry(ug, cwg_ref, _carry_in(cg_ref, tiles_per_seq))
    yv = _conv3_carry(uv, cwv_ref, _carry_in(cv_ref, tiles_per_seq))
    tail_g = ug[ug.shape[0] - SUBLANES:, :]
    tail_v = uv[uv.shape[0] - SUBLANES:, :]
    cg_ref[j] = tail_g
    cv_ref[j] = tail_v
    tg_ref[...] = tail_g
    tv_ref[...] = tail_v
    act_ref[...] = _silu_mul(yg, yv).astype(act_ref.dtype)


def _ffnup_sample_body(h_ref, wg_ref, wv_ref, cwg_ref, cwv_ref, eg_ref, ev_ref, act_ref, ug_ref, uv_ref, *, seg):
    h = h_ref[...]
    ug = _dot(h, wg_ref[...])
    uv = _dot(h, wv_ref[...])
    ug_ref[...] = ug
    uv_ref[...] = uv
    yg = _conv3_segments(ug, cwg_ref, eg_ref[...], seg)
    yv = _conv3_segments(uv, cwv_ref, ev_ref[...], seg)
    act_ref[...] = _silu_mul(yg, yv).astype(act_ref.dtype)


class _Group:
    def __init__(self, batch, seq, tm, ada_rows):
        self.batch, self.seq, self.tm = batch, seq, tm
        self.m = batch * seq
        self.per_row = seq < tm
        self.ada = ada_rows

    def param_spec(self, chunk, tm, tn, col_axis=True):
        cb = chunk * (D_MODEL // tn)
        if self.per_row:
            if col_axis:
                return pl.BlockSpec((tm, tn), lambda i, j: (i, cb + j))
            return pl.BlockSpec((tm, tn), lambda i: (i, cb))
        tps = self.seq // tm
        if col_axis:
            return pl.BlockSpec((None, 1, tn), lambda i, j: (i // tps, 0, cb + j))
        return pl.BlockSpec((None, 1, tn), lambda i: (i // tps, 0, cb))


def _ada(c_all, w_ada, b_ada):
    tn = 512
    n = w_ada.shape[-1]
    return pl.pallas_call(
        _ada_body,
        out_shape=jax.ShapeDtypeStruct((DEPTH, ADA_ROWS, n), F32),
        grid=(DEPTH, n // tn),
        in_specs=[pl.BlockSpec((ADA_ROWS, D_MODEL), lambda l, j: (0, 0)),
                  pl.BlockSpec((None, D_MODEL, tn), lambda l, j: (l, 0, j)),
                  pl.BlockSpec((None, 1, tn), lambda l, j: (l, 0, j))],
        out_specs=pl.BlockSpec((None, ADA_ROWS, tn), lambda l, j: (l, 0, j)),
        compiler_params=_cp(2), name="ada",
    )(c_all, w_ada, b_ada.reshape(DEPTH, 1, n))


def _modulate(grp, x):
    tr = min(grp.tm, 512)
    return pl.pallas_call(
        _mod_body,
        out_shape=jax.ShapeDtypeStruct((grp.m, D_MODEL), BF),
        grid=(grp.m // tr,),
        in_specs=[pl.BlockSpec((tr, D_MODEL), lambda i: (i, 0)),
                  _row_param_spec(grp, SC1, tr), _row_param_spec(grp, SH1, tr)],
        out_specs=pl.BlockSpec((tr, D_MODEL), lambda i: (i, 0)),
        compiler_params=_cp(1), name="modulate",
    )(x, grp.ada, grp.ada)


def _row_param_spec(grp, chunk, tr):
    if grp.per_row:
        return pl.BlockSpec((tr, D_MODEL), lambda i: (i, chunk))
    tps = grp.seq // tr
    return pl.BlockSpec((None, 1, D_MODEL), lambda i: (i // tps, 0, chunk))


def _proj_rope(grp, h, w_in, l, col0, n_cols, cos, sin, n_rope, scale, out_dtype, name):
    tm, tn = grp.tm, 512
    cb = col0 // tn
    t_tiles = cos.shape[0] // tm
    return pl.pallas_call(
        functools.partial(_proj_rope_body, n_rope=n_rope, scale=scale),
        out_shape=jax.ShapeDtypeStruct((grp.m, n_cols), out_dtype),
        grid=(grp.m // tm, n_cols // tn),
        in_specs=[pl.BlockSpec((tm, D_MODEL), lambda i, j: (i, 0)),
                  pl.BlockSpec((None, D_MODEL, tn), lambda i, j: (l, 0, cb + j)),
                  pl.BlockSpec((tm, LANES), lambda i, j: (i % t_tiles, 0)),
                  pl.BlockSpec((tm, LANES), lambda i, j: (i % t_tiles, 0))],
        out_specs=pl.BlockSpec((tm, tn), lambda i, j: (i, j)),
        compiler_params=_cp(2), name=name,
    )(h, w_in, cos, sin)


def _attn_prompt(sinks, q, kv, batch, seq):
    nb = seq // WINDOW
    return pl.pallas_call(
        _attn_prompt_body,
        out_shape=jax.ShapeDtypeStruct(q.shape, BF),
        grid=(batch, nb),
        in_specs=[pl.BlockSpec(memory_space=pltpu.SMEM),
                  pl.BlockSpec((WINDOW, ATTN_DIM), lambda b, n: (b * nb + n, 0)),
                  pl.BlockSpec((WINDOW, 2 * KV_DIM), lambda b, n: (b * nb + jnp.maximum(n - 1, 0), 0)),
                  pl.BlockSpec((WINDOW, 2 * KV_DIM), lambda b, n: (b * nb + n, 0))],
        out_specs=pl.BlockSpec((WINDOW, ATTN_DIM), lambda b, n: (b * nb + n, 0)),
        compiler_params=_cp(2), name="attn_prompt",
    )(sinks, q, kv, kv)


def _attn_sample(sinks, q, kvn, cache_k, cache_v, l, batch, seq):
    return pl.pallas_call(
        _attn_sample_body,
        out_shape=(jax.ShapeDtypeStruct(q.shape, F32),
                   jax.ShapeDtypeStruct((batch, WINDOW, KV_DIM), F32),
                   jax.ShapeDtypeStruct((batch, WINDOW, KV_DIM), F32)),
        grid=(batch,),
        in_specs=[pl.BlockSpec(memory_space=pltpu.SMEM),
                  pl.BlockSpec((seq, ATTN_DIM), lambda b: (b, 0)),
                  pl.BlockSpec((seq, 2 * KV_DIM), lambda b: (b, 0)),
                  pl.BlockSpec((None, None, WINDOW, KV_DIM), lambda b: (l, b, 0, 0)),
                  pl.BlockSpec((None, None, WINDOW, KV_DIM), lambda b: (l, b, 0, 0))],
        out_specs=(pl.BlockSpec((seq, ATTN_DIM), lambda b: (b, 0)),
                   pl.BlockSpec((None, WINDOW, KV_DIM), lambda b: (b, 0, 0)),
                   pl.BlockSpec((None, WINDOW, KV_DIM), lambda b: (b, 0, 0))),
        compiler_params=_cp(1), name="attn_sample",
    )(sinks, q, kvn, cache_k, cache_v)


def _in_conv(grp, h, w_in, conv_w, l, past_rows):
    tm, tn = grp.tm, 256
    nj = CONV_DIM // tn
    w_spec = lambda col0: pl.BlockSpec((None, D_MODEL, tn), lambda i, j: (l, 0, col0 // tn + j))
    ins = [pl.BlockSpec((tm, D_MODEL), lambda i, j: (i, 0)),
           w_spec(COL_CB), w_spec(COL_CC), w_spec(COL_CX),
           pl.BlockSpec((None, 3, tn), lambda i, j: (l, 0, j))]
    s_shape = jax.ShapeDtypeStruct((grp.m, CONV_DIM), BF)
    s_spec = pl.BlockSpec((tm, tn), lambda i, j: (i, j))
    if grp.per_row:
        return pl.pallas_call(
            functools.partial(_inconv_sample_body, seg=grp.seq),
            out_shape=(s_shape, jax.ShapeDtypeStruct((grp.m, CONV_DIM), F32)),
            grid=(grp.m // tm, nj),
            in_specs=ins + [pl.BlockSpec((tm, tn), lambda i, j: (i, j))],
            out_specs=(s_spec, pl.BlockSpec((tm, tn), lambda i, j: (i, j))),
            compiler_params=_cp(2), name="in_conv_sample",
        )(h, w_in, w_in, w_in, conv_w, past_rows)
    return pl.pallas_call(
        functools.partial(_inconv_prompt_body, tiles_per_seq=grp.seq // tm),
        out_shape=(s_shape, jax.ShapeDtypeStruct((grp.m // tm, SUBLANES, CONV_DIM), F32)),
        grid=(grp.m // tm, nj),
        in_specs=ins,
        out_specs=(s_spec, pl.BlockSpec((None, SUBLANES, tn), lambda i, j: (i, 0, j))),
        scratch_shapes=[pltpu.VMEM((nj, SUBLANES, tn), F32)],
        compiler_params=_cp(2), name="in_conv_prompt",
    )(h, w_in, w_in, w_in, conv_w)


def _merged(grp, h, attn, s, w_in, w_attn_br, w_conv_br, l):
    tm, tn = grp.tm, 256
    row = lambda width: pl.BlockSpec((tm, width), lambda i, j: (i, 0))
    return pl.pallas_call(
        _merged_body,
        out_shape=jax.ShapeDtypeStruct((grp.m, D_MODEL), BF),
        grid=(grp.m // tm, D_MODEL // tn),
        in_specs=[row(D_MODEL), row(ATTN_DIM), row(CONV_DIM),
                  pl.BlockSpec((None, D_MODEL, tn), lambda i, j: (l, 0, COL_GA // tn + j)),
                  pl.BlockSpec((None, D_MODEL, tn), lambda i, j: (l, 0, COL_GC // tn + j)),
                  pl.BlockSpec((None, ATTN_DIM, tn), lambda i, j: (l, 0, j)),
                  pl.BlockSpec((None, CONV_DIM, tn), lambda i, j: (l, 0, j))],
        out_specs=pl.BlockSpec((tm, tn), lambda i, j: (i, j)),
        compiler_params=_cp(2), name="merged",
    )(h, attn, s, w_in, w_in, w_attn_br, w_conv_br)


def _mm_resid(grp, a, w, l, x, gate_chunk, tm, tn, name):
    k = a.shape[1]
    return pl.pallas_call(
        _mm_resid_body,
        out_shape=jax.ShapeDtypeStruct((grp.m, D_MODEL), F32),
        grid=(grp.m // tm, D_MODEL // tn),
        in_specs=[pl.BlockSpec((tm, k), lambda i, j: (i, 0)),
                  pl.BlockSpec((None, k, tn), lambda i, j: (l, 0, j)),
                  pl.BlockSpec((tm, tn), lambda i, j: (i, j)),
                  grp.param_spec(gate_chunk, tm, tn)],
        out_specs=pl.BlockSpec((tm, tn), lambda i, j: (i, j)),
        compiler_params=_cp(2), name=name,
    )(a, w, x, grp.ada)


def _ln(grp, r, g, b, l, mod_grp, mod_chunks, name):
    tr = 256
    row = pl.BlockSpec((tr, D_MODEL), lambda i: (i, 0))
    vec = pl.BlockSpec((None, 1, D_MODEL), lambda i: (l, 0, 0))
    g3 = g.reshape(DEPTH, 1, D_MODEL)
    b3 = b.reshape(DEPTH, 1, D_MODEL)
    if mod_chunks is None:
        return pl.pallas_call(
            _ln_body, out_shape=jax.ShapeDtypeStruct((grp.m, D_MODEL), F32), grid=(grp.m // tr,),
            in_specs=[row, vec, vec], out_specs=row, compiler_params=_cp(1), name=name,
        )(r, g3, b3), None
    sc, sh = mod_chunks
    return pl.pallas_call(
        _ln_mod_body,
        out_shape=(jax.ShapeDtypeStruct((grp.m, D_MODEL), F32), jax.ShapeDtypeStruct((grp.m, D_MODEL), BF)),
        grid=(grp.m // tr,),
        in_specs=[row, vec, vec, _row_param_spec(mod_grp, sc, tr), _row_param_spec(mod_grp, sh, tr)],
        out_specs=(row, row), compiler_params=_cp(1), name=name,
    )(r, g3, b3, mod_grp.ada, mod_grp.ada)


def _ffn_up(grp, h, w_up, ffn_conv_w, l, past_g, past_v):
    tm, tn = grp.tm, 256
    nj = D_FF // tn
    ins = [pl.BlockSpec((tm, D_MODEL), lambda i, j: (i, 0)),
           pl.BlockSpec((None, D_MODEL, tn), lambda i, j: (l, 0, j)),
           pl.BlockSpec((None, D_MODEL, tn), lambda i, j: (l, 0, nj + j)),
           pl.BlockSpec((None, 3, tn), lambda i, j: (l, 0, j)),
           pl.BlockSpec((None, 3, tn), lambda i, j: (l, 0, nj + j))]
    tile = pl.BlockSpec((tm, tn), lambda i, j: (i, j))
    act_shape = jax.ShapeDtypeStruct((grp.m, D_FF), BF)
    if grp.per_row:
        full = jax.ShapeDtypeStruct((grp.m, D_FF), F32)
        return pl.pallas_call(
            functools.partial(_ffnup_sample_body, seg=grp.seq),
            out_shape=(act_shape, full, full), grid=(grp.m // tm, nj),
            in_specs=ins + [tile, tile], out_specs=(tile, tile, tile),
            compiler_params=_cp(2), name="ffn_up_sample",
        )(h, w_up, w_up, ffn_conv_w, ffn_conv_w, past_g, past_v)
    tail = jax.ShapeDtypeStruct((grp.m // tm, SUBLANES, D_FF), F32)
    tail_spec = pl.BlockSpec((None, SUBLANES, tn), lambda i, j: (i, 0, j))
    return pl.pallas_call(
        functools.partial(_ffnup_prompt_body, tiles_per_seq=grp.seq // tm),
        out_shape=(act_shape, tail, tail), grid=(grp.m // tm, nj),
        in_specs=ins, out_specs=(tile, tail_spec, tail_spec),
        scratch_shapes=[pltpu.VMEM((nj, SUBLANES, tn), F32), pltpu.VMEM((nj, SUBLANES, tn), F32)],
        compiler_params=_cp(2), name="ffn_up_prompt",
    )(h, w_up, w_up, ffn_conv_w, ffn_conv_w)


def _rope_tables(pos):
    inv = ROPE_THETA ** (-jnp.arange(0, HEAD_DIM, 2, dtype=F32) / HEAD_DIM)
    ang = pos.astype(F32)[:, None] * inv[None, :]
    cos, sin = jnp.cos(ang), jnp.sin(ang)
    return jnp.concatenate([cos] * 4, axis=-1), jnp.concatenate([-sin, sin, -sin, sin], axis=-1)


def _expand_past(past, seq):
    batch, _, n = past.shape
    return jnp.concatenate([past, jnp.zeros((batch, seq - 2, n), past.dtype)], axis=1).reshape(batch * seq, n)


def _last_rows(x, batch, seq, n_rows):
    return x.reshape(batch, seq, x.shape[-1])[:, seq - n_rows:, :]


def _layer(grp, next_grp, l, x, h, rope, weights, sinks, past):
    (w_in, conv_w, w_attn_br, w_conv_br, w_out, ln1_g, ln1_b, w_up, ffn_conv_w, w_down, ln2_g, ln2_b) = weights
    cos, sin = rope
    sample = grp.per_row
    q = _proj_rope(grp, h, w_in, l, COL_Q, ATTN_DIM, cos, sin, ATTN_DIM // LANES, HEAD_DIM ** -0.5,
                   F32 if sample else BF, "proj_q")
    kv = _proj_rope(grp, h, w_in, l, COL_KV, 2 * KV_DIM, cos, sin, KV_DIM // LANES, 1.0, F32, "proj_kv")
    if sample:
        cache_k, cache_v, state_conv, state_ffn = past
        attn, k_state, v_state = _attn_sample(sinks, q, kv, cache_k, cache_v, l, grp.batch, grp.seq)
        s, u = _in_conv(grp, h, w_in, conv_w, l, _expand_past(state_conv[l], grp.seq))
        conv_state = _last_rows(u, grp.batch, grp.seq, 2)
    else:
        attn = _attn_prompt(sinks, q, kv, grp.batch, grp.seq)
        kv_tail = _last_rows(kv, grp.batch, grp.seq, WINDOW)
        k_state, v_state = kv_tail[..., :KV_DIM], kv_tail[..., KV_DIM:]
        s, tails = _in_conv(grp, h, w_in, conv_w, l, None)
        tps = grp.seq // grp.tm
        conv_state = tails[tps - 1::tps, SUBLANES - 2:, :]
    merged = _merged(grp, h, attn, s, w_in, w_attn_br, w_conv_br, l)
    r1 = _mm_resid(grp, merged, w_out, l, x, G1, grp.tm, 512, "out_proj")
    x1, h2 = _ln(grp, r1, ln1_g, ln1_b, l, grp, (SC2, SH2), "ln1")
    if sample:
        eg = _expand_past(state_ffn[l][..., :D_FF], grp.seq)
        ev = _expand_past(state_ffn[l][..., D_FF:], grp.seq)
        act, ug, uv = _ffn_up(grp, h2, w_up, ffn_conv_w, l, eg, ev)
        ffn_state = _last_rows(jnp.concatenate([ug, uv], axis=-1), grp.batch, grp.seq, 2)
    else:
        act, tg, tv = _ffn_up(grp, h2, w_up, ffn_conv_w, l, None, None)
        ffn_state = jnp.concatenate([tg, tv], axis=-1)[tps - 1::tps, SUBLANES - 2:, :]
    r2 = _mm_resid(grp, act, w_down, l, x1, G2, min(grp.tm, 512), 512, "ffn_down")
    x2, h_next = _ln(grp, r2, ln2_g, ln2_b, l, next_grp, None if next_grp is None else (SC1, SH1), "ln2")
    shape4 = (grp.batch, WINDOW, N_KV_HEADS, HEAD_DIM)
    return x2, h_next, k_state.reshape(shape4), v_state.reshape(shape4), conv_state, ffn_state


def kernel(x_prompt, x_sample, cache_attn_k, cache_attn_v, state_conv, state_ffn_conv, c_prompt, c_sample, w_ada, b_ada, w_in, attn_sinks, conv_w, w_attn_br, w_conv_br, w_out, ln1_g, ln1_b, w_up, ffn_conv_w, w_down, ln2_g, ln2_b):
    bp, tp, _ = x_prompt.shape
    bs, ts, _ = x_sample.shape
    assert bp <= ADA_SAMPLE_ROW0 and ADA_SAMPLE_ROW0 + bs == ADA_ROWS

    c_all = jnp.concatenate([c_prompt, jnp.zeros((ADA_SAMPLE_ROW0 - bp, D_MODEL), F32), c_sample], axis=0)
    ada = _ada(c_all, w_ada, b_ada)

    weights = (w_in.astype(BF), conv_w, w_attn_br.astype(BF), w_conv_br.astype(BF), w_out.astype(BF),
               ln1_g, ln1_b, w_up.astype(BF), ffn_conv_w, w_down.astype(BF), ln2_g, ln2_b)

    rope_p = _rope_tables(jnp.arange(tp, dtype=jnp.int32))
    rope_s = tuple(jnp.tile(t, (bs, 1)) for t in _rope_tables(PAST_LEN + jnp.arange(ts, dtype=jnp.int32)))
    cache_k = cache_attn_k.reshape(DEPTH, bs, WINDOW, KV_DIM)
    cache_v = cache_attn_v.reshape(DEPTH, bs, WINDOW, KV_DIM)
    past = (cache_k, cache_v, state_conv, state_ffn_conv)

    xp = x_prompt.reshape(bp * tp, D_MODEL)
    xs = x_sample.reshape(bs * ts, D_MODEL)
    gps = [_Group(bp, tp, 1024, ada[l, :bp].reshape(bp, 1, -1)) for l in range(DEPTH)] + [None]
    gss = [_Group(bs, ts, bs * ts, jnp.repeat(ada[l, ADA_SAMPLE_ROW0:], ts, axis=0)) for l in range(DEPTH)] + [None]
    hp = _modulate(gps[0], xp)
    hs = _modulate(gss[0], xs)
    outs_p, outs_s = [], []
    for l in range(DEPTH):
        xp, hp, *st_p = _layer(gps[l], gps[l + 1], l, xp, hp, rope_p, weights, attn_sinks[l], None)
        xs, hs, *st_s = _layer(gss[l], gss[l + 1], l, xs, hs, rope_s, weights, attn_sinks[l], past)
        outs_p.append(st_p)
        outs_s.append(st_s)
    stack = lambda outs, k: jnp.stack([o[k] for o in outs])
    return (xp.reshape(bp, tp, D_MODEL), xs.reshape(bs, ts, D_MODEL),
            stack(outs_p, 0), stack(outs_p, 1), stack(outs_p, 2), stack(outs_p, 3),
            stack(outs_s, 0), stack(outs_s, 1), stack(outs_s, 2), stack(outs_s, 3))
```

```python
import functools

import jax
import jax.numpy as jnp
from jax import lax
from jax.experimental import pallas as pl
from jax.experimental.pallas import tpu as pltpu

D_MODEL = 4096
DEPTH = 2
HEAD_DIM = 64
N_HEADS = 32
N_KV_HEADS = 4
ATTN_DIM = N_HEADS * HEAD_DIM
KV_DIM = N_KV_HEADS * HEAD_DIM
WINDOW = 128
ROPE_THETA = 10000.0
CONV_DIM = D_MODEL // 2
D_FF = 11008
PAST_LEN = 16384
LN_EPS = 1e-5
ALPHA = (2 * DEPTH) ** 0.25

COL_Q = 0
COL_KV = ATTN_DIM
COL_CB = ATTN_DIM + 2 * KV_DIM
COL_CC = COL_CB + CONV_DIM
COL_CX = COL_CC + CONV_DIM
COL_GA = COL_CX + CONV_DIM
COL_GC = COL_GA + D_MODEL

SH1, SC1, G1, SH2, SC2, G2 = range(6)

LANES = 128
SUBLANES = 8
ADA_ROWS = 40
ADA_SAMPLE_ROW0 = 8
ROW_CHUNK = 256
VMEM_LIMIT = 56 * 2**20
NEG = -1e30

BF = jnp.bfloat16
F32 = jnp.float32


def _cp(n_axes):
    return pltpu.CompilerParams(dimension_semantics=("arbitrary",) * n_axes, vmem_limit_bytes=VMEM_LIMIT)


def _bf(x):
    return x if x.dtype == BF else x.astype(BF)


def _dot(a, b):
    return jnp.dot(_bf(a), _bf(b), preferred_element_type=F32)


def _row_chunks(tm):
    assert tm % ROW_CHUNK == 0
    return [pl.ds(r, ROW_CHUNK) for r in range(0, tm, ROW_CHUNK)]


def _rows_of(ref, rows):
    return ref[...] if ref.shape[0] == 1 else ref[rows, :]


def _ada_body(c_ref, w_ref, b_ref, o_ref):
    c = c_ref[...]
    o_ref[...] = _dot(c * jax.nn.sigmoid(c), w_ref[...]) + b_ref[...]


def _mod_body(x_ref, sc_ref, sh_ref, o_ref):
    o_ref[...] = (x_ref[...] * (1.0 + sc_ref[...]) + sh_ref[...]).astype(o_ref.dtype)


def _proj_rope_body(h_ref, w_ref, cos_ref, sin_ref, o_ref, *, n_rope, scale):
    tn = o_ref.shape[1]
    lane = lax.broadcasted_iota(jnp.int32, (ROW_CHUNK, LANES), 1)
    first_half = (lane & (HEAD_DIM - 1)) < HEAD_DIM // 2
    for rows in _row_chunks(h_ref.shape[0]):
        acc = _dot(h_ref[rows, :], w_ref[...])
        cos = cos_ref[rows, :]
        sin = sin_ref[rows, :]
        for c in range(tn // LANES):
            ch = acc[:, c * LANES:(c + 1) * LANES]
            if c < n_rope:
                rot = jnp.where(first_half,
                                pltpu.roll(ch, LANES - HEAD_DIM // 2, axis=1),
                                pltpu.roll(ch, HEAD_DIM // 2, axis=1))
                ch = ch * cos + rot * sin
            if scale != 1.0:
                ch = ch * scale
            o_ref[rows, c * LANES:(c + 1) * LANES] = ch.astype(o_ref.dtype)


def _attend(q_ref, k2, v2, valid, sink_ref, o_ref):
    tq = q_ref.shape[0]
    nk = k2.shape[0]
    lo = lax.broadcasted_iota(jnp.int32, (nk, LANES), 1) < HEAD_DIM
    lo_q = lax.broadcasted_iota(jnp.int32, (tq, LANES), 1) < HEAD_DIM
    group = N_HEADS // N_KV_HEADS
    for m in range(KV_DIM // LANES):
        kc = k2[:, m * LANES:(m + 1) * LANES]
        vc = v2[:, m * LANES:(m + 1) * LANES]
        kr = pltpu.roll(kc, HEAD_DIM, axis=1)
        vr = pltpu.roll(vc, HEAD_DIM, axis=1)
        for hh in range(2):
            kv_head = 2 * m + hh
            k_lo, k_hi = (kc, kr) if hh == 0 else (kr, kc)
            v_lo, v_hi = (vc, vr) if hh == 0 else (vr, vc)
            kbd = jnp.concatenate([jnp.where(lo, k_lo, 0.0), jnp.where(lo, 0.0, k_hi)], axis=0).astype(BF)
            vbd = jnp.concatenate([jnp.where(lo, v_lo, 0.0), jnp.where(lo, 0.0, v_hi)], axis=0).astype(BF)
            pairs = [kv_head * (group // 2) + pp for pp in range(group // 2)]
            qs = _bf(jnp.concatenate([q_ref[:, p * LANES:(p + 1) * LANES] for p in pairs], axis=0))
            s_all = lax.dot_general(qs, kbd, (((1,), (1,)), ((), ())), preferred_element_type=F32)
            es, inv = [], []
            for pp, p in enumerate(pairs):
                s = s_all[pp * tq:(pp + 1) * tq, :]
                e_pair, d_pair = [], []
                for half in range(2):
                    sh = jnp.where(valid, s[:, half * nk:(half + 1) * nk], NEG)
                    sink = sink_ref[2 * p + half]
                    mx = jnp.maximum(jnp.max(sh, axis=-1, keepdims=True), sink)
                    e = jnp.exp(sh - mx)
                    e_pair.append(e)
                    d_pair.append(jnp.sum(e, axis=-1, keepdims=True) + jnp.exp(sink - mx))
                es.append(jnp.concatenate(e_pair, axis=1))
                inv.append(jnp.where(lo_q, 1.0 / d_pair[0], 1.0 / d_pair[1]))
            o_all = jnp.dot(_bf(jnp.concatenate(es, axis=0)), vbd, preferred_element_type=F32)
            for pp, p in enumerate(pairs):
                o = o_all[pp * tq:(pp + 1) * tq, :] * inv[pp]
                o_ref[:, p * LANES:(p + 1) * LANES] = o.astype(o_ref.dtype)


def _attn_prompt_body(sink_ref, q_ref, kvp_ref, kvc_ref, o_ref):
    n = pl.program_id(1)
    kvp = kvp_ref[...]
    kvc = kvc_ref[...]
    k2 = jnp.concatenate([kvp[:, :KV_DIM], kvc[:, :KV_DIM]], axis=0)
    v2 = jnp.concatenate([kvp[:, KV_DIM:], kvc[:, KV_DIM:]], axis=0)
    i = lax.broadcasted_iota(jnp.int32, (WINDOW, 2 * WINDOW), 0)
    j = lax.broadcasted_iota(jnp.int32, (WINDOW, 2 * WINDOW), 1)
    valid = (j >= i) & (j <= i + WINDOW) & ((j >= WINDOW) | (n > 0))
    _attend(q_ref, k2, v2, valid, sink_ref, o_ref)


def _attn_sample_body(sink_ref, q_ref, kvn_ref, ck_ref, cv_ref, o_ref, ks_ref, vs_ref):
    t = q_ref.shape[0]
    ck = ck_ref[...]
    cv = cv_ref[...]
    kvn = kvn_ref[...]
    pad = jnp.zeros((WINDOW - t, KV_DIM), F32)
    k2 = jnp.concatenate([ck, kvn[:, :KV_DIM], pad], axis=0)
    v2 = jnp.concatenate([cv, kvn[:, KV_DIM:], pad], axis=0)
    i = lax.broadcasted_iota(jnp.int32, (t, 2 * WINDOW), 0)
    j = lax.broadcasted_iota(jnp.int32, (t, 2 * WINDOW), 1)
    valid = (j >= i) & (j <= i + WINDOW)
    _attend(q_ref, k2, v2, valid, sink_ref, o_ref)
    ks_ref[0:WINDOW - t, :] = ck[t:, :]
    ks_ref[WINDOW - t:, :] = kvn[:, :KV_DIM]
    vs_ref[0:WINDOW - t, :] = cv[t:, :]
    vs_ref[WINDOW - t:, :] = kvn[:, KV_DIM:]


def _conv3_carry(u, w_ref, prev):
    ext = jnp.concatenate([prev, u], axis=0)
    x1 = pltpu.roll(ext, 1, axis=0)[SUBLANES:, :]
    x2 = pltpu.roll(ext, 2, axis=0)[SUBLANES:, :]
    return w_ref[0:1, :] * x2 + w_ref[1:2, :] * x1 + w_ref[2:3, :] * u


def _conv3_segments(u, w_ref, e, seg):
    t = lax.broadcasted_iota(jnp.int32, u.shape, 0) & (seg - 1)
    x1 = jnp.where(t >= 1, pltpu.roll(u, 1, axis=0), pltpu.roll(e, u.shape[0] - 1, axis=0))
    x2 = jnp.where(t >= 2, pltpu.roll(u, 2, axis=0), e)
    return w_ref[0:1, :] * x2 + w_ref[1:2, :] * x1 + w_ref[2:3, :] * u


def _carry_in(carry_ref, tiles_per_seq):
    i = pl.program_id(0)
    j = pl.program_id(1)

    @pl.when(i % tiles_per_seq == 0)
    def _():
        carry_ref[j] = jnp.zeros(carry_ref.shape[1:], F32)

    return carry_ref[j]


def _inconv_prompt_body(h_ref, wb_ref, wc_ref, wx_ref, cw_ref, s_ref, tail_ref, carry_ref, *, tiles_per_seq):
    j = pl.program_id(1)
    prev = _carry_in(carry_ref, tiles_per_seq)
    for rows in _row_chunks(h_ref.shape[0]):
        h = h_ref[rows, :]
        u = _dot(h, wc_ref[...]) * _dot(h, wx_ref[...])
        y = _conv3_carry(u, cw_ref, prev)
        prev = u[ROW_CHUNK - SUBLANES:, :]
        s_ref[rows, :] = (_dot(h, wb_ref[...]) * y).astype(s_ref.dtype)
    carry_ref[j] = prev
    tail_ref[...] = prev


def _inconv_sample_body(h_ref, wb_ref, wc_ref, wx_ref, cw_ref, e_ref, s_ref, u_ref, *, seg):
    h = h_ref[...]
    u = _dot(h, wc_ref[...]) * _dot(h, wx_ref[...])
    u_ref[...] = u
    y = _conv3_segments(u, cw_ref, e_ref[...], seg)
    s_ref[...] = (_dot(h, wb_ref[...]) * y).astype(s_ref.dtype)


def _merged_body(h_ref, a_ref, s_ref, wga_ref, wgc_ref, wab_ref, wcb_ref, o_ref):
    for rows in _row_chunks(h_ref.shape[0]):
        h = h_ref[rows, :]
        attn_br = jax.nn.sigmoid(_dot(h, wga_ref[...])) * _dot(a_ref[rows, :], wab_ref[...])
        conv_br = jax.nn.sigmoid(_dot(h, wgc_ref[...])) * _dot(s_ref[rows, :], wcb_ref[...])
        o_ref[rows, :] = (attn_br + conv_br).astype(o_ref.dtype)


def _mm_resid_body(a_ref, w_ref, x_ref, g_ref, o_ref):
    for rows in _row_chunks(a_ref.shape[0]):
        o_ref[rows, :] = ALPHA * x_ref[rows, :] + _rows_of(g_ref, rows) * _dot(a_ref[rows, :], w_ref[...])


def _layernorm(r, g, b):
    mu = jnp.mean(r, axis=-1, keepdims=True)
    d = r - mu
    var = jnp.mean(d * d, axis=-1, keepdims=True)
    return d * lax.rsqrt(var + LN_EPS) * g + b


def _ln_mod_body(r_ref, g_ref, b_ref, sc_ref, sh_ref, x_ref, h_ref):
    y = _layernorm(r_ref[...], g_ref[...], b_ref[...])
    x_ref[...] = y
    h_ref[...] = (y * (1.0 + sc_ref[...]) + sh_ref[...]).astype(h_ref.dtype)


def _ln_body(r_ref, g_ref, b_ref, x_ref):
    x_ref[...] = _layernorm(r_ref[...], g_ref[...], b_ref[...])


def _silu_mul(g, v):
    return g * jax.nn.sigmoid(g) * v


def _ffnup_prompt_body(h_ref, wg_ref, wv_ref, cwg_ref, cwv_ref, act_ref, tg_ref, tv_ref, cg_ref, cv_ref,
                       *, tiles_per_seq):
    j = pl.program_id(1)
    prev_g = _carry_in(cg_ref, tiles_per_seq)
    prev_v = _carry_in(cv_ref, tiles_per_seq)
    for rows in _row_chunks(h_ref.shape[0]):
        h = h_ref[rows, :]
        ug = _dot(h, wg_ref[...])
        uv = _dot(h, wv_ref[...])
        yg = _conv3_carry(ug, cwg_ref, prev_g)
        yv = _conv3_carry(uv, cwv_ref, prev_v)
        prev_g = ug[ROW_CHUNK - SUBLANES:, :]
        prev_v = uv[ROW_CHUNK - SUBLANES:, :]
        act_ref[rows, :] = _silu_mul(yg, yv).astype(act_ref.dtype)
    cg_ref[j] = prev_g
    cv_ref[j] = prev_v
    tg_ref[...] = prev_g
    tv_ref[...] = prev_v


def _ffnup_sample_body(h_ref, wg_ref, wv_ref, cwg_ref, cwv_ref, eg_ref, ev_ref, act_ref, ug_ref, uv_ref, *, seg):
    h = h_ref[...]
    ug = _dot(h, wg_ref[...])
    uv = _dot(h, wv_ref[...])
    ug_ref[...] = ug
    uv_ref[...] = uv
    yg = _conv3_segments(ug, cwg_ref, eg_ref[...], seg)
    yv = _conv3_segments(uv, cwv_ref, ev_ref[...], seg)
    act_ref[...] = _silu_mul(yg, yv).astype(act_ref.dtype)


class _Group:
    def __init__(self, batch, seq, tm, tm_big, ada_rows):
        self.batch, self.seq, self.tm, self.tm_big = batch, seq, tm, tm_big
        self.m = batch * seq
        self.per_row = seq < tm
        self.ada = ada_rows

    def param_spec(self, chunk, tm, tn):
        cb = chunk * (D_MODEL // tn)
        if self.per_row:
            return pl.BlockSpec((tm, tn), lambda i, j: (i, cb + j))
        tps = self.seq // tm
        return pl.BlockSpec((None, 1, tn), lambda i, j: (i // tps, 0, cb + j))


def _ada(c_all, w_ada, b_ada):
    tn = 512
    n = w_ada.shape[-1]
    return pl.pallas_call(
        _ada_body,
        out_shape=jax.ShapeDtypeStruct((DEPTH, ADA_ROWS, n), F32),
        grid=(DEPTH, n // tn),
        in_specs=[pl.BlockSpec((ADA_ROWS, D_MODEL), lambda l, j: (0, 0)),
                  pl.BlockSpec((None, D_MODEL, tn), lambda l, j: (l, 0, j)),
                  pl.BlockSpec((None, 1, tn), lambda l, j: (l, 0, j))],
        out_specs=pl.BlockSpec((None, ADA_ROWS, tn), lambda l, j: (l, 0, j)),
        compiler_params=_cp(2), name="ada",
    )(c_all, w_ada, b_ada.reshape(DEPTH, 1, n))


def _modulate(grp, x):
    tr = min(grp.tm, 512)
    return pl.pallas_call(
        _mod_body,
        out_shape=jax.ShapeDtypeStruct((grp.m, D_MODEL), BF),
        grid=(grp.m // tr,),
        in_specs=[pl.BlockSpec((tr, D_MODEL), lambda i: (i, 0)),
                  _row_param_spec(grp, SC1, tr), _row_param_spec(grp, SH1, tr)],
        out_specs=pl.BlockSpec((tr, D_MODEL), lambda i: (i, 0)),
        compiler_params=_cp(1), name="modulate",
    )(x, grp.ada, grp.ada)


def _row_param_spec(grp, chunk, tr):
    if grp.per_row:
        return pl.BlockSpec((tr, D_MODEL), lambda i: (i, chunk))
    tps = grp.seq // tr
    return pl.BlockSpec((None, 1, D_MODEL), lambda i: (i // tps, 0, chunk))


def _proj_rope(grp, h, w_in, l, col0, n_cols, cos, sin, n_rope, scale, out_dtype, name):
    tm, tn = grp.tm_big, 512
    cb = col0 // tn
    t_tiles = cos.shape[0] // tm
    return pl.pallas_call(
        functools.partial(_proj_rope_body, n_rope=n_rope, scale=scale),
        out_shape=jax.ShapeDtypeStruct((grp.m, n_cols), out_dtype),
        grid=(grp.m // tm, n_cols // tn),
        in_specs=[pl.BlockSpec((tm, D_MODEL), lambda i, j: (i, 0)),
                  pl.BlockSpec((None, D_MODEL, tn), lambda i, j: (l, 0, cb + j)),
                  pl.BlockSpec((tm, LANES), lambda i, j: (i % t_tiles, 0)),
                  pl.BlockSpec((tm, LANES), lambda i, j: (i % t_tiles, 0))],
        out_specs=pl.BlockSpec((tm, tn), lambda i, j: (i, j)),
        compiler_params=_cp(2), name=name,
    )(h, w_in, cos, sin)


def _attn_prompt(sinks, q, kv, batch, seq):
    nb = seq // WINDOW
    return pl.pallas_call(
        _attn_prompt_body,
        out_shape=jax.ShapeDtypeStruct(q.shape, BF),
        grid=(batch, nb),
        in_specs=[pl.BlockSpec(memory_space=pltpu.SMEM),
                  pl.BlockSpec((WINDOW, ATTN_DIM), lambda b, n: (b * nb + n, 0)),
                  pl.BlockSpec((WINDOW, 2 * KV_DIM), lambda b, n: (b * nb + jnp.maximum(n - 1, 0), 0)),
                  pl.BlockSpec((WINDOW, 2 * KV_DIM), lambda b, n: (b * nb + n, 0))],
        out_specs=pl.BlockSpec((WINDOW, ATTN_DIM), lambda b, n: (b * nb + n, 0)),
        compiler_params=_cp(2), name="attn_prompt",
    )(sinks, q, kv, kv)


def _attn_sample(sinks, q, kvn, cache_k, cache_v, l, batch, seq):
    return pl.pallas_call(
        _attn_sample_body,
        out_shape=(jax.ShapeDtypeStruct(q.shape, F32),
                   jax.ShapeDtypeStruct((batch, WINDOW, KV_DIM), F32),
                   jax.ShapeDtypeStruct((batch, WINDOW, KV_DIM), F32)),
        grid=(batch,),
        in_specs=[pl.BlockSpec(memory_space=pltpu.SMEM),
                  pl.BlockSpec((seq, ATTN_DIM), lambda b: (b, 0)),
                  pl.BlockSpec((seq, 2 * KV_DIM), lambda b: (b, 0)),
                  pl.BlockSpec((None, None, WINDOW, KV_DIM), lambda b: (l, b, 0, 0)),
                  pl.BlockSpec((None, None, WINDOW, KV_DIM), lambda b: (l, b, 0, 0))],
        out_specs=(pl.BlockSpec((seq, ATTN_DIM), lambda b: (b, 0)),
                   pl.BlockSpec((None, WINDOW, KV_DIM), lambda b: (b, 0, 0)),
                   pl.BlockSpec((None, WINDOW, KV_DIM), lambda b: (b, 0, 0))),
        compiler_params=_cp(1), name="attn_sample",
    )(sinks, q, kvn, cache_k, cache_v)


def _in_conv(grp, h, w_in, conv_w, l, past_rows):
    tm, tn = grp.tm_big, 256
    nj = CONV_DIM // tn
    w_spec = lambda col0: pl.BlockSpec((None, D_MODEL, tn), lambda i, j: (l, 0, col0 // tn + j))
    ins = [pl.BlockSpec((tm, D_MODEL), lambda i, j: (i, 0)),
           w_spec(COL_CB), w_spec(COL_CC), w_spec(COL_CX),
           pl.BlockSpec((None, 3, tn), lambda i, j: (l, 0, j))]
    s_shape = jax.ShapeDtypeStruct((grp.m, CONV_DIM), BF)
    s_spec = pl.BlockSpec((tm, tn), lambda i, j: (i, j))
    if grp.per_row:
        return pl.pallas_call(
            functools.partial(_inconv_sample_body, seg=grp.seq),
            out_shape=(s_shape, jax.ShapeDtypeStruct((grp.m, CONV_DIM), F32)),
            grid=(grp.m // tm, nj),
            in_specs=ins + [pl.BlockSpec((tm, tn), lambda i, j: (i, j))],
            out_specs=(s_spec, pl.BlockSpec((tm, tn), lambda i, j: (i, j))),
            compiler_params=_cp(2), name="in_conv_sample",
        )(h, w_in, w_in, w_in, conv_w, past_rows)
    return pl.pallas_call(
        functools.partial(_inconv_prompt_body, tiles_per_seq=grp.seq // tm),
        out_shape=(s_shape, jax.ShapeDtypeStruct((grp.m // tm, SUBLANES, CONV_DIM), F32)),
        grid=(grp.m // tm, nj),
        in_specs=ins,
        out_specs=(s_spec, pl.BlockSpec((None, SUBLANES, tn), lambda i, j: (i, 0, j))),
        scratch_shapes=[pltpu.VMEM((nj, SUBLANES, tn), F32)],
        compiler_params=_cp(2), name="in_conv_prompt",
    )(h, w_in, w_in, w_in, conv_w)


def _merged(grp, h, attn, s, w_in, w_attn_br, w_conv_br, l):
    tm, tn = grp.tm, 256
    row = lambda width: pl.BlockSpec((tm, width), lambda i, j: (i, 0))
    return pl.pallas_call(
        _merged_body,
        out_shape=jax.ShapeDtypeStruct((grp.m, D_MODEL), BF),
        grid=(grp.m // tm, D_MODEL // tn),
        in_specs=[row(D_MODEL), row(ATTN_DIM), row(CONV_DIM),
                  pl.BlockSpec((None, D_MODEL, tn), lambda i, j: (l, 0, COL_GA // tn + j)),
                  pl.BlockSpec((None, D_MODEL, tn), lambda i, j: (l, 0, COL_GC // tn + j)),
                  pl.BlockSpec((None, ATTN_DIM, tn), lambda i, j: (l, 0, j)),
                  pl.BlockSpec((None, CONV_DIM, tn), lambda i, j: (l, 0, j))],
        out_specs=pl.BlockSpec((tm, tn), lambda i, j: (i, j)),
        compiler_params=_cp(2), name="merged",
    )(h, attn, s, w_in, w_in, w_attn_br, w_conv_br)


def _mm_resid(grp, a, w, l, x, gate_chunk, tm, tn, name):
    k = a.shape[1]
    return pl.pallas_call(
        _mm_resid_body,
        out_shape=jax.ShapeDtypeStruct((grp.m, D_MODEL), F32),
        grid=(grp.m // tm, D_MODEL // tn),
        in_specs=[pl.BlockSpec((tm, k), lambda i, j: (i, 0)),
                  pl.BlockSpec((None, k, tn), lambda i, j: (l, 0, j)),
                  pl.BlockSpec((tm, tn), lambda i, j: (i, j)),
                  grp.param_spec(gate_chunk, tm, tn)],
        out_specs=pl.BlockSpec((tm, tn), lambda i, j: (i, j)),
        compiler_params=_cp(2), name=name,
    )(a, w, x, grp.ada)


def _ln(grp, r, g, b, l, mod_grp, mod_chunks, name):
    tr = 256
    row = pl.BlockSpec((tr, D_MODEL), lambda i: (i, 0))
    vec = pl.BlockSpec((None, 1, D_MODEL), lambda i: (l, 0, 0))
    g3 = g.reshape(DEPTH, 1, D_MODEL)
    b3 = b.reshape(DEPTH, 1, D_MODEL)
    if mod_chunks is None:
        return pl.pallas_call(
            _ln_body, out_shape=jax.ShapeDtypeStruct((grp.m, D_MODEL), F32), grid=(grp.m // tr,),
            in_specs=[row, vec, vec], out_specs=row, compiler_params=_cp(1), name=name,
        )(r, g3, b3), None
    sc, sh = mod_chunks
    return pl.pallas_call(
        _ln_mod_body,
        out_shape=(jax.ShapeDtypeStruct((grp.m, D_MODEL), F32), jax.ShapeDtypeStruct((grp.m, D_MODEL), BF)),
        grid=(grp.m // tr,),
        in_specs=[row, vec, vec, _row_param_spec(mod_grp, sc, tr), _row_param_spec(mod_grp, sh, tr)],
        out_specs=(row, row), compiler_params=_cp(1), name=name,
    )(r, g3, b3, mod_grp.ada, mod_grp.ada)


def _ffn_up(grp, h, w_up, ffn_conv_w, l, past_g, past_v):
    tm, tn = grp.tm_big, 256
    nj = D_FF // tn
    ins = [pl.BlockSpec((tm, D_MODEL), lambda i, j: (i, 0)),
           pl.BlockSpec((None, D_MODEL, tn), lambda i, j: (l, 0, j)),
           pl.BlockSpec((None, D_MODEL, tn), lambda i, j: (l, 0, nj + j)),
           pl.BlockSpec((None, 3, tn), lambda i, j: (l, 0, j)),
           pl.BlockSpec((None, 3, tn), lambda i, j: (l, 0, nj + j))]
    tile = pl.BlockSpec((tm, tn), lambda i, j: (i, j))
    act_shape = jax.ShapeDtypeStruct((grp.m, D_FF), BF)
    if grp.per_row:
        full = jax.ShapeDtypeStruct((grp.m, D_FF), F32)
        return pl.pallas_call(
            functools.partial(_ffnup_sample_body, seg=grp.seq),
            out_shape=(act_shape, full, full), grid=(grp.m // tm, nj),
            in_specs=ins + [tile, tile], out_specs=(tile, tile, tile),
            compiler_params=_cp(2), name="ffn_up_sample",
        )(h, w_up, w_up, ffn_conv_w, ffn_conv_w, past_g, past_v)
    tail = jax.ShapeDtypeStruct((grp.m // tm, SUBLANES, D_FF), F32)
    tail_spec = pl.BlockSpec((None, SUBLANES, tn), lambda i, j: (i, 0, j))
    return pl.pallas_call(
        functools.partial(_ffnup_prompt_body, tiles_per_seq=grp.seq // tm),
        out_shape=(act_shape, tail, tail), grid=(grp.m // tm, nj),
        in_specs=ins, out_specs=(tile, tail_spec, tail_spec),
        scratch_shapes=[pltpu.VMEM((nj, SUBLANES, tn), F32), pltpu.VMEM((nj, SUBLANES, tn), F32)],
        compiler_params=_cp(2), name="ffn_up_prompt",
    )(h, w_up, w_up, ffn_conv_w, ffn_conv_w)


def _rope_tables(pos):
    inv = ROPE_THETA ** (-jnp.arange(0, HEAD_DIM, 2, dtype=F32) / HEAD_DIM)
    ang = pos.astype(F32)[:, None] * inv[None, :]
    cos, sin = jnp.cos(ang), jnp.sin(ang)
    return jnp.concatenate([cos] * 4, axis=-1), jnp.concatenate([-sin, sin, -sin, sin], axis=-1)


def _expand_past(past, seq):
    batch, _, n = past.shape
    return jnp.concatenate([past, jnp.zeros((batch, seq - 2, n), past.dtype)], axis=1).reshape(batch * seq, n)


def _last_rows(x, batch, seq, n_rows):
    return x.reshape(batch, seq, x.shape[-1])[:, seq - n_rows:, :]


def _layer(grp, next_grp, l, x, h, rope, weights, sinks, past):
    (w_in, conv_w, w_attn_br, w_conv_br, w_out, ln1_g, ln1_b, w_up, ffn_conv_w, w_down, ln2_g, ln2_b) = weights
    cos, sin = rope
    sample = grp.per_row
    q = _proj_rope(grp, h, w_in, l, COL_Q, ATTN_DIM, cos, sin, ATTN_DIM // LANES, HEAD_DIM ** -0.5,
                   F32 if sample else BF, "proj_q")
    kv = _proj_rope(grp, h, w_in, l, COL_KV, 2 * KV_DIM, cos, sin, KV_DIM // LANES, 1.0, F32, "proj_kv")
    if sample:
        cache_k, cache_v, state_conv, state_ffn = past
        attn, k_state, v_state = _attn_sample(sinks, q, kv, cache_k, cache_v, l, grp.batch, grp.seq)
        s, u = _in_conv(grp, h, w_in, conv_w, l, _expand_past(state_conv[l], grp.seq))
        conv_state = _last_rows(u, grp.batch, grp.seq, 2)
    else:
        attn = _attn_prompt(sinks, q, kv, grp.batch, grp.seq)
        kv_tail = _last_rows(kv, grp.batch, grp.seq, WINDOW)
        k_state, v_state = kv_tail[..., :KV_DIM], kv_tail[..., KV_DIM:]
        s, tails = _in_conv(grp, h, w_in, conv_w, l, None)
        tps = grp.seq // grp.tm_big
        conv_state = tails[tps - 1::tps, SUBLANES - 2:, :]
    merged = _merged(grp, h, attn, s, w_in, w_attn_br, w_conv_br, l)
    r1 = _mm_resid(grp, merged, w_out, l, x, G1, grp.tm, 512, "out_proj")
    x1, h2 = _ln(grp, r1, ln1_g, ln1_b, l, grp, (SC2, SH2), "ln1")
    if sample:
        eg = _expand_past(state_ffn[l][..., :D_FF], grp.seq)
        ev = _expand_past(state_ffn[l][..., D_FF:], grp.seq)
        act, ug, uv = _ffn_up(grp, h2, w_up, ffn_conv_w, l, eg, ev)
        ffn_state = _last_rows(jnp.concatenate([ug, uv], axis=-1), grp.batch, grp.seq, 2)
    else:
        act, tg, tv = _ffn_up(grp, h2, w_up, ffn_conv_w, l, None, None)
        ffn_state = jnp.concatenate([tg, tv], axis=-1)[tps - 1::tps, SUBLANES - 2:, :]
    r2 = _mm_resid(grp, act, w_down, l, x1, G2, min(grp.tm, 512), 512, "ffn_down")
    x2, h_next = _ln(grp, r2, ln2_g, ln2_b, l, next_grp, None if next_grp is None else (SC1, SH1), "ln2")
    shape4 = (grp.batch, WINDOW, N_KV_HEADS, HEAD_DIM)
    return x2, h_next, k_state.reshape(shape4), v_state.reshape(shape4), conv_state, ffn_state


def kernel(x_prompt, x_sample, cache_attn_k, cache_attn_v, state_conv, state_ffn_conv, c_prompt, c_sample, w_ada, b_ada, w_in, attn_sinks, conv_w, w_attn_br, w_conv_br, w_out, ln1_g, ln1_b, w_up, ffn_conv_w, w_down, ln2_g, ln2_b):
    bp, tp, _ = x_prompt.shape
    bs, ts, _ = x_sample.shape
    assert bp <= ADA_SAMPLE_ROW0 and ADA_SAMPLE_ROW0 + bs == ADA_ROWS

    c_all = jnp.concatenate([c_prompt, jnp.zeros((ADA_SAMPLE_ROW0 - bp, D_MODEL), F32), c_sample], axis=0)
    ada = _ada(c_all, w_ada, b_ada)

    weights = (w_in.astype(BF), conv_w, w_attn_br.astype(BF), w_conv_br.astype(BF), w_out.astype(BF),
               ln1_g, ln1_b, w_up.astype(BF), ffn_conv_w, w_down.astype(BF), ln2_g, ln2_b)

    rope_p = _rope_tables(jnp.arange(tp, dtype=jnp.int32))
    rope_s = tuple(jnp.tile(t, (bs, 1)) for t in _rope_tables(PAST_LEN + jnp.arange(ts, dtype=jnp.int32)))
    cache_k = cache_attn_k.reshape(DEPTH, bs, WINDOW, KV_DIM)
    cache_v = cache_attn_v.reshape(DEPTH, bs, WINDOW, KV_DIM)
    past = (cache_k, cache_v, state_conv, state_ffn_conv)

    xp = x_prompt.reshape(bp * tp, D_MODEL)
    xs = x_sample.reshape(bs * ts, D_MODEL)
    gps = [_Group(bp, tp, 1024, 2048, ada[l, :bp].reshape(bp, 1, -1)) for l in range(DEPTH)] + [None]
    gss = [_Group(bs, ts, bs * ts, bs * ts, jnp.repeat(ada[l, ADA_SAMPLE_ROW0:], ts, axis=0))
           for l in range(DEPTH)] + [None]
    hp = _modulate(gps[0], xp)
    hs = _modulate(gss[0], xs)
    outs_p, outs_s = [], []
    for l in range(DEPTH):
        xp, hp, *st_p = _layer(gps[l], gps[l + 1], l, xp, hp, rope_p, weights, attn_sinks[l], None)
        xs, hs, *st_s = _layer(gss[l], gss[l + 1], l, xs, hs, rope_s, weights, attn_sinks[l], past)
        outs_p.append(st_p)
        outs_s.append(st_s)
    stack = lambda outs, k: jnp.stack([o[k] for o in outs])
    return (xp.reshape(bp, tp, D_MODEL), xs.reshape(bs, ts, D_MODEL),
            stack(outs_p, 0), stack(outs_p, 1), stack(outs_p, 2), stack(outs_p, 3),
            stack(outs_s, 0), stack(outs_s, 1), stack(outs_s, 2), stack(outs_s, 3))
```

```python
import functools

import jax
import jax.numpy as jnp
from jax import lax
from jax.experimental import pallas as pl
from jax.experimental.pallas import tpu as pltpu

D_MODEL = 4096
DEPTH = 2
HEAD_DIM = 64
N_HEADS = 32
N_KV_HEADS = 4
ATTN_DIM = N_HEADS * HEAD_DIM
KV_DIM = N_KV_HEADS * HEAD_DIM
WINDOW = 128
ROPE_THETA = 10000.0
CONV_DIM = D_MODEL // 2
D_FF = 11008
PAST_LEN = 16384
LN_EPS = 1e-5
ALPHA = (2 * DEPTH) ** 0.25

COL_Q = 0
COL_KV = ATTN_DIM
COL_CB = ATTN_DIM + 2 * KV_DIM
COL_CC = COL_CB + CONV_DIM
COL_CX = COL_CC + CONV_DIM
COL_GA = COL_CX + CONV_DIM
COL_GC = COL_GA + D_MODEL

SH1, SC1, G1, SH2, SC2, G2 = range(6)

LANES = 128
SUBLANES = 8
ADA_ROWS = 40
ADA_SAMPLE_ROW0 = 8
ROW_CHUNK = 256
VMEM_LIMIT = 56 * 2**20
NEG = -1e30

BF = jnp.bfloat16
F32 = jnp.float32


def _cp(n_axes):
    return pltpu.CompilerParams(dimension_semantics=("arbitrary",) * n_axes, vmem_limit_bytes=VMEM_LIMIT)


def _bf(x):
    return x if x.dtype == BF else x.astype(BF)


def _dot(a, b):
    return jnp.dot(_bf(a), _bf(b), preferred_element_type=F32)


def _row_chunks(tm):
    chunk = min(tm, ROW_CHUNK)
    assert tm % chunk == 0
    return [pl.ds(r, chunk) for r in range(0, tm, chunk)]


def _staged(w_ref, stage_ref):
    stage_ref[...] = w_ref[...].astype(BF)
    return stage_ref


def _rows_of(ref, rows):
    return ref[...] if ref.shape[0] == 1 else ref[rows, :]


def _ada_body(c_ref, w_ref, b_ref, o_ref):
    c = c_ref[...]
    o_ref[...] = _dot(c * jax.nn.sigmoid(c), w_ref[...]) + b_ref[...]


def _mod_body(x_ref, sc_ref, sh_ref, o_ref):
    o_ref[...] = (x_ref[...] * (1.0 + sc_ref[...]) + sh_ref[...]).astype(o_ref.dtype)


def _proj_rope_body(h_ref, w_ref, cos_ref, sin_ref, o_ref, ws_ref, *, n_rope, scale):
    tn = o_ref.shape[1]
    w = _staged(w_ref, ws_ref)
    lane = lax.broadcasted_iota(jnp.int32, (min(h_ref.shape[0], ROW_CHUNK), LANES), 1)
    first_half = (lane & (HEAD_DIM - 1)) < HEAD_DIM // 2
    for rows in _row_chunks(h_ref.shape[0]):
        acc = _dot(h_ref[rows, :], w[...])
        cos = cos_ref[rows, :]
        sin = sin_ref[rows, :]
        for c in range(tn // LANES):
            ch = acc[:, c * LANES:(c + 1) * LANES]
            if c < n_rope:
                rot = jnp.where(first_half,
                                pltpu.roll(ch, LANES - HEAD_DIM // 2, axis=1),
                                pltpu.roll(ch, HEAD_DIM // 2, axis=1))
                ch = ch * cos + rot * sin
            if scale != 1.0:
                ch = ch * scale
            o_ref[rows, c * LANES:(c + 1) * LANES] = ch.astype(o_ref.dtype)


def _attend(q_ref, k2, v2, valid, sink_ref, o_ref):
    tq = q_ref.shape[0]
    nk = k2.shape[0]
    lo = lax.broadcasted_iota(jnp.int32, (nk, LANES), 1) < HEAD_DIM
    lo_q = lax.broadcasted_iota(jnp.int32, (tq, LANES), 1) < HEAD_DIM
    group = N_HEADS // N_KV_HEADS
    for m in range(KV_DIM // LANES):
        kc = k2[:, m * LANES:(m + 1) * LANES]
        vc = v2[:, m * LANES:(m + 1) * LANES]
        kr = pltpu.roll(kc, HEAD_DIM, axis=1)
        vr = pltpu.roll(vc, HEAD_DIM, axis=1)
        for hh in range(2):
            kv_head = 2 * m + hh
            k_lo, k_hi = (kc, kr) if hh == 0 else (kr, kc)
            v_lo, v_hi = (vc, vr) if hh == 0 else (vr, vc)
            kbd = jnp.concatenate([jnp.where(lo, k_lo, 0.0), jnp.where(lo, 0.0, k_hi)], axis=0).astype(BF)
            vbd = jnp.concatenate([jnp.where(lo, v_lo, 0.0), jnp.where(lo, 0.0, v_hi)], axis=0).astype(BF)
            pairs = [kv_head * (group // 2) + pp for pp in range(group // 2)]
            qs = _bf(jnp.concatenate([q_ref[:, p * LANES:(p + 1) * LANES] for p in pairs], axis=0))
            s_all = lax.dot_general(qs, kbd, (((1,), (1,)), ((), ())), preferred_element_type=F32)
            es, inv = [], []
            for pp, p in enumerate(pairs):
                s = s_all[pp * tq:(pp + 1) * tq, :]
                e_pair, d_pair = [], []
                for half in range(2):
                    sh = jnp.where(valid, s[:, half * nk:(half + 1) * nk], NEG)
                    sink = sink_ref[2 * p + half]
                    mx = jnp.maximum(jnp.max(sh, axis=-1, keepdims=True), sink)
                    e = jnp.exp(sh - mx)
                    e_pair.append(e)
                    d_pair.append(jnp.sum(e, axis=-1, keepdims=True) + jnp.exp(sink - mx))
                es.append(jnp.concatenate(e_pair, axis=1))
                inv.append(jnp.where(lo_q, 1.0 / d_pair[0], 1.0 / d_pair[1]))
            o_all = jnp.dot(_bf(jnp.concatenate(es, axis=0)), vbd, preferred_element_type=F32)
            for pp, p in enumerate(pairs):
                o = o_all[pp * tq:(pp + 1) * tq, :] * inv[pp]
                o_ref[:, p * LANES:(p + 1) * LANES] = o.astype(o_ref.dtype)


def _attn_prompt_body(sink_ref, q_ref, kvp_ref, kvc_ref, o_ref):
    n = pl.program_id(1)
    kvp = kvp_ref[...]
    kvc = kvc_ref[...]
    k2 = jnp.concatenate([kvp[:, :KV_DIM], kvc[:, :KV_DIM]], axis=0)
    v2 = jnp.concatenate([kvp[:, KV_DIM:], kvc[:, KV_DIM:]], axis=0)
    i = lax.broadcasted_iota(jnp.int32, (WINDOW, 2 * WINDOW), 0)
    j = lax.broadcasted_iota(jnp.int32, (WINDOW, 2 * WINDOW), 1)
    valid = (j >= i) & (j <= i + WINDOW) & ((j >= WINDOW) | (n > 0))
    _attend(q_ref, k2, v2, valid, sink_ref, o_ref)


def _attn_sample_body(sink_ref, q_ref, kvn_ref, ck_ref, cv_ref, o_ref, ks_ref, vs_ref):
    t = q_ref.shape[0]
    ck = ck_ref[...]
    cv = cv_ref[...]
    kvn = kvn_ref[...]
    pad = jnp.zeros((WINDOW - t, KV_DIM), F32)
    k2 = jnp.concatenate([ck, kvn[:, :KV_DIM], pad], axis=0)
    v2 = jnp.concatenate([cv, kvn[:, KV_DIM:], pad], axis=0)
    i = lax.broadcasted_iota(jnp.int32, (t, 2 * WINDOW), 0)
    j = lax.broadcasted_iota(jnp.int32, (t, 2 * WINDOW), 1)
    valid = (j >= i) & (j <= i + WINDOW)
    _attend(q_ref, k2, v2, valid, sink_ref, o_ref)
    ks_ref[0:WINDOW - t, :] = ck[t:, :]
    ks_ref[WINDOW - t:, :] = kvn[:, :KV_DIM]
    vs_ref[0:WINDOW - t, :] = cv[t:, :]
    vs_ref[WINDOW - t:, :] = kvn[:, KV_DIM:]


def _conv3_carry(u, w_ref, prev):
    ext = jnp.concatenate([prev, u], axis=0)
    x1 = pltpu.roll(ext, 1, axis=0)[SUBLANES:, :]
    x2 = pltpu.roll(ext, 2, axis=0)[SUBLANES:, :]
    return w_ref[0:1, :] * x2 + w_ref[1:2, :] * x1 + w_ref[2:3, :] * u


def _conv3_segments(u, w_ref, e, seg):
    t = lax.broadcasted_iota(jnp.int32, u.shape, 0) & (seg - 1)
    x1 = jnp.where(t >= 1, pltpu.roll(u, 1, axis=0), pltpu.roll(e, u.shape[0] - 1, axis=0))
    x2 = jnp.where(t >= 2, pltpu.roll(u, 2, axis=0), e)
    return w_ref[0:1, :] * x2 + w_ref[1:2, :] * x1 + w_ref[2:3, :] * u


def _carry_in(carry_ref, tiles_per_seq):
    i = pl.program_id(0)
    j = pl.program_id(1)

    @pl.when(i % tiles_per_seq == 0)
    def _():
        carry_ref[j] = jnp.zeros(carry_ref.shape[1:], F32)

    return carry_ref[j]


def _inconv_prompt_body(h_ref, wb_ref, wc_ref, wx_ref, cw_ref, s_ref, tail_ref, carry_ref,
                        wbs_ref, wcs_ref, wxs_ref, *, tiles_per_seq):
    j = pl.program_id(1)
    wb = _staged(wb_ref, wbs_ref)
    wc = _staged(wc_ref, wcs_ref)
    wx = _staged(wx_ref, wxs_ref)
    prev = _carry_in(carry_ref, tiles_per_seq)
    for rows in _row_chunks(h_ref.shape[0]):
        h = h_ref[rows, :]
        u = _dot(h, wc[...]) * _dot(h, wx[...])
        y = _conv3_carry(u, cw_ref, prev)
        prev = u[u.shape[0] - SUBLANES:, :]
        s_ref[rows, :] = (_dot(h, wb[...]) * y).astype(s_ref.dtype)
    carry_ref[j] = prev
    tail_ref[...] = prev


def _inconv_sample_body(h_ref, wb_ref, wc_ref, wx_ref, cw_ref, e_ref, s_ref, u_ref, *, seg):
    h = h_ref[...]
    u = _dot(h, wc_ref[...]) * _dot(h, wx_ref[...])
    u_ref[...] = u
    y = _conv3_segments(u, cw_ref, e_ref[...], seg)
    s_ref[...] = (_dot(h, wb_ref[...]) * y).astype(s_ref.dtype)


def _merged_body(h_ref, a_ref, s_ref, wga_ref, wgc_ref, wab_ref, wcb_ref, o_ref):
    for rows in _row_chunks(h_ref.shape[0]):
        h = h_ref[rows, :]
        attn_br = jax.nn.sigmoid(_dot(h, wga_ref[...])) * _dot(a_ref[rows, :], wab_ref[...])
        conv_br = jax.nn.sigmoid(_dot(h, wgc_ref[...])) * _dot(s_ref[rows, :], wcb_ref[...])
        o_ref[rows, :] = (attn_br + conv_br).astype(o_ref.dtype)


def _mm_resid_body(a_ref, w_ref, x_ref, g_ref, o_ref):
    for rows in _row_chunks(a_ref.shape[0]):
        o_ref[rows, :] = ALPHA * x_ref[rows, :] + _rows_of(g_ref, rows) * _dot(a_ref[rows, :], w_ref[...])


def _layernorm(r, g, b):
    mu = jnp.mean(r, axis=-1, keepdims=True)
    d = r - mu
    var = jnp.mean(d * d, axis=-1, keepdims=True)
    return d * lax.rsqrt(var + LN_EPS) * g + b


def _ln_mod_body(r_ref, g_ref, b_ref, sc_ref, sh_ref, x_ref, h_ref):
    y = _layernorm(r_ref[...], g_ref[...], b_ref[...])
    x_ref[...] = y
    h_ref[...] = (y * (1.0 + sc_ref[...]) + sh_ref[...]).astype(h_ref.dtype)


def _ln_body(r_ref, g_ref, b_ref, x_ref):
    x_ref[...] = _layernorm(r_ref[...], g_ref[...], b_ref[...])


def _silu_mul(g, v):
    return g * jax.nn.sigmoid(g) * v


def _ffnup_prompt_body(h_ref, wg_ref, wv_ref, cwg_ref, cwv_ref, act_ref, tg_ref, tv_ref, cg_ref, cv_ref,
                       wgs_ref, wvs_ref, *, tiles_per_seq):
    j = pl.program_id(1)
    wg = _staged(wg_ref, wgs_ref)
    wv = _staged(wv_ref, wvs_ref)
    prev_g = _carry_in(cg_ref, tiles_per_seq)
    prev_v = _carry_in(cv_ref, tiles_per_seq)
    for rows in _row_chunks(h_ref.shape[0]):
        h = h_ref[rows, :]
        ug = _dot(h, wg[...])
        uv = _dot(h, wv[...])
        yg = _conv3_carry(ug, cwg_ref, prev_g)
        yv = _conv3_carry(uv, cwv_ref, prev_v)
        prev_g = ug[ug.shape[0] - SUBLANES:, :]
        prev_v = uv[uv.shape[0] - SUBLANES:, :]
        act_ref[rows, :] = _silu_mul(yg, yv).astype(act_ref.dtype)
    cg_ref[j] = prev_g
    cv_ref[j] = prev_v
    tg_ref[...] = prev_g
    tv_ref[...] = prev_v


def _ffnup_sample_body(h_ref, wg_ref, wv_ref, cwg_ref, cwv_ref, eg_ref, ev_ref, act_ref, ug_ref, uv_ref, *, seg):
    h = h_ref[...]
    ug = _dot(h, wg_ref[...])
    uv = _dot(h, wv_ref[...])
    ug_ref[...] = ug
    uv_ref[...] = uv
    yg = _conv3_segments(ug, cwg_ref, eg_ref[...], seg)
    yv = _conv3_segments(uv, cwv_ref, ev_ref[...], seg)
    act_ref[...] = _silu_mul(yg, yv).astype(act_ref.dtype)


class _Group:
    def __init__(self, batch, seq, tm, tm_big, ada_rows):
        self.batch, self.seq, self.tm, self.tm_big = batch, seq, tm, tm_big
        self.m = batch * seq
        self.per_row = seq < tm
        self.tn_wide = 1024 if tm <= 256 else 512
        self.ada = ada_rows

    def param_spec(self, chunk, tm, tn):
        cb = chunk * (D_MODEL // tn)
        if self.per_row:
            return pl.BlockSpec((tm, tn), lambda i, j: (i, cb + j))
        tps = self.seq // tm
        return pl.BlockSpec((None, 1, tn), lambda i, j: (i // tps, 0, cb + j))


def _resident_rows(tm, width):
    return pl.BlockSpec((tm, width), lambda i, j: (i, 0), pipeline_mode=pl.Buffered(1))


def _ada(c_all, w_ada, b_ada):
    tn = 512
    n = w_ada.shape[-1]
    return pl.pallas_call(
        _ada_body,
        out_shape=jax.ShapeDtypeStruct((DEPTH, ADA_ROWS, n), F32),
        grid=(DEPTH, n // tn),
        in_specs=[pl.BlockSpec((ADA_ROWS, D_MODEL), lambda l, j: (0, 0)),
                  pl.BlockSpec((None, D_MODEL, tn), lambda l, j: (l, 0, j)),
                  pl.BlockSpec((None, 1, tn), lambda l, j: (l, 0, j))],
        out_specs=pl.BlockSpec((None, ADA_ROWS, tn), lambda l, j: (l, 0, j)),
        compiler_params=_cp(2), name="ada",
    )(c_all, w_ada, b_ada.reshape(DEPTH, 1, n))


def _modulate(grp, x):
    tr = min(grp.tm, 512)
    return pl.pallas_call(
        _mod_body,
        out_shape=jax.ShapeDtypeStruct((grp.m, D_MODEL), BF),
        grid=(grp.m // tr,),
        in_specs=[pl.BlockSpec((tr, D_MODEL), lambda i: (i, 0)),
                  _row_param_spec(grp, SC1, tr), _row_param_spec(grp, SH1, tr)],
        out_specs=pl.BlockSpec((tr, D_MODEL), lambda i: (i, 0)),
        compiler_params=_cp(1), name="modulate",
    )(x, grp.ada, grp.ada)


def _row_param_spec(grp, chunk, tr):
    if grp.per_row:
        return pl.BlockSpec((tr, D_MODEL), lambda i: (i, chunk))
    tps = grp.seq // tr
    return pl.BlockSpec((None, 1, D_MODEL), lambda i: (i // tps, 0, chunk))


def _proj_rope(grp, h, w_in, l, col0, n_cols, cos, sin, n_rope, scale, out_dtype, name):
    tm, tn = grp.tm_big, 512
    cb = col0 // tn
    t_tiles = cos.shape[0] // tm
    return pl.pallas_call(
        functools.partial(_proj_rope_body, n_rope=n_rope, scale=scale),
        out_shape=jax.ShapeDtypeStruct((grp.m, n_cols), out_dtype),
        grid=(grp.m // tm, n_cols // tn),
        in_specs=[_resident_rows(tm, D_MODEL),
                  pl.BlockSpec((None, D_MODEL, tn), lambda i, j: (l, 0, cb + j)),
                  pl.BlockSpec((tm, LANES), lambda i, j: (i % t_tiles, 0)),
                  pl.BlockSpec((tm, LANES), lambda i, j: (i % t_tiles, 0))],
        out_specs=pl.BlockSpec((tm, tn), lambda i, j: (i, j)),
        scratch_shapes=[pltpu.VMEM((D_MODEL, tn), BF)],
        compiler_params=_cp(2), name=name,
    )(h, w_in, cos, sin)


def _attn_prompt(sinks, q, kv, batch, seq):
    nb = seq // WINDOW
    return pl.pallas_call(
        _attn_prompt_body,
        out_shape=jax.ShapeDtypeStruct(q.shape, BF),
        grid=(batch, nb),
        in_specs=[pl.BlockSpec(memory_space=pltpu.SMEM),
                  pl.BlockSpec((WINDOW, ATTN_DIM), lambda b, n: (b * nb + n, 0)),
                  pl.BlockSpec((WINDOW, 2 * KV_DIM), lambda b, n: (b * nb + jnp.maximum(n - 1, 0), 0)),
                  pl.BlockSpec((WINDOW, 2 * KV_DIM), lambda b, n: (b * nb + n, 0))],
        out_specs=pl.BlockSpec((WINDOW, ATTN_DIM), lambda b, n: (b * nb + n, 0)),
        compiler_params=_cp(2), name="attn_prompt",
    )(sinks, q, kv, kv)


def _attn_sample(sinks, q, kvn, cache_k, cache_v, l, batch, seq):
    return pl.pallas_call(
        _attn_sample_body,
        out_shape=(jax.ShapeDtypeStruct(q.shape, F32),
                   jax.ShapeDtypeStruct((batch, WINDOW, KV_DIM), F32),
                   jax.ShapeDtypeStruct((batch, WINDOW, KV_DIM), F32)),
        grid=(batch,),
        in_specs=[pl.BlockSpec(memory_space=pltpu.SMEM),
                  pl.BlockSpec((seq, ATTN_DIM), lambda b: (b, 0)),
                  pl.BlockSpec((seq, 2 * KV_DIM), lambda b: (b, 0)),
                  pl.BlockSpec((None, None, WINDOW, KV_DIM), lambda b: (l, b, 0, 0)),
                  pl.BlockSpec((None, None, WINDOW, KV_DIM), lambda b: (l, b, 0, 0))],
        out_specs=(pl.BlockSpec((seq, ATTN_DIM), lambda b: (b, 0)),
                   pl.BlockSpec((None, WINDOW, KV_DIM), lambda b: (b, 0, 0)),
                   pl.BlockSpec((None, WINDOW, KV_DIM), lambda b: (b, 0, 0))),
        compiler_params=_cp(1), name="attn_sample",
    )(sinks, q, kvn, cache_k, cache_v)


def _in_conv(grp, h, w_in, conv_w, l, past_rows):
    tm, tn = grp.tm_big, 256
    nj = CONV_DIM // tn
    w_spec = lambda col0: pl.BlockSpec((None, D_MODEL, tn), lambda i, j: (l, 0, col0 // tn + j))
    ins = [_resident_rows(tm, D_MODEL),
           w_spec(COL_CB), w_spec(COL_CC), w_spec(COL_CX),
           pl.BlockSpec((None, 3, tn), lambda i, j: (l, 0, j))]
    s_shape = jax.ShapeDtypeStruct((grp.m, CONV_DIM), BF)
    s_spec = pl.BlockSpec((tm, tn), lambda i, j: (i, j))
    if grp.per_row:
        return pl.pallas_call(
            functools.partial(_inconv_sample_body, seg=grp.seq),
            out_shape=(s_shape, jax.ShapeDtypeStruct((grp.m, CONV_DIM), F32)),
            grid=(grp.m // tm, nj),
            in_specs=ins + [pl.BlockSpec((tm, tn), lambda i, j: (i, j))],
            out_specs=(s_spec, pl.BlockSpec((tm, tn), lambda i, j: (i, j))),
            compiler_params=_cp(2), name="in_conv_sample",
        )(h, w_in, w_in, w_in, conv_w, past_rows)
    return pl.pallas_call(
        functools.partial(_inconv_prompt_body, tiles_per_seq=grp.seq // tm),
        out_shape=(s_shape, jax.ShapeDtypeStruct((grp.m // tm, SUBLANES, CONV_DIM), F32)),
        grid=(grp.m // tm, nj),
        in_specs=ins,
        out_specs=(s_spec, pl.BlockSpec((None, SUBLANES, tn), lambda i, j: (i, 0, j))),
        scratch_shapes=[pltpu.VMEM((nj, SUBLANES, tn), F32)] + [pltpu.VMEM((D_MODEL, tn), BF)] * 3,
        compiler_params=_cp(2), name="in_conv_prompt",
    )(h, w_in, w_in, w_in, conv_w)


def _merged(grp, h, attn, s, w_gates, w_attn_br, w_conv_br, l):
    tm, tn = grp.tm, grp.tn_wide // 2
    row = lambda width: pl.BlockSpec((tm, width), lambda i, j: (i, 0))
    return pl.pallas_call(
        _merged_body,
        out_shape=jax.ShapeDtypeStruct((grp.m, D_MODEL), BF),
        grid=(grp.m // tm, D_MODEL // tn),
        in_specs=[row(D_MODEL), row(ATTN_DIM), row(CONV_DIM),
                  pl.BlockSpec((None, D_MODEL, tn), lambda i, j: (l, 0, j)),
                  pl.BlockSpec((None, D_MODEL, tn), lambda i, j: (l, 0, (COL_GC - COL_GA) // tn + j)),
                  pl.BlockSpec((None, ATTN_DIM, tn), lambda i, j: (l, 0, j)),
                  pl.BlockSpec((None, CONV_DIM, tn), lambda i, j: (l, 0, j))],
        out_specs=pl.BlockSpec((tm, tn), lambda i, j: (i, j)),
        compiler_params=_cp(2), name="merged",
    )(h, attn, s, w_gates, w_gates, w_attn_br, w_conv_br)


def _mm_resid(grp, a, w, l, x, gate_chunk, tm, tn, name):
    k = a.shape[1]
    return pl.pallas_call(
        _mm_resid_body,
        out_shape=jax.ShapeDtypeStruct((grp.m, D_MODEL), F32),
        grid=(grp.m // tm, D_MODEL // tn),
        in_specs=[pl.BlockSpec((tm, k), lambda i, j: (i, 0)),
                  pl.BlockSpec((None, k, tn), lambda i, j: (l, 0, j)),
                  pl.BlockSpec((tm, tn), lambda i, j: (i, j)),
                  grp.param_spec(gate_chunk, tm, tn)],
        out_specs=pl.BlockSpec((tm, tn), lambda i, j: (i, j)),
        compiler_params=_cp(2), name=name,
    )(a, w, x, grp.ada)


def _ln(grp, r, g, b, l, mod_grp, mod_chunks, name):
    tr = 256
    row = pl.BlockSpec((tr, D_MODEL), lambda i: (i, 0))
    vec = pl.BlockSpec((None, 1, D_MODEL), lambda i: (l, 0, 0))
    g3 = g.reshape(DEPTH, 1, D_MODEL)
    b3 = b.reshape(DEPTH, 1, D_MODEL)
    if mod_chunks is None:
        return pl.pallas_call(
            _ln_body, out_shape=jax.ShapeDtypeStruct((grp.m, D_MODEL), F32), grid=(grp.m // tr,),
            in_specs=[row, vec, vec], out_specs=row, compiler_params=_cp(1), name=name,
        )(r, g3, b3), None
    sc, sh = mod_chunks
    return pl.pallas_call(
        _ln_mod_body,
        out_shape=(jax.ShapeDtypeStruct((grp.m, D_MODEL), F32), jax.ShapeDtypeStruct((grp.m, D_MODEL), BF)),
        grid=(grp.m // tr,),
        in_specs=[row, vec, vec, _row_param_spec(mod_grp, sc, tr), _row_param_spec(mod_grp, sh, tr)],
        out_specs=(row, row), compiler_params=_cp(1), name=name,
    )(r, g3, b3, mod_grp.ada, mod_grp.ada)


def _ffn_up(grp, h, w_up, ffn_conv_w, l, past_g, past_v):
    tm, tn = grp.tm_big, 256
    nj = D_FF // tn
    ins = [_resident_rows(tm, D_MODEL),
           pl.BlockSpec((None, D_MODEL, tn), lambda i, j: (l, 0, j)),
           pl.BlockSpec((None, D_MODEL, tn), lambda i, j: (l, 0, nj + j)),
           pl.BlockSpec((None, 3, tn), lambda i, j: (l, 0, j)),
           pl.BlockSpec((None, 3, tn), lambda i, j: (l, 0, nj + j))]
    tile = pl.BlockSpec((tm, tn), lambda i, j: (i, j))
    act_shape = jax.ShapeDtypeStruct((grp.m, D_FF), BF)
    if grp.per_row:
        full = jax.ShapeDtypeStruct((grp.m, D_FF), F32)
        return pl.pallas_call(
            functools.partial(_ffnup_sample_body, seg=grp.seq),
            out_shape=(act_shape, full, full), grid=(grp.m // tm, nj),
            in_specs=ins + [tile, tile], out_specs=(tile, tile, tile),
            compiler_params=_cp(2), name="ffn_up_sample",
        )(h, w_up, w_up, ffn_conv_w, ffn_conv_w, past_g, past_v)
    tail = jax.ShapeDtypeStruct((grp.m // tm, SUBLANES, D_FF), F32)
    tail_spec = pl.BlockSpec((None, SUBLANES, tn), lambda i, j: (i, 0, j))
    return pl.pallas_call(
        functools.partial(_ffnup_prompt_body, tiles_per_seq=grp.seq // tm),
        out_shape=(act_shape, tail, tail), grid=(grp.m // tm, nj),
        in_specs=ins, out_specs=(tile, tail_spec, tail_spec),
        scratch_shapes=[pltpu.VMEM((nj, SUBLANES, tn), F32), pltpu.VMEM((nj, SUBLANES, tn), F32),
                        pltpu.VMEM((D_MODEL, tn), BF), pltpu.VMEM((D_MODEL, tn), BF)],
        compiler_params=_cp(2), name="ffn_up_prompt",
    )(h, w_up, w_up, ffn_conv_w, ffn_conv_w)


def _rope_tables(pos):
    inv = ROPE_THETA ** (-jnp.arange(0, HEAD_DIM, 2, dtype=F32) / HEAD_DIM)
    ang = pos.astype(F32)[:, None] * inv[None, :]
    cos, sin = jnp.cos(ang), jnp.sin(ang)
    return jnp.concatenate([cos] * 4, axis=-1), jnp.concatenate([-sin, sin, -sin, sin], axis=-1)


def _expand_past(past, seq):
    batch, _, n = past.shape
    return jnp.concatenate([past, jnp.zeros((batch, seq - 2, n), past.dtype)], axis=1).reshape(batch * seq, n)


def _last_rows(x, batch, seq, n_rows):
    return x.reshape(batch, seq, x.shape[-1])[:, seq - n_rows:, :]


def _layer(grp, next_grp, l, x, h, rope, weights, sinks, past):
    (w_in, w_gates, conv_w, w_attn_br, w_conv_br, w_out, ln1_g, ln1_b, w_up, ffn_conv_w, w_down,
     ln2_g, ln2_b) = weights
    cos, sin = rope
    sample = grp.per_row
    q = _proj_rope(grp, h, w_in, l, COL_Q, ATTN_DIM, cos, sin, ATTN_DIM // LANES, HEAD_DIM ** -0.5,
                   F32 if sample else BF, "proj_q")
    kv = _proj_rope(grp, h, w_in, l, COL_KV, 2 * KV_DIM, cos, sin, KV_DIM // LANES, 1.0, F32, "proj_kv")
    if sample:
        cache_k, cache_v, state_conv, state_ffn = past
        attn, k_state, v_state = _attn_sample(sinks, q, kv, cache_k, cache_v, l, grp.batch, grp.seq)
        s, u = _in_conv(grp, h, w_in, conv_w, l, _expand_past(state_conv[l], grp.seq))
        conv_state = _last_rows(u, grp.batch, grp.seq, 2)
    else:
        attn = _attn_prompt(sinks, q, kv, grp.batch, grp.seq)
        kv_tail = _last_rows(kv, grp.batch, grp.seq, WINDOW)
        k_state, v_state = kv_tail[..., :KV_DIM], kv_tail[..., KV_DIM:]
        s, tails = _in_conv(grp, h, w_in, conv_w, l, None)
        tps = grp.seq // grp.tm_big
        conv_state = tails[tps - 1::tps, SUBLANES - 2:, :]
    merged = _merged(grp, h, attn, s, w_gates, w_attn_br, w_conv_br, l)
    r1 = _mm_resid(grp, merged, w_out, l, x, G1, grp.tm, grp.tn_wide, "out_proj")
    x1, h2 = _ln(grp, r1, ln1_g, ln1_b, l, grp, (SC2, SH2), "ln1")
    if sample:
        eg = _expand_past(state_ffn[l][..., :D_FF], grp.seq)
        ev = _expand_past(state_ffn[l][..., D_FF:], grp.seq)
        act, ug, uv = _ffn_up(grp, h2, w_up, ffn_conv_w, l, eg, ev)
        ffn_state = _last_rows(jnp.concatenate([ug, uv], axis=-1), grp.batch, grp.seq, 2)
    else:
        act, tg, tv = _ffn_up(grp, h2, w_up, ffn_conv_w, l, None, None)
        ffn_state = jnp.concatenate([tg, tv], axis=-1)[tps - 1::tps, SUBLANES - 2:, :]
    r2 = _mm_resid(grp, act, w_down, l, x1, G2, min(grp.tm, 512), 512, "ffn_down")
    x2, h_next = _ln(grp, r2, ln2_g, ln2_b, l, next_grp, None if next_grp is None else (SC1, SH1), "ln2")
    shape4 = (grp.batch, WINDOW, N_KV_HEADS, HEAD_DIM)
    return x2, h_next, k_state.reshape(shape4), v_state.reshape(shape4), conv_state, ffn_state


def kernel(x_prompt, x_sample, cache_attn_k, cache_attn_v, state_conv, state_ffn_conv, c_prompt, c_sample, w_ada, b_ada, w_in, attn_sinks, conv_w, w_attn_br, w_conv_br, w_out, ln1_g, ln1_b, w_up, ffn_conv_w, w_down, ln2_g, ln2_b):
    bp, tp, _ = x_prompt.shape
    bs, ts, _ = x_sample.shape
    assert bp <= ADA_SAMPLE_ROW0 and ADA_SAMPLE_ROW0 + bs == ADA_ROWS

    c_all = jnp.concatenate([c_prompt, jnp.zeros((ADA_SAMPLE_ROW0 - bp, D_MODEL), F32), c_sample], axis=0)
    ada = _ada(c_all, w_ada, b_ada)

    weights = (w_in, w_in[:, :, COL_GA:].astype(BF), conv_w, w_attn_br.astype(BF), w_conv_br.astype(BF),
               w_out.astype(BF), ln1_g, ln1_b, w_up, ffn_conv_w, w_down.astype(BF), ln2_g, ln2_b)

    rope_p = _rope_tables(jnp.arange(tp, dtype=jnp.int32))
    rope_s = tuple(jnp.tile(t, (bs, 1)) for t in _rope_tables(PAST_LEN + jnp.arange(ts, dtype=jnp.int32)))
    cache_k = cache_attn_k.reshape(DEPTH, bs, WINDOW, KV_DIM)
    cache_v = cache_attn_v.reshape(DEPTH, bs, WINDOW, KV_DIM)
    past = (cache_k, cache_v, state_conv, state_ffn_conv)

    xp = x_prompt.reshape(bp * tp, D_MODEL)
    xs = x_sample.reshape(bs * ts, D_MODEL)
    gps = [_Group(bp, tp, 1024, 2048, ada[l, :bp].reshape(bp, 1, -1)) for l in range(DEPTH)] + [None]
    gss = [_Group(bs, ts, bs * ts, bs * ts, jnp.repeat(ada[l, ADA_SAMPLE_ROW0:], ts, axis=0))
           for l in range(DEPTH)] + [None]
    hp = _modulate(gps[0], xp)
    hs = _modulate(gss[0], xs)
    outs_p, outs_s = [], []
    for l in range(DEPTH):
        xp, hp, *st_p = _layer(gps[l], gps[l + 1], l, xp, hp, rope_p, weights, attn_sinks[l], None)
        xs, hs, *st_s = _layer(gss[l], gss[l + 1], l, xs, hs, rope_s, weights, attn_sinks[l], past)
        outs_p.append(st_p)
        outs_s.append(st_s)
    stack = lambda outs, k: jnp.stack([o[k] for o in outs])
    return (xp.reshape(bp, tp, D_MODEL), xs.reshape(bs, ts, D_MODEL),
            stack(outs_p, 0), stack(outs_p, 1), stack(outs_p, 2), stack(outs_p, 3),
            stack(outs_s, 0), stack(outs_s, 1), stack(outs_s, 2), stack(outs_s, 3))
```

```python
import functools

import jax
import jax.numpy as jnp
from jax import lax
from jax.experimental import pallas as pl
from jax.experimental.pallas import tpu as pltpu

D_MODEL = 4096
DEPTH = 2
HEAD_DIM = 64
N_HEADS = 32
N_KV_HEADS = 4
ATTN_DIM = N_HEADS * HEAD_DIM
KV_DIM = N_KV_HEADS * HEAD_DIM
WINDOW = 128
ROPE_THETA = 10000.0
CONV_DIM = D_MODEL // 2
D_FF = 11008
PAST_LEN = 16384
LN_EPS = 1e-5
ALPHA = (2 * DEPTH) ** 0.25

COL_Q = 0
COL_KV = ATTN_DIM
COL_CB = ATTN_DIM + 2 * KV_DIM
COL_CC = COL_CB + CONV_DIM
COL_CX = COL_CC + CONV_DIM
COL_GA = COL_CX + CONV_DIM
COL_GC = COL_GA + D_MODEL

SH1, SC1, G1, SH2, SC2, G2 = range(6)

LANES = 128
SUBLANES = 8
ADA_ROWS = 40
ADA_SAMPLE_ROW0 = 8
ROW_CHUNK = 256
VMEM_LIMIT = 56 * 2**20
NEG = -1e30

BF = jnp.bfloat16
F32 = jnp.float32


def _cp(n_axes):
    return pltpu.CompilerParams(dimension_semantics=("arbitrary",) * n_axes, vmem_limit_bytes=VMEM_LIMIT)


def _bf(x):
    return x if x.dtype == BF else x.astype(BF)


def _dot(a, b):
    return jnp.dot(_bf(a), _bf(b), preferred_element_type=F32)


def _row_chunks(tm):
    chunk = min(tm, ROW_CHUNK)
    assert tm % chunk == 0
    return [pl.ds(r, chunk) for r in range(0, tm, chunk)]


def _staged(w_ref, stage_ref):
    stage_ref[...] = w_ref[...].astype(BF)
    return stage_ref


def _rows_of(ref, rows):
    return ref[...] if ref.shape[0] == 1 else ref[rows, :]


def _ada_body(c_ref, w_ref, b_ref, o_ref):
    c = c_ref[...]
    o_ref[...] = _dot(c * jax.nn.sigmoid(c), w_ref[...]) + b_ref[...]


def _mod_body(x_ref, sc_ref, sh_ref, o_ref):
    o_ref[...] = (x_ref[...] * (1.0 + sc_ref[...]) + sh_ref[...]).astype(o_ref.dtype)


def _proj_rope_body(h_ref, w_ref, cos_ref, sin_ref, o_ref, ws_ref, *, n_rope, scale):
    tn = o_ref.shape[1]
    w = _staged(w_ref, ws_ref)
    lane = lax.broadcasted_iota(jnp.int32, (min(h_ref.shape[0], ROW_CHUNK), LANES), 1)
    first_half = (lane & (HEAD_DIM - 1)) < HEAD_DIM // 2
    for rows in _row_chunks(h_ref.shape[0]):
        acc = _dot(h_ref[rows, :], w[...])
        cos = cos_ref[rows, :]
        sin = sin_ref[rows, :]
        for c in range(tn // LANES):
            ch = acc[:, c * LANES:(c + 1) * LANES]
            if c < n_rope:
                rot = jnp.where(first_half,
                                pltpu.roll(ch, LANES - HEAD_DIM // 2, axis=1),
                                pltpu.roll(ch, HEAD_DIM // 2, axis=1))
                ch = ch * cos + rot * sin
            if scale != 1.0:
                ch = ch * scale
            o_ref[rows, c * LANES:(c + 1) * LANES] = ch.astype(o_ref.dtype)


def _attend(q_ref, k2, v2, valid, sink_ref, o_ref):
    tq = q_ref.shape[0]
    nk = k2.shape[0]
    lo = lax.broadcasted_iota(jnp.int32, (nk, LANES), 1) < HEAD_DIM
    lo_q = lax.broadcasted_iota(jnp.int32, (tq, LANES), 1) < HEAD_DIM
    group = N_HEADS // N_KV_HEADS
    for m in range(KV_DIM // LANES):
        kc = k2[:, m * LANES:(m + 1) * LANES]
        vc = v2[:, m * LANES:(m + 1) * LANES]
        kr = pltpu.roll(kc, HEAD_DIM, axis=1)
        vr = pltpu.roll(vc, HEAD_DIM, axis=1)
        for hh in range(2):
            kv_head = 2 * m + hh
            k_lo, k_hi = (kc, kr) if hh == 0 else (kr, kc)
            v_lo, v_hi = (vc, vr) if hh == 0 else (vr, vc)
            kbd = jnp.concatenate([jnp.where(lo, k_lo, 0.0), jnp.where(lo, 0.0, k_hi)], axis=0).astype(BF)
            vbd = jnp.concatenate([jnp.where(lo, v_lo, 0.0), jnp.where(lo, 0.0, v_hi)], axis=0).astype(BF)
            pairs = [kv_head * (group // 2) + pp for pp in range(group // 2)]
            qs = _bf(jnp.concatenate([q_ref[:, p * LANES:(p + 1) * LANES] for p in pairs], axis=0))
            s_all = lax.dot_general(qs, kbd, (((1,), (1,)), ((), ())), preferred_element_type=F32)
            es, inv = [], []
            for pp, p in enumerate(pairs):
                s = s_all[pp * tq:(pp + 1) * tq, :]
                e_pair, d_pair = [], []
                for half in range(2):
                    sh = jnp.where(valid, s[:, half * nk:(half + 1) * nk], NEG)
                    sink = sink_ref[2 * p + half]
                    mx = jnp.maximum(jnp.max(sh, axis=-1, keepdims=True), sink)
                    e = jnp.exp(sh - mx)
                    e_pair.append(e)
                    d_pair.append(jnp.sum(e, axis=-1, keepdims=True) + jnp.exp(sink - mx))
                es.append(jnp.concatenate(e_pair, axis=1))
                inv.append(jnp.where(lo_q, 1.0 / d_pair[0], 1.0 / d_pair[1]))
            o_all = jnp.dot(_bf(jnp.concatenate(es, axis=0)), vbd, preferred_element_type=F32)
            for pp, p in enumerate(pairs):
                o = o_all[pp * tq:(pp + 1) * tq, :] * inv[pp]
                o_ref[:, p * LANES:(p + 1) * LANES] = o.astype(o_ref.dtype)


def _attn_prompt_body(sink_ref, q_ref, kvp_ref, kvc_ref, o_ref):
    n = pl.program_id(1)
    kvp = kvp_ref[...]
    kvc = kvc_ref[...]
    k2 = jnp.concatenate([kvp[:, :KV_DIM], kvc[:, :KV_DIM]], axis=0)
    v2 = jnp.concatenate([kvp[:, KV_DIM:], kvc[:, KV_DIM:]], axis=0)
    i = lax.broadcasted_iota(jnp.int32, (WINDOW, 2 * WINDOW), 0)
    j = lax.broadcasted_iota(jnp.int32, (WINDOW, 2 * WINDOW), 1)
    valid = (j >= i) & (j <= i + WINDOW) & ((j >= WINDOW) | (n > 0))
    _attend(q_ref, k2, v2, valid, sink_ref, o_ref)


def _attn_sample_body(sink_ref, q_ref, kvn_ref, ck_ref, cv_ref, o_ref, ks_ref, vs_ref):
    t = q_ref.shape[0]
    ck = ck_ref[...]
    cv = cv_ref[...]
    kvn = kvn_ref[...]
    pad = jnp.zeros((WINDOW - t, KV_DIM), F32)
    k2 = jnp.concatenate([ck, kvn[:, :KV_DIM], pad], axis=0)
    v2 = jnp.concatenate([cv, kvn[:, KV_DIM:], pad], axis=0)
    i = lax.broadcasted_iota(jnp.int32, (t, 2 * WINDOW), 0)
    j = lax.broadcasted_iota(jnp.int32, (t, 2 * WINDOW), 1)
    valid = (j >= i) & (j <= i + WINDOW)
    _attend(q_ref, k2, v2, valid, sink_ref, o_ref)
    ks_ref[0:WINDOW - t, :] = ck[t:, :]
    ks_ref[WINDOW - t:, :] = kvn[:, :KV_DIM]
    vs_ref[0:WINDOW - t, :] = cv[t:, :]
    vs_ref[WINDOW - t:, :] = kvn[:, KV_DIM:]


def _conv3_carry(u, w_ref, prev):
    ext = jnp.concatenate([prev, u], axis=0)
    x1 = pltpu.roll(ext, 1, axis=0)[SUBLANES:, :]
    x2 = pltpu.roll(ext, 2, axis=0)[SUBLANES:, :]
    return w_ref[0:1, :] * x2 + w_ref[1:2, :] * x1 + w_ref[2:3, :] * u


def _conv3_segments(u, w_ref, e, seg):
    t = lax.broadcasted_iota(jnp.int32, u.shape, 0) & (seg - 1)
    x1 = jnp.where(t >= 1, pltpu.roll(u, 1, axis=0), pltpu.roll(e, u.shape[0] - 1, axis=0))
    x2 = jnp.where(t >= 2, pltpu.roll(u, 2, axis=0), e)
    return w_ref[0:1, :] * x2 + w_ref[1:2, :] * x1 + w_ref[2:3, :] * u


def _carry_in(carry_ref, tiles_per_seq):
    i = pl.program_id(0)
    j = pl.program_id(1)

    @pl.when(i % tiles_per_seq == 0)
    def _():
        carry_ref[j] = jnp.zeros(carry_ref.shape[1:], F32)

    return carry_ref[j]


def _inconv_prompt_body(h_ref, wb_ref, wc_ref, wx_ref, cw_ref, s_ref, tail_ref, carry_ref,
                        wbs_ref, wcs_ref, wxs_ref, *, tiles_per_seq):
    j = pl.program_id(1)
    wb = _staged(wb_ref, wbs_ref)
    wc = _staged(wc_ref, wcs_ref)
    wx = _staged(wx_ref, wxs_ref)
    prev = _carry_in(carry_ref, tiles_per_seq)
    for rows in _row_chunks(h_ref.shape[0]):
        h = h_ref[rows, :]
        u = _dot(h, wc[...]) * _dot(h, wx[...])
        y = _conv3_carry(u, cw_ref, prev)
        prev = u[u.shape[0] - SUBLANES:, :]
        s_ref[rows, :] = (_dot(h, wb[...]) * y).astype(s_ref.dtype)
    carry_ref[j] = prev
    tail_ref[...] = prev


def _inconv_sample_body(h_ref, wb_ref, wc_ref, wx_ref, cw_ref, e_ref, s_ref, u_ref, *, seg):
    h = h_ref[...]
    u = _dot(h, wc_ref[...]) * _dot(h, wx_ref[...])
    u_ref[...] = u
    y = _conv3_segments(u, cw_ref, e_ref[...], seg)
    s_ref[...] = (_dot(h, wb_ref[...]) * y).astype(s_ref.dtype)


def _merged_body(h_ref, a_ref, s_ref, wga_ref, wgc_ref, wab_ref, wcb_ref, o_ref):
    for rows in _row_chunks(h_ref.shape[0]):
        h = h_ref[rows, :]
        attn_br = jax.nn.sigmoid(_dot(h, wga_ref[...])) * _dot(a_ref[rows, :], wab_ref[...])
        conv_br = jax.nn.sigmoid(_dot(h, wgc_ref[...])) * _dot(s_ref[rows, :], wcb_ref[...])
        o_ref[rows, :] = (attn_br + conv_br).astype(o_ref.dtype)


def _merged_cast_body(h_ref, a_ref, s_ref, wga_ref, wgc_ref, wab_ref, wcb_ref, o_ref, ga_ref, gc_ref, ab_ref, cb_ref):
    for w_ref, b_ref in ((wga_ref, ga_ref), (wgc_ref, gc_ref), (wab_ref, ab_ref), (wcb_ref, cb_ref)):
        b_ref[...] = w_ref[...].astype(BF)
    _merged_body(h_ref, a_ref, s_ref, ga_ref, gc_ref, ab_ref, cb_ref, o_ref)


def _mm_resid_body(a_ref, w_ref, x_ref, g_ref, o_ref):
    for rows in _row_chunks(a_ref.shape[0]):
        o_ref[rows, :] = ALPHA * x_ref[rows, :] + _rows_of(g_ref, rows) * _dot(a_ref[rows, :], w_ref[...])


def _mm_resid_cast_body(a_ref, w_ref, x_ref, g_ref, o_ref, wb_ref):
    wb_ref[...] = w_ref[...].astype(BF)
    _mm_resid_body(a_ref, wb_ref, x_ref, g_ref, o_ref)


def _layernorm(r, g, b):
    mu = jnp.mean(r, axis=-1, keepdims=True)
    d = r - mu
    var = jnp.mean(d * d, axis=-1, keepdims=True)
    return d * lax.rsqrt(var + LN_EPS) * g + b


def _ln_mod_body(r_ref, g_ref, b_ref, sc_ref, sh_ref, x_ref, h_ref):
    y = _layernorm(r_ref[...], g_ref[...], b_ref[...])
    x_ref[...] = y
    h_ref[...] = (y * (1.0 + sc_ref[...]) + sh_ref[...]).astype(h_ref.dtype)


def _ln_body(r_ref, g_ref, b_ref, x_ref):
    x_ref[...] = _layernorm(r_ref[...], g_ref[...], b_ref[...])


def _silu_mul(g, v):
    return g * jax.nn.sigmoid(g) * v


def _ffnup_prompt_body(h_ref, wg_ref, wv_ref, cwg_ref, cwv_ref, act_ref, tg_ref, tv_ref, cg_ref, cv_ref,
                       wgs_ref, wvs_ref, *, tiles_per_seq):
    j = pl.program_id(1)
    wg = _staged(wg_ref, wgs_ref)
    wv = _staged(wv_ref, wvs_ref)
    prev_g = _carry_in(cg_ref, tiles_per_seq)
    prev_v = _carry_in(cv_ref, tiles_per_seq)
    for rows in _row_chunks(h_ref.shape[0]):
        h = h_ref[rows, :]
        ug = _dot(h, wg[...])
        uv = _dot(h, wv[...])
        yg = _conv3_carry(ug, cwg_ref, prev_g)
        yv = _conv3_carry(uv, cwv_ref, prev_v)
        prev_g = ug[ug.shape[0] - SUBLANES:, :]
        prev_v = uv[uv.shape[0] - SUBLANES:, :]
        act_ref[rows, :] = _silu_mul(yg, yv).astype(act_ref.dtype)
    cg_ref[j] = prev_g
    cv_ref[j] = prev_v
    tg_ref[...] = prev_g
    tv_ref[...] = prev_v


def _ffnup_sample_body(h_ref, wg_ref, wv_ref, cwg_ref, cwv_ref, eg_ref, ev_ref, act_ref, ug_ref, uv_ref, *, seg):
    h = h_ref[...]
    ug = _dot(h, wg_ref[...])
    uv = _dot(h, wv_ref[...])
    ug_ref[...] = ug
    uv_ref[...] = uv
    yg = _conv3_segments(ug, cwg_ref, eg_ref[...], seg)
    yv = _conv3_segments(uv, cwv_ref, ev_ref[...], seg)
    act_ref[...] = _silu_mul(yg, yv).astype(act_ref.dtype)


class _Group:
    def __init__(self, batch, seq, tm, tm_big, ada_rows):
        self.batch, self.seq, self.tm, self.tm_big = batch, seq, tm, tm_big
        self.m = batch * seq
        self.per_row = seq < tm
        self.ada = ada_rows

    def param_spec(self, chunk, tm, tn):
        cb = chunk * (D_MODEL // tn)
        if self.per_row:
            return pl.BlockSpec((tm, tn), lambda i, j: (i, cb + j))
        tps = self.seq // tm
        return pl.BlockSpec((None, 1, tn), lambda i, j: (i // tps, 0, cb + j))


def _resident_rows(tm, width):
    return pl.BlockSpec((tm, width), lambda i, j: (i, 0), pipeline_mode=pl.Buffered(1))


def _ada(c_all, w_ada, b_ada):
    tn = 512
    n = w_ada.shape[-1]
    return pl.pallas_call(
        _ada_body,
        out_shape=jax.ShapeDtypeStruct((DEPTH, ADA_ROWS, n), F32),
        grid=(DEPTH, n // tn),
        in_specs=[pl.BlockSpec((ADA_ROWS, D_MODEL), lambda l, j: (0, 0)),
                  pl.BlockSpec((None, D_MODEL, tn), lambda l, j: (l, 0, j)),
                  pl.BlockSpec((None, 1, tn), lambda l, j: (l, 0, j))],
        out_specs=pl.BlockSpec((None, ADA_ROWS, tn), lambda l, j: (l, 0, j)),
        compiler_params=_cp(2), name="ada",
    )(c_all, w_ada, b_ada.reshape(DEPTH, 1, n))


def _modulate(grp, x):
    tr = min(grp.tm, 512)
    return pl.pallas_call(
        _mod_body,
        out_shape=jax.ShapeDtypeStruct((grp.m, D_MODEL), BF),
        grid=(grp.m // tr,),
        in_specs=[pl.BlockSpec((tr, D_MODEL), lambda i: (i, 0)),
                  _row_param_spec(grp, SC1, tr), _row_param_spec(grp, SH1, tr)],
        out_specs=pl.BlockSpec((tr, D_MODEL), lambda i: (i, 0)),
        compiler_params=_cp(1), name="modulate",
    )(x, grp.ada, grp.ada)


def _row_param_spec(grp, chunk, tr):
    if grp.per_row:
        return pl.BlockSpec((tr, D_MODEL), lambda i: (i, chunk))
    tps = grp.seq // tr
    return pl.BlockSpec((None, 1, D_MODEL), lambda i: (i // tps, 0, chunk))


def _proj_rope(grp, h, w_in, l, col0, n_cols, tm, tn, cos, sin, n_rope, scale, out_dtype, name):
    cb = col0 // tn
    t_tiles = cos.shape[0] // tm
    return pl.pallas_call(
        functools.partial(_proj_rope_body, n_rope=n_rope, scale=scale),
        out_shape=jax.ShapeDtypeStruct((grp.m, n_cols), out_dtype),
        grid=(grp.m // tm, n_cols // tn),
        in_specs=[pl.BlockSpec((tm, D_MODEL), lambda i, j: (i, 0)),
                  pl.BlockSpec((None, D_MODEL, tn), lambda i, j: (l, 0, cb + j)),
                  pl.BlockSpec((tm, LANES), lambda i, j: (i % t_tiles, 0)),
                  pl.BlockSpec((tm, LANES), lambda i, j: (i % t_tiles, 0))],
        out_specs=pl.BlockSpec((tm, tn), lambda i, j: (i, j)),
        scratch_shapes=[pltpu.VMEM((D_MODEL, tn), BF)],
        compiler_params=_cp(2), name=name,
    )(h, w_in, cos, sin)


def _attn_prompt(sinks, q, kv, batch, seq):
    nb = seq // WINDOW
    return pl.pallas_call(
        _attn_prompt_body,
        out_shape=jax.ShapeDtypeStruct(q.shape, BF),
        grid=(batch, nb),
        in_specs=[pl.BlockSpec(memory_space=pltpu.SMEM),
                  pl.BlockSpec((WINDOW, ATTN_DIM), lambda b, n: (b * nb + n, 0)),
                  pl.BlockSpec((WINDOW, 2 * KV_DIM), lambda b, n: (b * nb + jnp.maximum(n - 1, 0), 0)),
                  pl.BlockSpec((WINDOW, 2 * KV_DIM), lambda b, n: (b * nb + n, 0))],
        out_specs=pl.BlockSpec((WINDOW, ATTN_DIM), lambda b, n: (b * nb + n, 0)),
        compiler_params=_cp(2), name="attn_prompt",
    )(sinks, q, kv, kv)


def _attn_sample(sinks, q, kvn, cache_k, cache_v, l, batch, seq):
    return pl.pallas_call(
        _attn_sample_body,
        out_shape=(jax.ShapeDtypeStruct(q.shape, F32),
                   jax.ShapeDtypeStruct((batch, WINDOW, KV_DIM), F32),
                   jax.ShapeDtypeStruct((batch, WINDOW, KV_DIM), F32)),
        grid=(batch,),
        in_specs=[pl.BlockSpec(memory_space=pltpu.SMEM),
                  pl.BlockSpec((seq, ATTN_DIM), lambda b: (b, 0)),
                  pl.BlockSpec((seq, 2 * KV_DIM), lambda b: (b, 0)),
                  pl.BlockSpec((None, None, WINDOW, KV_DIM), lambda b: (l, b, 0, 0)),
                  pl.BlockSpec((None, None, WINDOW, KV_DIM), lambda b: (l, b, 0, 0))],
        out_specs=(pl.BlockSpec((seq, ATTN_DIM), lambda b: (b, 0)),
                   pl.BlockSpec((None, WINDOW, KV_DIM), lambda b: (b, 0, 0)),
                   pl.BlockSpec((None, WINDOW, KV_DIM), lambda b: (b, 0, 0))),
        compiler_params=_cp(1), name="attn_sample",
    )(sinks, q, kvn, cache_k, cache_v)


def _in_conv(grp, h, w_in, conv_w, l, past_rows):
    tm, tn = grp.tm_big, 256
    nj = CONV_DIM // tn
    w_spec = lambda col0: pl.BlockSpec((None, D_MODEL, tn), lambda i, j: (l, 0, col0 // tn + j))
    ins = [_resident_rows(tm, D_MODEL),
           w_spec(COL_CB), w_spec(COL_CC), w_spec(COL_CX),
           pl.BlockSpec((None, 3, tn), lambda i, j: (l, 0, j))]
    s_shape = jax.ShapeDtypeStruct((grp.m, CONV_DIM), BF)
    s_spec = pl.BlockSpec((tm, tn), lambda i, j: (i, j))
    if grp.per_row:
        return pl.pallas_call(
            functools.partial(_inconv_sample_body, seg=grp.seq),
            out_shape=(s_shape, jax.ShapeDtypeStruct((grp.m, CONV_DIM), F32)),
            grid=(grp.m // tm, nj),
            in_specs=ins + [pl.BlockSpec((tm, tn), lambda i, j: (i, j))],
            out_specs=(s_spec, pl.BlockSpec((tm, tn), lambda i, j: (i, j))),
            compiler_params=_cp(2), name="in_conv_sample",
        )(h, w_in, w_in, w_in, conv_w, past_rows)
    return pl.pallas_call(
        functools.partial(_inconv_prompt_body, tiles_per_seq=grp.seq // tm),
        out_shape=(s_shape, jax.ShapeDtypeStruct((grp.m // tm, SUBLANES, CONV_DIM), F32)),
        grid=(grp.m // tm, nj),
        in_specs=ins,
        out_specs=(s_spec, pl.BlockSpec((None, SUBLANES, tn), lambda i, j: (i, 0, j))),
        scratch_shapes=[pltpu.VMEM((nj, SUBLANES, tn), F32)] + [pltpu.VMEM((D_MODEL, tn), BF)] * 3,
        compiler_params=_cp(2), name="in_conv_prompt",
    )(h, w_in, w_in, w_in, conv_w)


def _merged(grp, h, attn, s, l, w_f32=None, w_bf16=None):
    tm, tn = grp.tm, 256
    grid = (grp.m // tm, D_MODEL // tn)
    row = lambda width: pl.BlockSpec((tm, width), lambda i, j: (i, 0))
    rows = [row(D_MODEL), row(ATTN_DIM), row(CONV_DIM)]
    out_shape = jax.ShapeDtypeStruct((grp.m, D_MODEL), BF)
    out_spec = pl.BlockSpec((tm, tn), lambda i, j: (i, j))
    w2d = lambda k: pl.BlockSpec((k, tn), lambda i, j: (0, j))
    copies = [w2d(D_MODEL), w2d(D_MODEL), w2d(ATTN_DIM), w2d(CONV_DIM)]
    if w_bf16 is not None:
        return pl.pallas_call(
            _merged_body, out_shape=out_shape, grid=grid, in_specs=rows + copies, out_specs=out_spec,
            compiler_params=_cp(2), name="merged",
        )(h, attn, s, *w_bf16)
    assert grid[0] == 1
    w_in, w_attn_br, w_conv_br = w_f32
    w3d = lambda k, col0: pl.BlockSpec((None, k, tn), lambda i, j: (l, 0, col0 // tn + j))
    copy_shape = lambda k: jax.ShapeDtypeStruct((k, D_MODEL), BF)
    out = pl.pallas_call(
        _merged_cast_body,
        out_shape=(out_shape, copy_shape(D_MODEL), copy_shape(D_MODEL), copy_shape(ATTN_DIM), copy_shape(CONV_DIM)),
        grid=grid,
        in_specs=rows + [w3d(D_MODEL, COL_GA), w3d(D_MODEL, COL_GC), w3d(ATTN_DIM, 0), w3d(CONV_DIM, 0)],
        out_specs=(out_spec, *copies),
        compiler_params=_cp(2), name="merged_cast",
    )(h, attn, s, w_in, w_in, w_attn_br, w_conv_br)
    return out[0], out[1:]


def _mm_resid(grp, a, x, gate_chunk, tm, tn, name, l=None, w_f32=None, w_bf16=None):
    k = a.shape[1]
    grid = (grp.m // tm, D_MODEL // tn)
    ins = [pl.BlockSpec((tm, k), lambda i, j: (i, 0)), None,
           pl.BlockSpec((tm, tn), lambda i, j: (i, j)), grp.param_spec(gate_chunk, tm, tn)]
    out_shape = jax.ShapeDtypeStruct((grp.m, D_MODEL), F32)
    out_spec = pl.BlockSpec((tm, tn), lambda i, j: (i, j))
    copy_spec = pl.BlockSpec((k, tn), lambda i, j: (0, j))
    if w_bf16 is not None:
        ins[1] = copy_spec
        return pl.pallas_call(
            _mm_resid_body, out_shape=out_shape, grid=grid, in_specs=ins, out_specs=out_spec,
            compiler_params=_cp(2), name=name,
        )(a, w_bf16, x, grp.ada)
    assert grid[0] == 1
    ins[1] = pl.BlockSpec((None, k, tn), lambda i, j: (l, 0, j))
    return pl.pallas_call(
        _mm_resid_cast_body, out_shape=(out_shape, jax.ShapeDtypeStruct((k, D_MODEL), BF)), grid=grid,
        in_specs=ins, out_specs=(out_spec, copy_spec),
        compiler_params=_cp(2), name=name + "_cast",
    )(a, w_f32, x, grp.ada)


def _ln(grp, r, g, b, l, mod_grp, mod_chunks, name):
    tr = 256
    row = pl.BlockSpec((tr, D_MODEL), lambda i: (i, 0))
    vec = pl.BlockSpec((None, 1, D_MODEL), lambda i: (l, 0, 0))
    g3 = g.reshape(DEPTH, 1, D_MODEL)
    b3 = b.reshape(DEPTH, 1, D_MODEL)
    if mod_chunks is None:
        return pl.pallas_call(
            _ln_body, out_shape=jax.ShapeDtypeStruct((grp.m, D_MODEL), F32), grid=(grp.m // tr,),
            in_specs=[row, vec, vec], out_specs=row, compiler_params=_cp(1), name=name,
        )(r, g3, b3), None
    sc, sh = mod_chunks
    return pl.pallas_call(
        _ln_mod_body,
        out_shape=(jax.ShapeDtypeStruct((grp.m, D_MODEL), F32), jax.ShapeDtypeStruct((grp.m, D_MODEL), BF)),
        grid=(grp.m // tr,),
        in_specs=[row, vec, vec, _row_param_spec(mod_grp, sc, tr), _row_param_spec(mod_grp, sh, tr)],
        out_specs=(row, row), compiler_params=_cp(1), name=name,
    )(r, g3, b3, mod_grp.ada, mod_grp.ada)


def _ffn_up(grp, h, w_up, ffn_conv_w, l, past_g, past_v):
    tm, tn = grp.tm_big, 256
    nj = D_FF // tn
    ins = [_resident_rows(tm, D_MODEL),
           pl.BlockSpec((None, D_MODEL, tn), lambda i, j: (l, 0, j)),
           pl.BlockSpec((None, D_MODEL, tn), lambda i, j: (l, 0, nj + j)),
           pl.BlockSpec((None, 3, tn), lambda i, j: (l, 0, j)),
           pl.BlockSpec((None, 3, tn), lambda i, j: (l, 0, nj + j))]
    tile = pl.BlockSpec((tm, tn), lambda i, j: (i, j))
    act_shape = jax.ShapeDtypeStruct((grp.m, D_FF), BF)
    if grp.per_row:
        full = jax.ShapeDtypeStruct((grp.m, D_FF), F32)
        return pl.pallas_call(
            functools.partial(_ffnup_sample_body, seg=grp.seq),
            out_shape=(act_shape, full, full), grid=(grp.m // tm, nj),
            in_specs=ins + [tile, tile], out_specs=(tile, tile, tile),
            compiler_params=_cp(2), name="ffn_up_sample",
        )(h, w_up, w_up, ffn_conv_w, ffn_conv_w, past_g, past_v)
    tail = jax.ShapeDtypeStruct((grp.m // tm, SUBLANES, D_FF), F32)
    tail_spec = pl.BlockSpec((None, SUBLANES, tn), lambda i, j: (i, 0, j))
    return pl.pallas_call(
        functools.partial(_ffnup_prompt_body, tiles_per_seq=grp.seq // tm),
        out_shape=(act_shape, tail, tail), grid=(grp.m // tm, nj),
        in_specs=ins, out_specs=(tile, tail_spec, tail_spec),
        scratch_shapes=[pltpu.VMEM((nj, SUBLANES, tn), F32), pltpu.VMEM((nj, SUBLANES, tn), F32),
                        pltpu.VMEM((D_MODEL, tn), BF), pltpu.VMEM((D_MODEL, tn), BF)],
        compiler_params=_cp(2), name="ffn_up_prompt",
    )(h, w_up, w_up, ffn_conv_w, ffn_conv_w)


def _rope_tables(pos):
    inv = ROPE_THETA ** (-jnp.arange(0, HEAD_DIM, 2, dtype=F32) / HEAD_DIM)
    ang = pos.astype(F32)[:, None] * inv[None, :]
    cos, sin = jnp.cos(ang), jnp.sin(ang)
    return jnp.concatenate([cos] * 4, axis=-1), jnp.concatenate([-sin, sin, -sin, sin], axis=-1)


def _expand_past(past, seq):
    batch, _, n = past.shape
    return jnp.concatenate([past, jnp.zeros((batch, seq - 2, n), past.dtype)], axis=1).reshape(batch * seq, n)


def _last_rows(x, batch, seq, n_rows):
    return x.reshape(batch, seq, x.shape[-1])[:, seq - n_rows:, :]


def _layer(grp, next_grp, l, x, h, rope, weights, sinks, past, copies):
    (w_in, conv_w, w_attn_br, w_conv_br, w_out, ln1_g, ln1_b, w_up, ffn_conv_w, w_down, ln2_g, ln2_b) = weights
    cos, sin = rope
    sample = grp.per_row
    q = _proj_rope(grp, h, w_in, l, COL_Q, ATTN_DIM, grp.tm_big, 256, cos, sin, 256 // LANES, HEAD_DIM ** -0.5,
                   F32 if sample else BF, "proj_q")
    kv = _proj_rope(grp, h, w_in, l, COL_KV, 2 * KV_DIM, grp.tm, 2 * KV_DIM, cos, sin, KV_DIM // LANES, 1.0,
                    F32, "proj_kv")
    if sample:
        cache_k, cache_v, state_conv, state_ffn = past
        attn, k_state, v_state = _attn_sample(sinks, q, kv, cache_k, cache_v, l, grp.batch, grp.seq)
        s, u = _in_conv(grp, h, w_in, conv_w, l, _expand_past(state_conv[l], grp.seq))
        conv_state = _last_rows(u, grp.batch, grp.seq, 2)
    else:
        attn = _attn_prompt(sinks, q, kv, grp.batch, grp.seq)
        kv_tail = _last_rows(kv, grp.batch, grp.seq, WINDOW)
        k_state, v_state = kv_tail[..., :KV_DIM], kv_tail[..., KV_DIM:]
        s, tails = _in_conv(grp, h, w_in, conv_w, l, None)
        tps = grp.seq // grp.tm_big
        conv_state = tails[tps - 1::tps, SUBLANES - 2:, :]
    if sample:
        copies = {}
        merged, copies["merged"] = _merged(grp, h, attn, s, l, w_f32=(w_in, w_attn_br, w_conv_br))
        r1, copies["out"] = _mm_resid(grp, merged, x, G1, grp.tm, 512, "out_proj", l=l, w_f32=w_out)
    else:
        merged = _merged(grp, h, attn, s, l, w_bf16=copies["merged"])
        r1 = _mm_resid(grp, merged, x, G1, grp.tm, 1024, "out_proj", w_bf16=copies["out"])
    x1, h2 = _ln(grp, r1, ln1_g, ln1_b, l, grp, (SC2, SH2), "ln1")
    if sample:
        eg = _expand_past(state_ffn[l][..., :D_FF], grp.seq)
        ev = _expand_past(state_ffn[l][..., D_FF:], grp.seq)
        act, ug, uv = _ffn_up(grp, h2, w_up, ffn_conv_w, l, eg, ev)
        ffn_state = _last_rows(jnp.concatenate([ug, uv], axis=-1), grp.batch, grp.seq, 2)
    else:
        act, tg, tv = _ffn_up(grp, h2, w_up, ffn_conv_w, l, None, None)
        ffn_state = jnp.concatenate([tg, tv], axis=-1)[tps - 1::tps, SUBLANES - 2:, :]
    if sample:
        r2, copies["down"] = _mm_resid(grp, act, x1, G2, grp.tm, 256, "ffn_down", l=l, w_f32=w_down)
    else:
        r2 = _mm_resid(grp, act, x1, G2, 512, 512, "ffn_down", w_bf16=copies["down"])
    x2, h_next = _ln(grp, r2, ln2_g, ln2_b, l, next_grp, None if next_grp is None else (SC1, SH1), "ln2")
    shape4 = (grp.batch, WINDOW, N_KV_HEADS, HEAD_DIM)
    return copies, x2, h_next, k_state.reshape(shape4), v_state.reshape(shape4), conv_state, ffn_state


def kernel(x_prompt, x_sample, cache_attn_k, cache_attn_v, state_conv, state_ffn_conv, c_prompt, c_sample, w_ada, b_ada, w_in, attn_sinks, conv_w, w_attn_br, w_conv_br, w_out, ln1_g, ln1_b, w_up, ffn_conv_w, w_down, ln2_g, ln2_b):
    bp, tp, _ = x_prompt.shape
    bs, ts, _ = x_sample.shape
    assert bp <= ADA_SAMPLE_ROW0 and ADA_SAMPLE_ROW0 + bs == ADA_ROWS

    c_all = jnp.concatenate([c_prompt, jnp.zeros((ADA_SAMPLE_ROW0 - bp, D_MODEL), F32), c_sample], axis=0)
    ada = _ada(c_all, w_ada, b_ada)

    weights = (w_in, conv_w, w_attn_br, w_conv_br, w_out, ln1_g, ln1_b, w_up, ffn_conv_w, w_down, ln2_g, ln2_b)

    rope_p = _rope_tables(jnp.arange(tp, dtype=jnp.int32))
    rope_s = tuple(jnp.tile(t, (bs, 1)) for t in _rope_tables(PAST_LEN + jnp.arange(ts, dtype=jnp.int32)))
    cache_k = cache_attn_k.reshape(DEPTH, bs, WINDOW, KV_DIM)
    cache_v = cache_attn_v.reshape(DEPTH, bs, WINDOW, KV_DIM)
    past = (cache_k, cache_v, state_conv, state_ffn_conv)

    xp = x_prompt.reshape(bp * tp, D_MODEL)
    xs = x_sample.reshape(bs * ts, D_MODEL)
    gps = [_Group(bp, tp, 1024, 2048, ada[l, :bp].reshape(bp, 1, -1)) for l in range(DEPTH)] + [None]
    gss = [_Group(bs, ts, bs * ts, bs * ts, jnp.repeat(ada[l, ADA_SAMPLE_ROW0:], ts, axis=0))
           for l in range(DEPTH)] + [None]
    hp = _modulate(gps[0], xp)
    hs = _modulate(gss[0], xs)
    outs_p, outs_s = [], []
    for l in range(DEPTH):
        copies, xs, hs, *st_s = _layer(gss[l], gss[l + 1], l, xs, hs, rope_s, weights, attn_sinks[l], past, None)
        _, xp, hp, *st_p = _layer(gps[l], gps[l + 1], l, xp, hp, rope_p, weights, attn_sinks[l], None, copies)
        outs_p.append(st_p)
        outs_s.append(st_s)
    stack = lambda outs, k: jnp.stack([o[k] for o in outs])
    return (xp.reshape(bp, tp, D_MODEL), xs.reshape(bs, ts, D_MODEL),
            stack(outs_p, 0), stack(outs_p, 1), stack(outs_p, 2), stack(outs_p, 3),
            stack(outs_s, 0), stack(outs_s, 1), stack(outs_s, 2), stack(outs_s, 3))
```

```python
import functools

import jax
import jax.numpy as jnp
from jax import lax
from jax.experimental import pallas as pl
from jax.experimental.pallas import tpu as pltpu

D_MODEL = 4096
DEPTH = 2
HEAD_DIM = 64
N_HEADS = 32
N_KV_HEADS = 4
ATTN_DIM = N_HEADS * HEAD_DIM
KV_DIM = N_KV_HEADS * HEAD_DIM
WINDOW = 128
ROPE_THETA = 10000.0
CONV_DIM = D_MODEL // 2
D_FF = 11008
PAST_LEN = 16384
LN_EPS = 1e-5
ALPHA = (2 * DEPTH) ** 0.25

COL_Q = 0
COL_KV = ATTN_DIM
COL_CB = ATTN_DIM + 2 * KV_DIM
COL_CC = COL_CB + CONV_DIM
COL_CX = COL_CC + CONV_DIM
COL_GA = COL_CX + CONV_DIM
COL_GC = COL_GA + D_MODEL

SH1, SC1, G1, SH2, SC2, G2 = range(6)

LANES = 128
SUBLANES = 8
ADA_ROWS = 40
ADA_SAMPLE_ROW0 = 8
ROW_CHUNK = 256
VMEM_LIMIT = 56 * 2**20
NEG = -1e30

BF = jnp.bfloat16
F32 = jnp.float32


def _cp(n_axes):
    return pltpu.CompilerParams(dimension_semantics=("arbitrary",) * n_axes, vmem_limit_bytes=VMEM_LIMIT)


def _bf(x):
    return x if x.dtype == BF else x.astype(BF)


def _dot(a, b):
    return jnp.dot(_bf(a), _bf(b), preferred_element_type=F32)


def _row_chunks(tm):
    chunk = min(tm, ROW_CHUNK)
    assert tm % chunk == 0
    return [pl.ds(r, chunk) for r in range(0, tm, chunk)]


def _staged(w_ref, stage_ref):
    stage_ref[...] = w_ref[...].astype(BF)
    return stage_ref


def _rows_of(ref, rows):
    return ref[...] if ref.shape[0] == 1 else ref[rows, :]


def _ada_body(c_ref, w_ref, b_ref, o_ref):
    c = c_ref[...]
    o_ref[...] = _dot(c * jax.nn.sigmoid(c), w_ref[...]) + b_ref[...]


def _mod_body(x_ref, sc_ref, sh_ref, o_ref):
    o_ref[...] = (x_ref[...] * (1.0 + sc_ref[...]) + sh_ref[...]).astype(o_ref.dtype)


def _proj_rope_body(h_ref, w_ref, cos_ref, sin_ref, o_ref, ws_ref, *, n_rope, scale):
    tn = o_ref.shape[1]
    w = _staged(w_ref, ws_ref)
    lane = lax.broadcasted_iota(jnp.int32, (min(h_ref.shape[0], ROW_CHUNK), LANES), 1)
    first_half = (lane & (HEAD_DIM - 1)) < HEAD_DIM // 2
    for rows in _row_chunks(h_ref.shape[0]):
        acc = _dot(h_ref[rows, :], w[...])
        cos = cos_ref[rows, :]
        sin = sin_ref[rows, :]
        for c in range(tn // LANES):
            ch = acc[:, c * LANES:(c + 1) * LANES]
            if c < n_rope:
                rot = jnp.where(first_half,
                                pltpu.roll(ch, LANES - HEAD_DIM // 2, axis=1),
                                pltpu.roll(ch, HEAD_DIM // 2, axis=1))
                ch = ch * cos + rot * sin
            if scale != 1.0:
                ch = ch * scale
            o_ref[rows, c * LANES:(c + 1) * LANES] = ch.astype(o_ref.dtype)


def _attend(q_ref, k2, v2, valid, sink_ref, o_ref):
    tq = q_ref.shape[0]
    nk = k2.shape[0]
    lo = lax.broadcasted_iota(jnp.int32, (nk, LANES), 1) < HEAD_DIM
    lo_q = lax.broadcasted_iota(jnp.int32, (tq, LANES), 1) < HEAD_DIM
    group = N_HEADS // N_KV_HEADS
    for m in range(KV_DIM // LANES):
        kc = k2[:, m * LANES:(m + 1) * LANES]
        vc = v2[:, m * LANES:(m + 1) * LANES]
        kr = pltpu.roll(kc, HEAD_DIM, axis=1)
        vr = pltpu.roll(vc, HEAD_DIM, axis=1)
        for hh in range(2):
            kv_head = 2 * m + hh
            k_lo, k_hi = (kc, kr) if hh == 0 else (kr, kc)
            v_lo, v_hi = (vc, vr) if hh == 0 else (vr, vc)
            kbd = jnp.concatenate([jnp.where(lo, k_lo, 0.0), jnp.where(lo, 0.0, k_hi)], axis=0).astype(BF)
            vbd = jnp.concatenate([jnp.where(lo, v_lo, 0.0), jnp.where(lo, 0.0, v_hi)], axis=0).astype(BF)
            pairs = [kv_head * (group // 2) + pp for pp in range(group // 2)]
            qs = _bf(jnp.concatenate([q_ref[:, p * LANES:(p + 1) * LANES] for p in pairs], axis=0))
            s_all = lax.dot_general(qs, kbd, (((1,), (1,)), ((), ())), preferred_element_type=F32)
            es, inv = [], []
            for pp, p in enumerate(pairs):
                s = s_all[pp * tq:(pp + 1) * tq, :]
                e_pair, d_pair = [], []
                for half in range(2):
                    sh = jnp.where(valid, s[:, half * nk:(half + 1) * nk], NEG)
                    sink = sink_ref[2 * p + half]
                    mx = jnp.maximum(jnp.max(sh, axis=-1, keepdims=True), sink)
                    e = jnp.exp(sh - mx)
                    e_pair.append(e)
                    d_pair.append(jnp.sum(e, axis=-1, keepdims=True) + jnp.exp(sink - mx))
                es.append(jnp.concatenate(e_pair, axis=1))
                inv.append(jnp.where(lo_q, 1.0 / d_pair[0], 1.0 / d_pair[1]))
            o_all = jnp.dot(_bf(jnp.concatenate(es, axis=0)), vbd, preferred_element_type=F32)
            for pp, p in enumerate(pairs):
                o = o_all[pp * tq:(pp + 1) * tq, :] * inv[pp]
                o_ref[:, p * LANES:(p + 1) * LANES] = o.astype(o_ref.dtype)


def _attn_prompt_body(sink_ref, q_ref, kvp_ref, kvc_ref, o_ref):
    n = pl.program_id(1)
    kvp = kvp_ref[...]
    kvc = kvc_ref[...]
    k2 = jnp.concatenate([kvp[:, :KV_DIM], kvc[:, :KV_DIM]], axis=0)
    v2 = jnp.concatenate([kvp[:, KV_DIM:], kvc[:, KV_DIM:]], axis=0)
    i = lax.broadcasted_iota(jnp.int32, (WINDOW, 2 * WINDOW), 0)
    j = lax.broadcasted_iota(jnp.int32, (WINDOW, 2 * WINDOW), 1)
    valid = (j >= i) & (j <= i + WINDOW) & ((j >= WINDOW) | (n > 0))
    _attend(q_ref, k2, v2, valid, sink_ref, o_ref)


def _attn_sample_body(sink_ref, q_ref, kvn_ref, ck_ref, cv_ref, o_ref, ks_ref, vs_ref):
    t = q_ref.shape[0]
    ck = ck_ref[...]
    cv = cv_ref[...]
    kvn = kvn_ref[...]
    pad = jnp.zeros((WINDOW - t, KV_DIM), F32)
    k2 = jnp.concatenate([ck, kvn[:, :KV_DIM], pad], axis=0)
    v2 = jnp.concatenate([cv, kvn[:, KV_DIM:], pad], axis=0)
    i = lax.broadcasted_iota(jnp.int32, (t, 2 * WINDOW), 0)
    j = lax.broadcasted_iota(jnp.int32, (t, 2 * WINDOW), 1)
    valid = (j >= i) & (j <= i + WINDOW)
    _attend(q_ref, k2, v2, valid, sink_ref, o_ref)
    ks_ref[0:WINDOW - t, :] = ck[t:, :]
    ks_ref[WINDOW - t:, :] = kvn[:, :KV_DIM]
    vs_ref[0:WINDOW - t, :] = cv[t:, :]
    vs_ref[WINDOW - t:, :] = kvn[:, KV_DIM:]


def _conv3_carry(u, w_ref, prev):
    ext = jnp.concatenate([prev, u], axis=0)
    x1 = pltpu.roll(ext, 1, axis=0)[SUBLANES:, :]
    x2 = pltpu.roll(ext, 2, axis=0)[SUBLANES:, :]
    return w_ref[0:1, :] * x2 + w_ref[1:2, :] * x1 + w_ref[2:3, :] * u


def _conv3_segments(u, w_ref, e, seg):
    t = lax.broadcasted_iota(jnp.int32, u.shape, 0) & (seg - 1)
    x1 = jnp.where(t >= 1, pltpu.roll(u, 1, axis=0), pltpu.roll(e, u.shape[0] - 1, axis=0))
    x2 = jnp.where(t >= 2, pltpu.roll(u, 2, axis=0), e)
    return w_ref[0:1, :] * x2 + w_ref[1:2, :] * x1 + w_ref[2:3, :] * u


def _carry_in(carry_ref, tiles_per_seq):
    i = pl.program_id(0)
    j = pl.program_id(1)

    @pl.when(i % tiles_per_seq == 0)
    def _():
        carry_ref[j] = jnp.zeros(carry_ref.shape[1:], F32)

    return carry_ref[j]


def _chunks_with_riders(h_ref, hs_ref):
    chunks = _row_chunks(h_ref.shape[0])
    out = []
    for c, rows in enumerate(chunks):
        lhs = h_ref[rows, :]
        if c == len(chunks) - 1:
            lhs = jnp.concatenate([lhs, hs_ref[...]], axis=0)
        out.append((rows, lhs, min(h_ref.shape[0], ROW_CHUNK)))
    return out


def _inconv_body(h_ref, hs_ref, wb_ref, wc_ref, wx_ref, cw_ref, e_ref, s_ref, tail_ref, ss_ref, us_ref,
                 carry_ref, wbs_ref, wcs_ref, wxs_ref, *, tiles_per_seq, seg):
    j = pl.program_id(1)
    wb = _staged(wb_ref, wbs_ref)
    wc = _staged(wc_ref, wcs_ref)
    wx = _staged(wx_ref, wxs_ref)
    prev = _carry_in(carry_ref, tiles_per_seq)
    for rows, lhs, n in _chunks_with_riders(h_ref, hs_ref):
        u_all = _dot(lhs, wc[...]) * _dot(lhs, wx[...])
        b_all = _dot(lhs, wb[...])
        u = u_all[:n, :]
        y = _conv3_carry(u, cw_ref, prev)
        prev = u[n - SUBLANES:, :]
        s_ref[rows, :] = (b_all[:n, :] * y).astype(s_ref.dtype)
        if u_all.shape[0] > n:
            us = u_all[n:, :]
            us_ref[...] = us
            ss_ref[...] = (b_all[n:, :] * _conv3_segments(us, cw_ref, e_ref[...], seg)).astype(ss_ref.dtype)
    carry_ref[j] = prev
    tail_ref[...] = prev


def _merged_body(h_ref, a_ref, s_ref, wga_ref, wgc_ref, wab_ref, wcb_ref, o_ref):
    for rows in _row_chunks(h_ref.shape[0]):
        h = h_ref[rows, :]
        attn_br = jax.nn.sigmoid(_dot(h, wga_ref[...])) * _dot(a_ref[rows, :], wab_ref[...])
        conv_br = jax.nn.sigmoid(_dot(h, wgc_ref[...])) * _dot(s_ref[rows, :], wcb_ref[...])
        o_ref[rows, :] = (attn_br + conv_br).astype(o_ref.dtype)


def _merged_cast_body(h_ref, a_ref, s_ref, wga_ref, wgc_ref, wab_ref, wcb_ref, o_ref, ga_ref, gc_ref, ab_ref, cb_ref):
    for w_ref, b_ref in ((wga_ref, ga_ref), (wgc_ref, gc_ref), (wab_ref, ab_ref), (wcb_ref, cb_ref)):
        b_ref[...] = w_ref[...].astype(BF)
    _merged_body(h_ref, a_ref, s_ref, ga_ref, gc_ref, ab_ref, cb_ref, o_ref)


def _mm_resid_body(a_ref, w_ref, x_ref, g_ref, o_ref):
    for rows in _row_chunks(a_ref.shape[0]):
        o_ref[rows, :] = ALPHA * x_ref[rows, :] + _rows_of(g_ref, rows) * _dot(a_ref[rows, :], w_ref[...])


def _mm_resid_cast_body(a_ref, w_ref, x_ref, g_ref, o_ref, wb_ref):
    wb_ref[...] = w_ref[...].astype(BF)
    _mm_resid_body(a_ref, wb_ref, x_ref, g_ref, o_ref)


def _layernorm(r, g, b):
    mu = jnp.mean(r, axis=-1, keepdims=True)
    d = r - mu
    var = jnp.mean(d * d, axis=-1, keepdims=True)
    return d * lax.rsqrt(var + LN_EPS) * g + b


def _ln_mod_body(r_ref, g_ref, b_ref, sc_ref, sh_ref, x_ref, h_ref):
    y = _layernorm(r_ref[...], g_ref[...], b_ref[...])
    x_ref[...] = y
    h_ref[...] = (y * (1.0 + sc_ref[...]) + sh_ref[...]).astype(h_ref.dtype)


def _ln_body(r_ref, g_ref, b_ref, x_ref):
    x_ref[...] = _layernorm(r_ref[...], g_ref[...], b_ref[...])


def _silu_mul(g, v):
    return g * jax.nn.sigmoid(g) * v


def _ffnup_body(h_ref, hs_ref, wg_ref, wv_ref, cwg_ref, cwv_ref, eg_ref, ev_ref,
                act_ref, tg_ref, tv_ref, acts_ref, ugs_ref, uvs_ref,
                cg_ref, cv_ref, wgs_ref, wvs_ref, *, tiles_per_seq, seg):
    j = pl.program_id(1)
    wg = _staged(wg_ref, wgs_ref)
    wv = _staged(wv_ref, wvs_ref)
    prev_g = _carry_in(cg_ref, tiles_per_seq)
    prev_v = _carry_in(cv_ref, tiles_per_seq)
    for rows, lhs, n in _chunks_with_riders(h_ref, hs_ref):
        ug_all = _dot(lhs, wg[...])
        uv_all = _dot(lhs, wv[...])
        ug = ug_all[:n, :]
        uv = uv_all[:n, :]
        yg = _conv3_carry(ug, cwg_ref, prev_g)
        yv = _conv3_carry(uv, cwv_ref, prev_v)
        prev_g = ug[n - SUBLANES:, :]
        prev_v = uv[n - SUBLANES:, :]
        act_ref[rows, :] = _silu_mul(yg, yv).astype(act_ref.dtype)
        if ug_all.shape[0] > n:
            ugs = ug_all[n:, :]
            uvs = uv_all[n:, :]
            ugs_ref[...] = ugs
            uvs_ref[...] = uvs
            ygs = _conv3_segments(ugs, cwg_ref, eg_ref[...], seg)
            yvs = _conv3_segments(uvs, cwv_ref, ev_ref[...], seg)
            acts_ref[...] = _silu_mul(ygs, yvs).astype(acts_ref.dtype)
    cg_ref[j] = prev_g
    cv_ref[j] = prev_v
    tg_ref[...] = prev_g
    tv_ref[...] = prev_v


class _Group:
    def __init__(self, batch, seq, tm, tm_big, ada_rows):
        self.batch, self.seq, self.tm, self.tm_big = batch, seq, tm, tm_big
        self.m = batch * seq
        self.per_row = seq < tm
        self.ada = ada_rows

    def param_spec(self, chunk, tm, tn):
        cb = chunk * (D_MODEL // tn)
        if self.per_row:
            return pl.BlockSpec((tm, tn), lambda i, j: (i, cb + j))
        tps = self.seq // tm
        return pl.BlockSpec((None, 1, tn), lambda i, j: (i // tps, 0, cb + j))


def _resident_rows(tm, width):
    return pl.BlockSpec((tm, width), lambda i, j: (i, 0), pipeline_mode=pl.Buffered(1))


def _ada(c_all, w_ada, b_ada):
    tn = 512
    n = w_ada.shape[-1]
    return pl.pallas_call(
        _ada_body,
        out_shape=jax.ShapeDtypeStruct((DEPTH, ADA_ROWS, n), F32),
        grid=(DEPTH, n // tn),
        in_specs=[pl.BlockSpec((ADA_ROWS, D_MODEL), lambda l, j: (0, 0)),
                  pl.BlockSpec((None, D_MODEL, tn), lambda l, j: (l, 0, j)),
                  pl.BlockSpec((None, 1, tn), lambda l, j: (l, 0, j))],
        out_specs=pl.BlockSpec((None, ADA_ROWS, tn), lambda l, j: (l, 0, j)),
        compiler_params=_cp(2), name="ada",
    )(c_all, w_ada, b_ada.reshape(DEPTH, 1, n))


def _modulate(grp, x):
    tr = min(grp.tm, 512)
    return pl.pallas_call(
        _mod_body,
        out_shape=jax.ShapeDtypeStruct((grp.m, D_MODEL), BF),
        grid=(grp.m // tr,),
        in_specs=[pl.BlockSpec((tr, D_MODEL), lambda i: (i, 0)),
                  _row_param_spec(grp, SC1, tr), _row_param_spec(grp, SH1, tr)],
        out_specs=pl.BlockSpec((tr, D_MODEL), lambda i: (i, 0)),
        compiler_params=_cp(1), name="modulate",
    )(x, grp.ada, grp.ada)


def _row_param_spec(grp, chunk, tr):
    if grp.per_row:
        return pl.BlockSpec((tr, D_MODEL), lambda i: (i, chunk))
    tps = grp.seq // tr
    return pl.BlockSpec((None, 1, D_MODEL), lambda i: (i // tps, 0, chunk))


def _proj_rope(grp, h, w_in, l, col0, n_cols, tm, tn, cos, sin, n_rope, scale, out_dtype, name):
    cb = col0 // tn
    t_tiles = cos.shape[0] // tm
    return pl.pallas_call(
        functools.partial(_proj_rope_body, n_rope=n_rope, scale=scale),
        out_shape=jax.ShapeDtypeStruct((grp.m, n_cols), out_dtype),
        grid=(grp.m // tm, n_cols // tn),
        in_specs=[pl.BlockSpec((tm, D_MODEL), lambda i, j: (i, 0)),
                  pl.BlockSpec((None, D_MODEL, tn), lambda i, j: (l, 0, cb + j)),
                  pl.BlockSpec((tm, LANES), lambda i, j: (i % t_tiles, 0)),
                  pl.BlockSpec((tm, LANES), lambda i, j: (i % t_tiles, 0))],
        out_specs=pl.BlockSpec((tm, tn), lambda i, j: (i, j)),
        scratch_shapes=[pltpu.VMEM((D_MODEL, tn), BF)],
        compiler_params=_cp(2), name=name,
    )(h, w_in, cos, sin)


def _attn_prompt(sinks, q, kv, batch, seq):
    nb = seq // WINDOW
    return pl.pallas_call(
        _attn_prompt_body,
        out_shape=jax.ShapeDtypeStruct(q.shape, BF),
        grid=(batch, nb),
        in_specs=[pl.BlockSpec(memory_space=pltpu.SMEM),
                  pl.BlockSpec((WINDOW, ATTN_DIM), lambda b, n: (b * nb + n, 0)),
                  pl.BlockSpec((WINDOW, 2 * KV_DIM), lambda b, n: (b * nb + jnp.maximum(n - 1, 0), 0)),
                  pl.BlockSpec((WINDOW, 2 * KV_DIM), lambda b, n: (b * nb + n, 0))],
        out_specs=pl.BlockSpec((WINDOW, ATTN_DIM), lambda b, n: (b * nb + n, 0)),
        compiler_params=_cp(2), name="attn_prompt",
    )(sinks, q, kv, kv)


def _attn_sample(sinks, q, kvn, cache_k, cache_v, l, batch, seq):
    return pl.pallas_call(
        _attn_sample_body,
        out_shape=(jax.ShapeDtypeStruct(q.shape, F32),
                   jax.ShapeDtypeStruct((batch, WINDOW, KV_DIM), F32),
                   jax.ShapeDtypeStruct((batch, WINDOW, KV_DIM), F32)),
        grid=(batch,),
        in_specs=[pl.BlockSpec(memory_space=pltpu.SMEM),
                  pl.BlockSpec((seq, ATTN_DIM), lambda b: (b, 0)),
                  pl.BlockSpec((seq, 2 * KV_DIM), lambda b: (b, 0)),
                  pl.BlockSpec((None, None, WINDOW, KV_DIM), lambda b: (l, b, 0, 0)),
                  pl.BlockSpec((None, None, WINDOW, KV_DIM), lambda b: (l, b, 0, 0))],
        out_specs=(pl.BlockSpec((seq, ATTN_DIM), lambda b: (b, 0)),
                   pl.BlockSpec((None, WINDOW, KV_DIM), lambda b: (b, 0, 0)),
                   pl.BlockSpec((None, WINDOW, KV_DIM), lambda b: (b, 0, 0))),
        compiler_params=_cp(1), name="attn_sample",
    )(sinks, q, kvn, cache_k, cache_v)


def _rider_rows(gp, gs, tm):
    rs = gs.m // (gp.m // tm)
    assert rs * (gp.m // tm) == gs.m and rs % gs.seq == 0 and rs % (2 * SUBLANES) == 0
    return rs


def _in_conv(gp, gs, h, hs, w_in, conv_w, l, past_rows):
    tm, tn = gp.tm_big, 256
    nj = CONV_DIM // tn
    rs = _rider_rows(gp, gs, tm)
    w_spec = lambda col0: pl.BlockSpec((None, D_MODEL, tn), lambda i, j: (l, 0, col0 // tn + j))
    tile = lambda rows: pl.BlockSpec((rows, tn), lambda i, j: (i, j))
    return pl.pallas_call(
        functools.partial(_inconv_body, tiles_per_seq=gp.seq // tm, seg=gs.seq),
        out_shape=(jax.ShapeDtypeStruct((gp.m, CONV_DIM), BF),
                   jax.ShapeDtypeStruct((gp.m // tm, SUBLANES, CONV_DIM), F32),
                   jax.ShapeDtypeStruct((gs.m, CONV_DIM), BF),
                   jax.ShapeDtypeStruct((gs.m, CONV_DIM), F32)),
        grid=(gp.m // tm, nj),
        in_specs=[_resident_rows(tm, D_MODEL), pl.BlockSpec((rs, D_MODEL), lambda i, j: (i, 0)),
                  w_spec(COL_CB), w_spec(COL_CC), w_spec(COL_CX),
                  pl.BlockSpec((None, 3, tn), lambda i, j: (l, 0, j)), tile(rs)],
        out_specs=(tile(tm), pl.BlockSpec((None, SUBLANES, tn), lambda i, j: (i, 0, j)), tile(rs), tile(rs)),
        scratch_shapes=[pltpu.VMEM((nj, SUBLANES, tn), F32)] + [pltpu.VMEM((D_MODEL, tn), BF)] * 3,
        compiler_params=_cp(2), name="in_conv",
    )(h, hs, w_in, w_in, w_in, conv_w, past_rows)


def _merged(grp, h, attn, s, l, w_f32=None, w_bf16=None):
    tm, tn = grp.tm, 256
    grid = (grp.m // tm, D_MODEL // tn)
    row = lambda width: pl.BlockSpec((tm, width), lambda i, j: (i, 0))
    rows = [row(D_MODEL), row(ATTN_DIM), row(CONV_DIM)]
    out_shape = jax.ShapeDtypeStruct((grp.m, D_MODEL), BF)
    out_spec = pl.BlockSpec((tm, tn), lambda i, j: (i, j))
    w2d = lambda k: pl.BlockSpec((k, tn), lambda i, j: (0, j))
    copies = [w2d(D_MODEL), w2d(D_MODEL), w2d(ATTN_DIM), w2d(CONV_DIM)]
    if w_bf16 is not None:
        return pl.pallas_call(
            _merged_body, out_shape=out_shape, grid=grid, in_specs=rows + copies, out_specs=out_spec,
            compiler_params=_cp(2), name="merged",
        )(h, attn, s, *w_bf16)
    assert grid[0] == 1
    w_in, w_attn_br, w_conv_br = w_f32
    w3d = lambda k, col0: pl.BlockSpec((None, k, tn), lambda i, j: (l, 0, col0 // tn + j))
    copy_shape = lambda k: jax.ShapeDtypeStruct((k, D_MODEL), BF)
    out = pl.pallas_call(
        _merged_cast_body,
        out_shape=(out_shape, copy_shape(D_MODEL), copy_shape(D_MODEL), copy_shape(ATTN_DIM), copy_shape(CONV_DIM)),
        grid=grid,
        in_specs=rows + [w3d(D_MODEL, COL_GA), w3d(D_MODEL, COL_GC), w3d(ATTN_DIM, 0), w3d(CONV_DIM, 0)],
        out_specs=(out_spec, *copies),
        compiler_params=_cp(2), name="merged_cast",
    )(h, attn, s, w_in, w_in, w_attn_br, w_conv_br)
    return out[0], out[1:]


def _mm_resid(grp, a, x, gate_chunk, tm, tn, name, l=None, w_f32=None, w_bf16=None):
    k = a.shape[1]
    grid = (grp.m // tm, D_MODEL // tn)
    ins = [pl.BlockSpec((tm, k), lambda i, j: (i, 0)), None,
           pl.BlockSpec((tm, tn), lambda i, j: (i, j)), grp.param_spec(gate_chunk, tm, tn)]
    out_shape = jax.ShapeDtypeStruct((grp.m, D_MODEL), F32)
    out_spec = pl.BlockSpec((tm, tn), lambda i, j: (i, j))
    copy_spec = pl.BlockSpec((k, tn), lambda i, j: (0, j))
    if w_bf16 is not None:
        ins[1] = copy_spec
        return pl.pallas_call(
            _mm_resid_body, out_shape=out_shape, grid=grid, in_specs=ins, out_specs=out_spec,
            compiler_params=_cp(2), name=name,
        )(a, w_bf16, x, grp.ada)
    assert grid[0] == 1
    ins[1] = pl.BlockSpec((None, k, tn), lambda i, j: (l, 0, j))
    return pl.pallas_call(
        _mm_resid_cast_body, out_shape=(out_shape, jax.ShapeDtypeStruct((k, D_MODEL), BF)), grid=grid,
        in_specs=ins, out_specs=(out_spec, copy_spec),
        compiler_params=_cp(2), name=name + "_cast",
    )(a, w_f32, x, grp.ada)


def _ln(grp, r, g, b, l, mod_grp, mod_chunks, name):
    tr = 256
    row = pl.BlockSpec((tr, D_MODEL), lambda i: (i, 0))
    vec = pl.BlockSpec((None, 1, D_MODEL), lambda i: (l, 0, 0))
    g3 = g.reshape(DEPTH, 1, D_MODEL)
    b3 = b.reshape(DEPTH, 1, D_MODEL)
    if mod_chunks is None:
        return pl.pallas_call(
            _ln_body, out_shape=jax.ShapeDtypeStruct((grp.m, D_MODEL), F32), grid=(grp.m // tr,),
            in_specs=[row, vec, vec], out_specs=row, compiler_params=_cp(1), name=name,
        )(r, g3, b3), None
    sc, sh = mod_chunks
    return pl.pallas_call(
        _ln_mod_body,
        out_shape=(jax.ShapeDtypeStruct((grp.m, D_MODEL), F32), jax.ShapeDtypeStruct((grp.m, D_MODEL), BF)),
        grid=(grp.m // tr,),
        in_specs=[row, vec, vec, _row_param_spec(mod_grp, sc, tr), _row_param_spec(mod_grp, sh, tr)],
        out_specs=(row, row), compiler_params=_cp(1), name=name,
    )(r, g3, b3, mod_grp.ada, mod_grp.ada)


def _ffn_up(gp, gs, h, hs, w_up, ffn_conv_w, l, past_g, past_v):
    tm, tn = gp.tm_big, 256
    nj = D_FF // tn
    rs = _rider_rows(gp, gs, tm)
    tile = lambda rows: pl.BlockSpec((rows, tn), lambda i, j: (i, j))
    tail_shape = jax.ShapeDtypeStruct((gp.m // tm, SUBLANES, D_FF), F32)
    tail_spec = pl.BlockSpec((None, SUBLANES, tn), lambda i, j: (i, 0, j))
    full_s = jax.ShapeDtypeStruct((gs.m, D_FF), F32)
    return pl.pallas_call(
        functools.partial(_ffnup_body, tiles_per_seq=gp.seq // tm, seg=gs.seq),
        out_shape=(jax.ShapeDtypeStruct((gp.m, D_FF), BF), tail_shape, tail_shape,
                   jax.ShapeDtypeStruct((gs.m, D_FF), BF), full_s, full_s),
        grid=(gp.m // tm, nj),
        in_specs=[_resident_rows(tm, D_MODEL), pl.BlockSpec((rs, D_MODEL), lambda i, j: (i, 0)),
                  pl.BlockSpec((None, D_MODEL, tn), lambda i, j: (l, 0, j)),
                  pl.BlockSpec((None, D_MODEL, tn), lambda i, j: (l, 0, nj + j)),
                  pl.BlockSpec((None, 3, tn), lambda i, j: (l, 0, j)),
                  pl.BlockSpec((None, 3, tn), lambda i, j: (l, 0, nj + j)),
                  tile(rs), tile(rs)],
        out_specs=(tile(tm), tail_spec, tail_spec, tile(rs), tile(rs), tile(rs)),
        scratch_shapes=[pltpu.VMEM((nj, SUBLANES, tn), F32), pltpu.VMEM((nj, SUBLANES, tn), F32),
                        pltpu.VMEM((D_MODEL, tn), BF), pltpu.VMEM((D_MODEL, tn), BF)],
        compiler_params=_cp(2), name="ffn_up",
    )(h, hs, w_up, w_up, ffn_conv_w, ffn_conv_w, past_g, past_v)


def _rope_tables(pos):
    inv = ROPE_THETA ** (-jnp.arange(0, HEAD_DIM, 2, dtype=F32) / HEAD_DIM)
    ang = pos.astype(F32)[:, None] * inv[None, :]
    cos, sin = jnp.cos(ang), jnp.sin(ang)
    return jnp.concatenate([cos] * 4, axis=-1), jnp.concatenate([-sin, sin, -sin, sin], axis=-1)


def _expand_past(past, seq):
    batch, _, n = past.shape
    return jnp.concatenate([past, jnp.zeros((batch, seq - 2, n), past.dtype)], axis=1).reshape(batch * seq, n)


def _last_rows(x, batch, seq, n_rows):
    return x.reshape(batch, seq, x.shape[-1])[:, seq - n_rows:, :]


def _qkv(grp, l, h, rope, w_in):
    cos, sin = rope
    q = _proj_rope(grp, h, w_in, l, COL_Q, ATTN_DIM, grp.tm_big, 256, cos, sin, 256 // LANES, HEAD_DIM ** -0.5,
                   F32 if grp.per_row else BF, "proj_q")
    kv = _proj_rope(grp, h, w_in, l, COL_KV, 2 * KV_DIM, grp.tm, 2 * KV_DIM, cos, sin, KV_DIM // LANES, 1.0,
                    F32, "proj_kv")
    return q, kv


def _layer(gp, gs, next_gp, next_gs, l, xp, xs, hp, hs, rope_p, rope_s, weights, sinks, past):
    (w_in, conv_w, w_attn_br, w_conv_br, w_out, ln1_g, ln1_b, w_up, ffn_conv_w, w_down, ln2_g, ln2_b) = weights
    cache_k, cache_v, state_conv, state_ffn = past
    tps = gp.seq // gp.tm_big
    state_rows = lambda tails: tails[tps - 1::tps, SUBLANES - 2:, :]

    q_s, kv_s = _qkv(gs, l, hs, rope_s, w_in)
    attn_s, k_s, v_s = _attn_sample(sinks, q_s, kv_s, cache_k, cache_v, l, gs.batch, gs.seq)
    q_p, kv_p = _qkv(gp, l, hp, rope_p, w_in)
    attn_p = _attn_prompt(sinks, q_p, kv_p, gp.batch, gp.seq)
    kv_tail = _last_rows(kv_p, gp.batch, gp.seq, WINDOW)
    k_p, v_p = kv_tail[..., :KV_DIM], kv_tail[..., KV_DIM:]

    s_p, tails, s_s, u_s = _in_conv(gp, gs, hp, hs, w_in, conv_w, l, _expand_past(state_conv[l], gs.seq))
    conv_p = state_rows(tails)
    conv_s = _last_rows(u_s, gs.batch, gs.seq, 2)

    merged_s, w_merged = _merged(gs, hs, attn_s, s_s, l, w_f32=(w_in, w_attn_br, w_conv_br))
    r1_s, w_out_b = _mm_resid(gs, merged_s, xs, G1, gs.tm, 512, "out_proj", l=l, w_f32=w_out)
    merged_p = _merged(gp, hp, attn_p, s_p, l, w_bf16=w_merged)
    r1_p = _mm_resid(gp, merged_p, xp, G1, gp.tm, 1024, "out_proj", w_bf16=w_out_b)
    x1_s, h2_s = _ln(gs, r1_s, ln1_g, ln1_b, l, gs, (SC2, SH2), "ln1")
    x1_p, h2_p = _ln(gp, r1_p, ln1_g, ln1_b, l, gp, (SC2, SH2), "ln1")

    eg = _expand_past(state_ffn[l][..., :D_FF], gs.seq)
    ev = _expand_past(state_ffn[l][..., D_FF:], gs.seq)
    act_p, tg, tv, act_s, ug_s, uv_s = _ffn_up(gp, gs, h2_p, h2_s, w_up, ffn_conv_w, l, eg, ev)
    ffn_p = state_rows(jnp.concatenate([tg, tv], axis=-1))
    ffn_s = _last_rows(jnp.concatenate([ug_s, uv_s], axis=-1), gs.batch, gs.seq, 2)

    r2_s, w_down_b = _mm_resid(gs, act_s, x1_s, G2, gs.tm, 256, "ffn_down", l=l, w_f32=w_down)
    r2_p = _mm_resid(gp, act_p, x1_p, G2, 512, 512, "ffn_down", w_bf16=w_down_b)
    mod = None if next_gp is None else (SC1, SH1)
    xs, hs = _ln(gs, r2_s, ln2_g, ln2_b, l, next_gs, mod, "ln2")
    xp, hp = _ln(gp, r2_p, ln2_g, ln2_b, l, next_gp, mod, "ln2")

    heads = lambda k, grp: k.reshape(grp.batch, WINDOW, N_KV_HEADS, HEAD_DIM)
    return (xp, xs, hp, hs, (heads(k_p, gp), heads(v_p, gp), conv_p, ffn_p),
            (heads(k_s, gs), heads(v_s, gs), conv_s, ffn_s))


def kernel(x_prompt, x_sample, cache_attn_k, cache_attn_v, state_conv, state_ffn_conv, c_prompt, c_sample, w_ada, b_ada, w_in, attn_sinks, conv_w, w_attn_br, w_conv_br, w_out, ln1_g, ln1_b, w_up, ffn_conv_w, w_down, ln2_g, ln2_b):
    bp, tp, _ = x_prompt.shape
    bs, ts, _ = x_sample.shape
    assert bp <= ADA_SAMPLE_ROW0 and ADA_SAMPLE_ROW0 + bs == ADA_ROWS

    c_all = jnp.concatenate([c_prompt, jnp.zeros((ADA_SAMPLE_ROW0 - bp, D_MODEL), F32), c_sample], axis=0)
    ada = _ada(c_all, w_ada, b_ada)

    weights = (w_in, conv_w, w_attn_br, w_conv_br, w_out, ln1_g, ln1_b, w_up, ffn_conv_w, w_down, ln2_g, ln2_b)
    rope_p = _rope_tables(jnp.arange(tp, dtype=jnp.int32))
    rope_s = tuple(jnp.tile(t, (bs, 1)) for t in _rope_tables(PAST_LEN + jnp.arange(ts, dtype=jnp.int32)))
    cache_k = cache_attn_k.reshape(DEPTH, bs, WINDOW, KV_DIM)
    cache_v = cache_attn_v.reshape(DEPTH, bs, WINDOW, KV_DIM)
    past = (cache_k, cache_v, state_conv, state_ffn_conv)

    xp = x_prompt.reshape(bp * tp, D_MODEL)
    xs = x_sample.reshape(bs * ts, D_MODEL)
    gps = [_Group(bp, tp, 1024, 2048, ada[l, :bp].reshape(bp, 1, -1)) for l in range(DEPTH)] + [None]
    gss = [_Group(bs, ts, bs * ts, bs * ts, jnp.repeat(ada[l, ADA_SAMPLE_ROW0:], ts, axis=0))
           for l in range(DEPTH)] + [None]
    hp = _modulate(gps[0], xp)
    hs = _modulate(gss[0], xs)
    outs_p, outs_s = [], []
    for l in range(DEPTH):
        xp, xs, hp, hs, st_p, st_s = _layer(gps[l], gss[l], gps[l + 1], gss[l + 1], l, xp, xs, hp, hs,
                                            rope_p, rope_s, weights, attn_sinks[l], past)
        outs_p.append(st_p)
        outs_s.append(st_s)
    stack = lambda outs, k: jnp.stack([o[k] for o in outs])
    return (xp.reshape(bp, tp, D_MODEL), xs.reshape(bs, ts, D_MODEL),
            stack(outs_p, 0), stack(outs_p, 1), stack(outs_p, 2), stack(outs_p, 3),
            stack(outs_s, 0), stack(outs_s, 1), stack(outs_s, 2), stack(outs_s, 3))
```

```python
import functools

import jax
import jax.numpy as jnp
from jax import lax
from jax.experimental import pallas as pl
from jax.experimental.pallas import tpu as pltpu

D_MODEL = 4096
DEPTH = 2
HEAD_DIM = 64
N_HEADS = 32
N_KV_HEADS = 4
ATTN_DIM = N_HEADS * HEAD_DIM
KV_DIM = N_KV_HEADS * HEAD_DIM
WINDOW = 128
ROPE_THETA = 10000.0
CONV_DIM = D_MODEL // 2
D_FF = 11008
PAST_LEN = 16384
LN_EPS = 1e-5
ALPHA = (2 * DEPTH) ** 0.25

COL_Q = 0
COL_KV = ATTN_DIM
COL_CB = ATTN_DIM + 2 * KV_DIM
COL_CC = COL_CB + CONV_DIM
COL_CX = COL_CC + CONV_DIM
COL_GA = COL_CX + CONV_DIM
COL_GC = COL_GA + D_MODEL

SH1, SC1, G1, SH2, SC2, G2 = range(6)

LANES = 128
SUBLANES = 8
ADA_ROWS = 40
ADA_SAMPLE_ROW0 = 8
ROW_CHUNK = 256
VMEM_LIMIT = 56 * 2**20
NEG = -1e30

BF = jnp.bfloat16
F32 = jnp.float32


def _cp(n_axes):
    return pltpu.CompilerParams(dimension_semantics=("arbitrary",) * n_axes, vmem_limit_bytes=VMEM_LIMIT)


def _bf(x):
    return x if x.dtype == BF else x.astype(BF)


def _dot(a, b):
    return jnp.dot(_bf(a), _bf(b), preferred_element_type=F32)


def _row_chunks(tm):
    chunk = min(tm, ROW_CHUNK)
    assert tm % chunk == 0
    return [pl.ds(r, chunk) for r in range(0, tm, chunk)]


def _staged(w_ref, stage_ref):
    stage_ref[...] = w_ref[...].astype(BF)
    return stage_ref


def _rows_of(ref, rows):
    return ref[...] if ref.shape[0] == 1 else ref[rows, :]


def _ada_body(c_ref, w_ref, b_ref, o_ref):
    c = c_ref[...]
    o_ref[...] = _dot(c * jax.nn.sigmoid(c), w_ref[...]) + b_ref[...]


def _mod_body(x_ref, sc_ref, sh_ref, o_ref):
    o_ref[...] = (x_ref[...] * (1.0 + sc_ref[...]) + sh_ref[...]).astype(o_ref.dtype)


def _proj_rope_body(h_ref, w_ref, cos_ref, sin_ref, o_ref, ws_ref, *, n_rope, scale):
    tn = o_ref.shape[1]
    w = _staged(w_ref, ws_ref)
    lane = lax.broadcasted_iota(jnp.int32, (min(h_ref.shape[0], ROW_CHUNK), LANES), 1)
    first_half = (lane & (HEAD_DIM - 1)) < HEAD_DIM // 2
    for rows in _row_chunks(h_ref.shape[0]):
        acc = _dot(h_ref[rows, :], w[...])
        cos = cos_ref[rows, :]
        sin = sin_ref[rows, :]
        for c in range(tn // LANES):
            ch = acc[:, c * LANES:(c + 1) * LANES]
            if c < n_rope:
                rot = jnp.where(first_half,
                                pltpu.roll(ch, LANES - HEAD_DIM // 2, axis=1),
                                pltpu.roll(ch, HEAD_DIM // 2, axis=1))
                ch = ch * cos + rot * sin
            if scale != 1.0:
                ch = ch * scale
            o_ref[rows, c * LANES:(c + 1) * LANES] = ch.astype(o_ref.dtype)


def _attend(q_ref, k2, v2, valid, sink_ref, o_ref):
    tq = q_ref.shape[0]
    nk = k2.shape[0]
    lo = lax.broadcasted_iota(jnp.int32, (nk, LANES), 1) < HEAD_DIM
    lo_q = lax.broadcasted_iota(jnp.int32, (tq, LANES), 1) < HEAD_DIM
    group = N_HEADS // N_KV_HEADS
    for m in range(KV_DIM // LANES):
        kc = k2[:, m * LANES:(m + 1) * LANES]
        vc = v2[:, m * LANES:(m + 1) * LANES]
        kr = pltpu.roll(kc, HEAD_DIM, axis=1)
        vr = pltpu.roll(vc, HEAD_DIM, axis=1)
        for hh in range(2):
            kv_head = 2 * m + hh
            k_lo, k_hi = (kc, kr) if hh == 0 else (kr, kc)
            v_lo, v_hi = (vc, vr) if hh == 0 else (vr, vc)
            kbd = jnp.concatenate([jnp.where(lo, k_lo, 0.0), jnp.where(lo, 0.0, k_hi)], axis=0).astype(BF)
            vbd = jnp.concatenate([jnp.where(lo, v_lo, 0.0), jnp.where(lo, 0.0, v_hi)], axis=0).astype(BF)
            pairs = [kv_head * (group // 2) + pp for pp in range(group // 2)]
            qs = _bf(jnp.concatenate([q_ref[:, p * LANES:(p + 1) * LANES] for p in pairs], axis=0))
            s_all = lax.dot_general(qs, kbd, (((1,), (1,)), ((), ())), preferred_element_type=F32)
            es, inv = [], []
            for pp, p in enumerate(pairs):
                s = s_all[pp * tq:(pp + 1) * tq, :]
                e_pair, d_pair = [], []
                for half in range(2):
                    sh = jnp.where(valid, s[:, half * nk:(half + 1) * nk], NEG)
                    sink = sink_ref[2 * p + half]
                    mx = jnp.maximum(jnp.max(sh, axis=-1, keepdims=True), sink)
                    e = jnp.exp(sh - mx)
                    e_pair.append(e)
                    d_pair.append(jnp.sum(e, axis=-1, keepdims=True) + jnp.exp(sink - mx))
                es.append(jnp.concatenate(e_pair, axis=1))
                inv.append(jnp.where(lo_q, 1.0 / d_pair[0], 1.0 / d_pair[1]))
            o_all = jnp.dot(_bf(jnp.concatenate(es, axis=0)), vbd, preferred_element_type=F32)
            for pp, p in enumerate(pairs):
                o = o_all[pp * tq:(pp + 1) * tq, :] * inv[pp]
                o_ref[:, p * LANES:(p + 1) * LANES] = o.astype(o_ref.dtype)


def _attn_prompt_body(sink_ref, q_ref, kvp_ref, kvc_ref, o_ref):
    n = pl.program_id(1)
    kvp = kvp_ref[...]
    kvc = kvc_ref[...]
    k2 = jnp.concatenate([kvp[:, :KV_DIM], kvc[:, :KV_DIM]], axis=0)
    v2 = jnp.concatenate([kvp[:, KV_DIM:], kvc[:, KV_DIM:]], axis=0)
    i = lax.broadcasted_iota(jnp.int32, (WINDOW, 2 * WINDOW), 0)
    j = lax.broadcasted_iota(jnp.int32, (WINDOW, 2 * WINDOW), 1)
    valid = (j >= i) & (j <= i + WINDOW) & ((j >= WINDOW) | (n > 0))
    _attend(q_ref, k2, v2, valid, sink_ref, o_ref)


def _attn_sample_body(sink_ref, q_ref, kvn_ref, ck_ref, cv_ref, o_ref, ks_ref, vs_ref):
    t = q_ref.shape[0]
    ck = ck_ref[...]
    cv = cv_ref[...]
    kvn = kvn_ref[...]
    pad = jnp.zeros((WINDOW - t, KV_DIM), F32)
    k2 = jnp.concatenate([ck, kvn[:, :KV_DIM], pad], axis=0)
    v2 = jnp.concatenate([cv, kvn[:, KV_DIM:], pad], axis=0)
    i = lax.broadcasted_iota(jnp.int32, (t, 2 * WINDOW), 0)
    j = lax.broadcasted_iota(jnp.int32, (t, 2 * WINDOW), 1)
    valid = (j >= i) & (j <= i + WINDOW)
    _attend(q_ref, k2, v2, valid, sink_ref, o_ref)
    ks_ref[0:WINDOW - t, :] = ck[t:, :]
    ks_ref[WINDOW - t:, :] = kvn[:, :KV_DIM]
    vs_ref[0:WINDOW - t, :] = cv[t:, :]
    vs_ref[WINDOW - t:, :] = kvn[:, KV_DIM:]


def _conv3_carry(u, w_ref, prev):
    ext = jnp.concatenate([prev, u], axis=0)
    x1 = pltpu.roll(ext, 1, axis=0)[SUBLANES:, :]
    x2 = pltpu.roll(ext, 2, axis=0)[SUBLANES:, :]
    return w_ref[0:1, :] * x2 + w_ref[1:2, :] * x1 + w_ref[2:3, :] * u


def _conv3_segments(u, w_ref, e, seg):
    t = lax.broadcasted_iota(jnp.int32, u.shape, 0) & (seg - 1)
    x1 = jnp.where(t >= 1, pltpu.roll(u, 1, axis=0), pltpu.roll(e, u.shape[0] - 1, axis=0))
    x2 = jnp.where(t >= 2, pltpu.roll(u, 2, axis=0), e)
    return w_ref[0:1, :] * x2 + w_ref[1:2, :] * x1 + w_ref[2:3, :] * u


def _carry_in(carry_ref, tiles_per_seq):
    i = pl.program_id(0)
    j = pl.program_id(1)

    @pl.when(i % tiles_per_seq == 0)
    def _():
        carry_ref[j] = jnp.zeros(carry_ref.shape[1:], F32)

    return carry_ref[j]


def _chunks_with_riders(h_ref, hs_ref):
    chunks = _row_chunks(h_ref.shape[0])
    out = []
    for c, rows in enumerate(chunks):
        lhs = h_ref[rows, :]
        if c == len(chunks) - 1:
            lhs = jnp.concatenate([lhs, hs_ref[...]], axis=0)
        out.append((rows, lhs, min(h_ref.shape[0], ROW_CHUNK)))
    return out


def _inconv_body(h_ref, hs_ref, wb_ref, wc_ref, wx_ref, cw_ref, e_ref, s_ref, tail_ref, ss_ref, us_ref,
                 carry_ref, wbs_ref, wcs_ref, wxs_ref, *, tiles_per_seq, seg):
    j = pl.program_id(1)
    wb = _staged(wb_ref, wbs_ref)
    wc = _staged(wc_ref, wcs_ref)
    wx = _staged(wx_ref, wxs_ref)
    prev = _carry_in(carry_ref, tiles_per_seq)
    for rows, lhs, n in _chunks_with_riders(h_ref, hs_ref):
        u_all = _dot(lhs, wc[...]) * _dot(lhs, wx[...])
        b_all = _dot(lhs, wb[...])
        u = u_all[:n, :]
        y = _conv3_carry(u, cw_ref, prev)
        prev = u[n - SUBLANES:, :]
        s_ref[rows, :] = (b_all[:n, :] * y).astype(s_ref.dtype)
        if u_all.shape[0] > n:
            us = u_all[n:, :]
            us_ref[...] = us
            ss_ref[...] = (b_all[n:, :] * _conv3_segments(us, cw_ref, e_ref[...], seg)).astype(ss_ref.dtype)
    carry_ref[j] = prev
    tail_ref[...] = prev


def _merged_body(h_ref, a_ref, s_ref, hs_ref, as_ref, ss_ref, wga_ref, wgc_ref, wab_ref, wcb_ref, o_ref, os_ref):
    chunks = _row_chunks(h_ref.shape[0])
    n = min(h_ref.shape[0], ROW_CHUNK)
    for c, rows in enumerate(chunks):
        h, a, s = h_ref[rows, :], a_ref[rows, :], s_ref[rows, :]
        last = c == len(chunks) - 1
        if last:
            h = jnp.concatenate([h, hs_ref[...]], axis=0)
            a = jnp.concatenate([a, _bf(as_ref[...])], axis=0)
            s = jnp.concatenate([s, ss_ref[...]], axis=0)
        attn_br = jax.nn.sigmoid(_dot(h, wga_ref[...])) * _dot(a, wab_ref[...])
        conv_br = jax.nn.sigmoid(_dot(h, wgc_ref[...])) * _dot(s, wcb_ref[...])
        m = attn_br + conv_br
        o_ref[rows, :] = m[:n, :].astype(o_ref.dtype)
        if last:
            os_ref[...] = m[n:, :].astype(os_ref.dtype)


def _mm_resid_body(a_ref, as_ref, w_ref, x_ref, xs_ref, g_ref, gs_ref, o_ref, os_ref):
    for rows, lhs, n in _chunks_with_riders(a_ref, as_ref):
        y = _dot(lhs, w_ref[...])
        o_ref[rows, :] = ALPHA * x_ref[rows, :] + _rows_of(g_ref, rows) * y[:n, :]
        if y.shape[0] > n:
            os_ref[...] = ALPHA * xs_ref[...] + gs_ref[...] * y[n:, :]


def _with_casts(body, n_in, n_out, n_casts):
    def wrapped(*refs):
        ins, refs_ = refs[:n_in], refs[n_in:]
        srcs, refs_ = refs_[:n_casts], refs_[n_casts:]
        outs, refs_ = refs_[:n_out], refs_[n_out:]
        dsts, scratch = refs_[:n_casts], refs_[n_casts:]
        for src, dst in zip(srcs, dsts):
            dst[...] = src[...].astype(BF)
        body(*ins, *outs, *scratch)
    return wrapped


def _layernorm(r, g, b):
    mu = jnp.mean(r, axis=-1, keepdims=True)
    d = r - mu
    var = jnp.mean(d * d, axis=-1, keepdims=True)
    return d * lax.rsqrt(var + LN_EPS) * g + b


def _ln_mod_body(r_ref, g_ref, b_ref, sc_ref, sh_ref, x_ref, h_ref):
    y = _layernorm(r_ref[...], g_ref[...], b_ref[...])
    x_ref[...] = y
    h_ref[...] = (y * (1.0 + sc_ref[...]) + sh_ref[...]).astype(h_ref.dtype)


def _ln_body(r_ref, g_ref, b_ref, x_ref):
    x_ref[...] = _layernorm(r_ref[...], g_ref[...], b_ref[...])


def _silu_mul(g, v):
    return g * jax.nn.sigmoid(g) * v


def _ffnup_body(h_ref, hs_ref, wg_ref, wv_ref, cwg_ref, cwv_ref, eg_ref, ev_ref, wd_ref,
                act_ref, tg_ref, tv_ref, acts_ref, ugs_ref, uvs_ref, wdb_ref,
                cg_ref, cv_ref, wgs_ref, wvs_ref, *, tiles_per_seq, seg):
    j = pl.program_id(1)
    wdb_ref[...] = wd_ref[...].astype(BF)
    wg = _staged(wg_ref, wgs_ref)
    wv = _staged(wv_ref, wvs_ref)
    prev_g = _carry_in(cg_ref, tiles_per_seq)
    prev_v = _carry_in(cv_ref, tiles_per_seq)
    for rows, lhs, n in _chunks_with_riders(h_ref, hs_ref):
        ug_all = _dot(lhs, wg[...])
        uv_all = _dot(lhs, wv[...])
        ug = ug_all[:n, :]
        uv = uv_all[:n, :]
        yg = _conv3_carry(ug, cwg_ref, prev_g)
        yv = _conv3_carry(uv, cwv_ref, prev_v)
        prev_g = ug[n - SUBLANES:, :]
        prev_v = uv[n - SUBLANES:, :]
        act_ref[rows, :] = _silu_mul(yg, yv).astype(act_ref.dtype)
        if ug_all.shape[0] > n:
            ugs = ug_all[n:, :]
            uvs = uv_all[n:, :]
            ugs_ref[...] = ugs
            uvs_ref[...] = uvs
            ygs = _conv3_segments(ugs, cwg_ref, eg_ref[...], seg)
            yvs = _conv3_segments(uvs, cwv_ref, ev_ref[...], seg)
            acts_ref[...] = _silu_mul(ygs, yvs).astype(acts_ref.dtype)
    cg_ref[j] = prev_g
    cv_ref[j] = prev_v
    tg_ref[...] = prev_g
    tv_ref[...] = prev_v


class _Group:
    def __init__(self, batch, seq, tm, tm_big, ada_rows):
        self.batch, self.seq, self.tm, self.tm_big = batch, seq, tm, tm_big
        self.m = batch * seq
        self.per_row = seq < tm
        self.ada = ada_rows

    def param_spec(self, chunk, tm, tn):
        cb = chunk * (D_MODEL // tn)
        if self.per_row:
            return pl.BlockSpec((tm, tn), lambda i, j: (i, cb + j))
        tps = self.seq // tm
        return pl.BlockSpec((None, 1, tn), lambda i, j: (i // tps, 0, cb + j))


def _resident_rows(tm, width):
    return pl.BlockSpec((tm, width), lambda i, j: (i, 0), pipeline_mode=pl.Buffered(1))


def _ada(c_all, w_ada, b_ada):
    tn = 512
    n = w_ada.shape[-1]
    return pl.pallas_call(
        _ada_body,
        out_shape=jax.ShapeDtypeStruct((DEPTH, ADA_ROWS, n), F32),
        grid=(DEPTH, n // tn),
        in_specs=[pl.BlockSpec((ADA_ROWS, D_MODEL), lambda l, j: (0, 0)),
                  pl.BlockSpec((None, D_MODEL, tn), lambda l, j: (l, 0, j)),
                  pl.BlockSpec((None, 1, tn), lambda l, j: (l, 0, j))],
        out_specs=pl.BlockSpec((None, ADA_ROWS, tn), lambda l, j: (l, 0, j)),
        compiler_params=_cp(2), name="ada",
    )(c_all, w_ada, b_ada.reshape(DEPTH, 1, n))


def _modulate(grp, x):
    tr = min(grp.tm, 512)
    return pl.pallas_call(
        _mod_body,
        out_shape=jax.ShapeDtypeStruct((grp.m, D_MODEL), BF),
        grid=(grp.m // tr,),
        in_specs=[pl.BlockSpec((tr, D_MODEL), lambda i: (i, 0)),
                  _row_param_spec(grp, SC1, tr), _row_param_spec(grp, SH1, tr)],
        out_specs=pl.BlockSpec((tr, D_MODEL), lambda i: (i, 0)),
        compiler_params=_cp(1), name="modulate",
    )(x, grp.ada, grp.ada)


def _row_param_spec(grp, chunk, tr):
    if grp.per_row:
        return pl.BlockSpec((tr, D_MODEL), lambda i: (i, chunk))
    tps = grp.seq // tr
    return pl.BlockSpec((None, 1, D_MODEL), lambda i: (i // tps, 0, chunk))


def _proj_rope(grp, h, w_in, l, col0, n_cols, tm, tn, cos, sin, n_rope, scale, out_dtype, name, cast_of=None):
    cb = col0 // tn
    t_tiles = cos.shape[0] // tm
    nj = n_cols // tn
    casts = [] if cast_of is None else [_row_cast(cast_of, l, (grp.m // tm) * nj, lambda i, j: i * nj + j)]
    out = pl.pallas_call(
        _with_casts(functools.partial(_proj_rope_body, n_rope=n_rope, scale=scale), 4, 1, len(casts)),
        out_shape=(jax.ShapeDtypeStruct((grp.m, n_cols), out_dtype), *[c[3] for c in casts]),
        grid=(grp.m // tm, nj),
        in_specs=[pl.BlockSpec((tm, D_MODEL), lambda i, j: (i, 0)),
                  pl.BlockSpec((None, D_MODEL, tn), lambda i, j: (l, 0, cb + j)),
                  pl.BlockSpec((tm, LANES), lambda i, j: (i % t_tiles, 0)),
                  pl.BlockSpec((tm, LANES), lambda i, j: (i % t_tiles, 0)),
                  *[c[1] for c in casts]],
        out_specs=(pl.BlockSpec((tm, tn), lambda i, j: (i, j)), *[c[2] for c in casts]),
        scratch_shapes=[pltpu.VMEM((D_MODEL, tn), BF)],
        compiler_params=_cp(2), name=name,
    )(h, w_in, cos, sin, *[c[0] for c in casts])
    return out[0] if cast_of is None else out


def _attn_prompt(sinks, q, kv, batch, seq, l, w_in, w_attn_br, w_conv_br):
    nb = seq // WINDOW
    step_of = lambda b, n: b * nb + n
    casts = [_col_cast(w_in, l, COL_GA, 2 * D_MODEL, batch * nb, step_of),
             _row_cast(w_attn_br, l, batch * nb, step_of), _row_cast(w_conv_br, l, batch * nb, step_of)]
    return pl.pallas_call(
        _with_casts(_attn_prompt_body, 4, 1, len(casts)),
        out_shape=(jax.ShapeDtypeStruct(q.shape, BF), *[c[3] for c in casts]),
        grid=(batch, nb),
        in_specs=[pl.BlockSpec(memory_space=pltpu.SMEM),
                  pl.BlockSpec((WINDOW, ATTN_DIM), lambda b, n: (b * nb + n, 0)),
                  pl.BlockSpec((WINDOW, 2 * KV_DIM), lambda b, n: (b * nb + jnp.maximum(n - 1, 0), 0)),
                  pl.BlockSpec((WINDOW, 2 * KV_DIM), lambda b, n: (b * nb + n, 0)),
                  *[c[1] for c in casts]],
        out_specs=(pl.BlockSpec((WINDOW, ATTN_DIM), lambda b, n: (b * nb + n, 0)), *[c[2] for c in casts]),
        compiler_params=_cp(2), name="attn_prompt",
    )(sinks, q, kv, kv, *[c[0] for c in casts])


def _attn_sample(sinks, q, kvn, cache_k, cache_v, l, batch, seq):
    return pl.pallas_call(
        _attn_sample_body,
        out_shape=(jax.ShapeDtypeStruct(q.shape, F32),
                   jax.ShapeDtypeStruct((batch, WINDOW, KV_DIM), F32),
                   jax.ShapeDtypeStruct((batch, WINDOW, KV_DIM), F32)),
        grid=(batch,),
        in_specs=[pl.BlockSpec(memory_space=pltpu.SMEM),
                  pl.BlockSpec((seq, ATTN_DIM), lambda b: (b, 0)),
                  pl.BlockSpec((seq, 2 * KV_DIM), lambda b: (b, 0)),
                  pl.BlockSpec((None, None, WINDOW, KV_DIM), lambda b: (l, b, 0, 0)),
                  pl.BlockSpec((None, None, WINDOW, KV_DIM), lambda b: (l, b, 0, 0))],
        out_specs=(pl.BlockSpec((seq, ATTN_DIM), lambda b: (b, 0)),
                   pl.BlockSpec((None, WINDOW, KV_DIM), lambda b: (b, 0, 0)),
                   pl.BlockSpec((None, WINDOW, KV_DIM), lambda b: (b, 0, 0))),
        compiler_params=_cp(1), name="attn_sample",
    )(sinks, q, kvn, cache_k, cache_v)


def _rider_rows(gp, gs, tm):
    rs = gs.m // (gp.m // tm)
    assert rs * (gp.m // tm) == gs.m and rs % gs.seq == 0 and rs % (2 * SUBLANES) == 0
    return rs


def _in_conv(gp, gs, h, hs, w_in, conv_w, l, past_rows):
    tm, tn = gp.tm_big, 256
    nj = CONV_DIM // tn
    rs = _rider_rows(gp, gs, tm)
    w_spec = lambda col0: pl.BlockSpec((None, D_MODEL, tn), lambda i, j: (l, 0, col0 // tn + j))
    tile = lambda rows: pl.BlockSpec((rows, tn), lambda i, j: (i, j))
    return pl.pallas_call(
        functools.partial(_inconv_body, tiles_per_seq=gp.seq // tm, seg=gs.seq),
        out_shape=(jax.ShapeDtypeStruct((gp.m, CONV_DIM), BF),
                   jax.ShapeDtypeStruct((gp.m // tm, SUBLANES, CONV_DIM), F32),
                   jax.ShapeDtypeStruct((gs.m, CONV_DIM), BF),
                   jax.ShapeDtypeStruct((gs.m, CONV_DIM), F32)),
        grid=(gp.m // tm, nj),
        in_specs=[_resident_rows(tm, D_MODEL), pl.BlockSpec((rs, D_MODEL), lambda i, j: (i, 0)),
                  w_spec(COL_CB), w_spec(COL_CC), w_spec(COL_CX),
                  pl.BlockSpec((None, 3, tn), lambda i, j: (l, 0, j)), tile(rs)],
        out_specs=(tile(tm), pl.BlockSpec((None, SUBLANES, tn), lambda i, j: (i, 0, j)), tile(rs), tile(rs)),
        scratch_shapes=[pltpu.VMEM((nj, SUBLANES, tn), F32)] + [pltpu.VMEM((D_MODEL, tn), BF)] * 3,
        compiler_params=_cp(2), name="in_conv",
    )(h, hs, w_in, w_in, w_in, conv_w, past_rows)


def _row_cast(w, l, steps, step_of):
    _, r, c = w.shape
    rows = r // steps
    assert rows * steps == r and rows % (2 * SUBLANES) == 0
    return (w, pl.BlockSpec((None, rows, c), lambda *g: (l, step_of(*g), 0)),
            pl.BlockSpec((rows, c), lambda *g: (step_of(*g), 0)), jax.ShapeDtypeStruct((r, c), BF))


def _col_cast(w, l, col0, n_cols, steps, step_of):
    _, r, _ = w.shape
    cols = n_cols // steps
    assert cols * steps == n_cols and cols % LANES == 0 and col0 % cols == 0
    return (w, pl.BlockSpec((None, r, cols), lambda *g: (l, 0, col0 // cols + step_of(*g))),
            pl.BlockSpec((r, cols), lambda *g: (0, step_of(*g))), jax.ShapeDtypeStruct((r, n_cols), BF))


def _merged(gp, gs, h, attn, s, hs, attn_s, s_s, w_gates, w_ab, w_cb):
    tm, tn = gp.tm, 256
    rs = _rider_rows(gp, gs, tm)
    row = lambda rows, width: pl.BlockSpec((rows, width), lambda i, j: (i, 0))
    w2d = lambda k, col0: pl.BlockSpec((k, tn), lambda i, j: (0, col0 // tn + j))
    tile = lambda rows: pl.BlockSpec((rows, tn), lambda i, j: (i, j))
    return pl.pallas_call(
        _merged_body,
        out_shape=(jax.ShapeDtypeStruct((gp.m, D_MODEL), BF), jax.ShapeDtypeStruct((gs.m, D_MODEL), BF)),
        grid=(gp.m // tm, D_MODEL // tn),
        in_specs=[row(tm, D_MODEL), row(tm, ATTN_DIM), row(tm, CONV_DIM),
                  row(rs, D_MODEL), row(rs, ATTN_DIM), row(rs, CONV_DIM),
                  w2d(D_MODEL, 0), w2d(D_MODEL, COL_GC - COL_GA), w2d(ATTN_DIM, 0), w2d(CONV_DIM, 0)],
        out_specs=(tile(tm), tile(rs)),
        compiler_params=_cp(2), name="merged",
    )(h, attn, s, hs, attn_s, s_s, w_gates, w_gates, w_ab, w_cb)


def _mm_resid(gp, gs, a, a_s, w_bf16, x, x_s, gate_chunk, tm, tn, name):
    k = a.shape[1]
    rs = _rider_rows(gp, gs, tm)
    tile = lambda rows: pl.BlockSpec((rows, tn), lambda i, j: (i, j))
    return pl.pallas_call(
        _mm_resid_body,
        out_shape=(jax.ShapeDtypeStruct((gp.m, D_MODEL), F32), jax.ShapeDtypeStruct((gs.m, D_MODEL), F32)),
        grid=(gp.m // tm, D_MODEL // tn),
        in_specs=[pl.BlockSpec((tm, k), lambda i, j: (i, 0)), pl.BlockSpec((rs, k), lambda i, j: (i, 0)),
                  pl.BlockSpec((k, tn), lambda i, j: (0, j)), tile(tm), tile(rs),
                  gp.param_spec(gate_chunk, tm, tn), gs.param_spec(gate_chunk, rs, tn)],
        out_specs=(tile(tm), tile(rs)),
        compiler_params=_cp(2), name=name,
    )(a, a_s, w_bf16, x, x_s, gp.ada, gs.ada)


def _ln(grp, r, g, b, l, mod_grp, mod_chunks, name):
    tr = 256
    row = pl.BlockSpec((tr, D_MODEL), lambda i: (i, 0))
    vec = pl.BlockSpec((None, 1, D_MODEL), lambda i: (l, 0, 0))
    g3 = g.reshape(DEPTH, 1, D_MODEL)
    b3 = b.reshape(DEPTH, 1, D_MODEL)
    if mod_chunks is None:
        return pl.pallas_call(
            _ln_body, out_shape=jax.ShapeDtypeStruct((grp.m, D_MODEL), F32), grid=(grp.m // tr,),
            in_specs=[row, vec, vec], out_specs=row, compiler_params=_cp(1), name=name,
        )(r, g3, b3), None
    sc, sh = mod_chunks
    return pl.pallas_call(
        _ln_mod_body,
        out_shape=(jax.ShapeDtypeStruct((grp.m, D_MODEL), F32), jax.ShapeDtypeStruct((grp.m, D_MODEL), BF)),
        grid=(grp.m // tr,),
        in_specs=[row, vec, vec, _row_param_spec(mod_grp, sc, tr), _row_param_spec(mod_grp, sh, tr)],
        out_specs=(row, row), compiler_params=_cp(1), name=name,
    )(r, g3, b3, mod_grp.ada, mod_grp.ada)


def _ffn_up(gp, gs, h, hs, w_up, ffn_conv_w, w_down, l, past_g, past_v):
    tm, tn = gp.tm_big, 256
    nj = D_FF // tn
    rs = _rider_rows(gp, gs, tm)
    steps = (gp.m // tm) * nj
    wd_rows = D_FF // steps
    assert wd_rows * steps == D_FF and wd_rows % (2 * SUBLANES) == 0
    tile = lambda rows: pl.BlockSpec((rows, tn), lambda i, j: (i, j))
    tail_shape = jax.ShapeDtypeStruct((gp.m // tm, SUBLANES, D_FF), F32)
    tail_spec = pl.BlockSpec((None, SUBLANES, tn), lambda i, j: (i, 0, j))
    full_s = jax.ShapeDtypeStruct((gs.m, D_FF), F32)
    return pl.pallas_call(
        functools.partial(_ffnup_body, tiles_per_seq=gp.seq // tm, seg=gs.seq),
        out_shape=(jax.ShapeDtypeStruct((gp.m, D_FF), BF), tail_shape, tail_shape,
                   jax.ShapeDtypeStruct((gs.m, D_FF), BF), full_s, full_s,
                   jax.ShapeDtypeStruct((D_FF, D_MODEL), BF)),
        grid=(gp.m // tm, nj),
        in_specs=[_resident_rows(tm, D_MODEL), pl.BlockSpec((rs, D_MODEL), lambda i, j: (i, 0)),
                  pl.BlockSpec((None, D_MODEL, tn), lambda i, j: (l, 0, j)),
                  pl.BlockSpec((None, D_MODEL, tn), lambda i, j: (l, 0, nj + j)),
                  pl.BlockSpec((None, 3, tn), lambda i, j: (l, 0, j)),
                  pl.BlockSpec((None, 3, tn), lambda i, j: (l, 0, nj + j)),
                  tile(rs), tile(rs),
                  pl.BlockSpec((None, wd_rows, D_MODEL), lambda i, j: (l, i * nj + j, 0))],
        out_specs=(tile(tm), tail_spec, tail_spec, tile(rs), tile(rs), tile(rs),
                   pl.BlockSpec((wd_rows, D_MODEL), lambda i, j: (i * nj + j, 0))),
        scratch_shapes=[pltpu.VMEM((nj, SUBLANES, tn), F32), pltpu.VMEM((nj, SUBLANES, tn), F32),
                        pltpu.VMEM((D_MODEL, tn), BF), pltpu.VMEM((D_MODEL, tn), BF)],
        compiler_params=_cp(2), name="ffn_up",
    )(h, hs, w_up, w_up, ffn_conv_w, ffn_conv_w, past_g, past_v, w_down)


def _rope_tables(pos):
    inv = ROPE_THETA ** (-jnp.arange(0, HEAD_DIM, 2, dtype=F32) / HEAD_DIM)
    ang = pos.astype(F32)[:, None] * inv[None, :]
    cos, sin = jnp.cos(ang), jnp.sin(ang)
    return jnp.concatenate([cos] * 4, axis=-1), jnp.concatenate([-sin, sin, -sin, sin], axis=-1)


def _expand_past(past, seq):
    batch, _, n = past.shape
    return jnp.concatenate([past, jnp.zeros((batch, seq - 2, n), past.dtype)], axis=1).reshape(batch * seq, n)


def _last_rows(x, batch, seq, n_rows):
    return x.reshape(batch, seq, x.shape[-1])[:, seq - n_rows:, :]


def _qkv(grp, l, h, rope, w_in, cast_of=None):
    cos, sin = rope
    q = _proj_rope(grp, h, w_in, l, COL_Q, ATTN_DIM, grp.tm_big, 256, cos, sin, 256 // LANES, HEAD_DIM ** -0.5,
                   F32 if grp.per_row else BF, "proj_q", cast_of)
    kv = _proj_rope(grp, h, w_in, l, COL_KV, 2 * KV_DIM, grp.tm, 2 * KV_DIM, cos, sin, KV_DIM // LANES, 1.0,
                    F32, "proj_kv")
    return q, kv


def _layer(gp, gs, next_gp, next_gs, l, xp, xs, hp, hs, rope_p, rope_s, weights, sinks, past):
    (w_in, conv_w, w_attn_br, w_conv_br, w_out, ln1_g, ln1_b, w_up, ffn_conv_w, w_down, ln2_g, ln2_b) = weights
    cache_k, cache_v, state_conv, state_ffn = past
    tps = gp.seq // gp.tm_big
    state_rows = lambda tails: tails[tps - 1::tps, SUBLANES - 2:, :]

    q_s, kv_s = _qkv(gs, l, hs, rope_s, w_in)
    attn_s, k_s, v_s = _attn_sample(sinks, q_s, kv_s, cache_k, cache_v, l, gs.batch, gs.seq)
    (q_p, w_out_b), kv_p = _qkv(gp, l, hp, rope_p, w_in, cast_of=w_out)
    attn_p, w_gates, w_ab, w_cb = _attn_prompt(sinks, q_p, kv_p, gp.batch, gp.seq, l, w_in, w_attn_br, w_conv_br)
    kv_tail = _last_rows(kv_p, gp.batch, gp.seq, WINDOW)
    k_p, v_p = kv_tail[..., :KV_DIM], kv_tail[..., KV_DIM:]

    s_p, tails, s_s, u_s = _in_conv(gp, gs, hp, hs, w_in, conv_w, l, _expand_past(state_conv[l], gs.seq))
    conv_p = state_rows(tails)
    conv_s = _last_rows(u_s, gs.batch, gs.seq, 2)

    merged_p, merged_s = _merged(gp, gs, hp, attn_p, s_p, hs, attn_s, s_s, w_gates, w_ab, w_cb)
    r1_p, r1_s = _mm_resid(gp, gs, merged_p, merged_s, w_out_b, xp, xs, G1, gp.tm, 1024, "out_proj")
    x1_s, h2_s = _ln(gs, r1_s, ln1_g, ln1_b, l, gs, (SC2, SH2), "ln1")
    x1_p, h2_p = _ln(gp, r1_p, ln1_g, ln1_b, l, gp, (SC2, SH2), "ln1")

    eg = _expand_past(state_ffn[l][..., :D_FF], gs.seq)
    ev = _expand_past(state_ffn[l][..., D_FF:], gs.seq)
    act_p, tg, tv, act_s, ug_s, uv_s, w_down_b = _ffn_up(gp, gs, h2_p, h2_s, w_up, ffn_conv_w, w_down, l, eg, ev)
    ffn_p = state_rows(jnp.concatenate([tg, tv], axis=-1))
    ffn_s = _last_rows(jnp.concatenate([ug_s, uv_s], axis=-1), gs.batch, gs.seq, 2)

    r2_p, r2_s = _mm_resid(gp, gs, act_p, act_s, w_down_b, x1_p, x1_s, G2, 512, 512, "ffn_down")
    mod = None if next_gp is None else (SC1, SH1)
    xs, hs = _ln(gs, r2_s, ln2_g, ln2_b, l, next_gs, mod, "ln2")
    xp, hp = _ln(gp, r2_p, ln2_g, ln2_b, l, next_gp, mod, "ln2")

    heads = lambda k, grp: k.reshape(grp.batch, WINDOW, N_KV_HEADS, HEAD_DIM)
    return (xp, xs, hp, hs, (heads(k_p, gp), heads(v_p, gp), conv_p, ffn_p),
            (heads(k_s, gs), heads(v_s, gs), conv_s, ffn_s))


def kernel(x_prompt, x_sample, cache_attn_k, cache_attn_v, state_conv, state_ffn_conv, c_prompt, c_sample, w_ada, b_ada, w_in, attn_sinks, conv_w, w_attn_br, w_conv_br, w_out, ln1_g, ln1_b, w_up, ffn_conv_w, w_down, ln2_g, ln2_b):
    bp, tp, _ = x_prompt.shape
    bs, ts, _ = x_sample.shape
    assert bp <= ADA_SAMPLE_ROW0 and ADA_SAMPLE_ROW0 + bs == ADA_ROWS

    c_all = jnp.concatenate([c_prompt, jnp.zeros((ADA_SAMPLE_ROW0 - bp, D_MODEL), F32), c_sample], axis=0)
    ada = _ada(c_all, w_ada, b_ada)

    weights = (w_in, conv_w, w_attn_br, w_conv_br, w_out, ln1_g, ln1_b, w_up, ffn_conv_w, w_down, ln2_g, ln2_b)
    rope_p = _rope_tables(jnp.arange(tp, dtype=jnp.int32))
    rope_s = tuple(jnp.tile(t, (bs, 1)) for t in _rope_tables(PAST_LEN + jnp.arange(ts, dtype=jnp.int32)))
    cache_k = cache_attn_k.reshape(DEPTH, bs, WINDOW, KV_DIM)
    cache_v = cache_attn_v.reshape(DEPTH, bs, WINDOW, KV_DIM)
    past = (cache_k, cache_v, state_conv, state_ffn_conv)

    xp = x_prompt.reshape(bp * tp, D_MODEL)
    xs = x_sample.reshape(bs * ts, D_MODEL)
    gps = [_Group(bp, tp, 1024, 2048, ada[l, :bp].reshape(bp, 1, -1)) for l in range(DEPTH)] + [None]
    gss = [_Group(bs, ts, bs * ts, bs * ts, jnp.repeat(ada[l, ADA_SAMPLE_ROW0:], ts, axis=0))
           for l in range(DEPTH)] + [None]
    hp = _modulate(gps[0], xp)
    hs = _modulate(gss[0], xs)
    outs_p, outs_s = [], []
    for l in range(DEPTH):
        xp, xs, hp, hs, st_p, st_s = _layer(gps[l], gss[l], gps[l + 1], gss[l + 1], l, xp, xs, hp, hs,
                                            rope_p, rope_s, weights, attn_sinks[l], past)
        outs_p.append(st_p)
        outs_s.append(st_s)
    stack = lambda outs, k: jnp.stack([o[k] for o in outs])
    return (xp.reshape(bp, tp, D_MODEL), xs.reshape(bs, ts, D_MODEL),
            stack(outs_p, 0), stack(outs_p, 1), stack(outs_p, 2), stack(outs_p, 3),
            stack(outs_s, 0), stack(outs_s, 1), stack(outs_s, 2), stack(outs_s, 3))
```

```python
import functools

import jax
import jax.numpy as jnp
from jax import lax
from jax.experimental import pallas as pl
from jax.experimental.pallas import tpu as pltpu

D_MODEL = 4096
DEPTH = 2
HEAD_DIM = 64
N_HEADS = 32
N_KV_HEADS = 4
ATTN_DIM = N_HEADS * HEAD_DIM
KV_DIM = N_KV_HEADS * HEAD_DIM
WINDOW = 128
ROPE_THETA = 10000.0
CONV_DIM = D_MODEL // 2
D_FF = 11008
PAST_LEN = 16384
LN_EPS = 1e-5
ALPHA = (2 * DEPTH) ** 0.25

COL_Q = 0
COL_KV = ATTN_DIM
COL_CB = ATTN_DIM + 2 * KV_DIM
COL_CC = COL_CB + CONV_DIM
COL_CX = COL_CC + CONV_DIM
COL_GA = COL_CX + CONV_DIM
COL_GC = COL_GA + D_MODEL

SH1, SC1, G1, SH2, SC2, G2 = range(6)

LANES = 128
SUBLANES = 8
ADA_ROWS = 40
ADA_SAMPLE_ROW0 = 8
ROW_CHUNK = 256
RIDER_CHUNK = 0
VMEM_LIMIT = 56 * 2**20
NEG = -1e30

BF = jnp.bfloat16
F32 = jnp.float32


def _cp(n_axes):
    return pltpu.CompilerParams(dimension_semantics=("arbitrary",) * n_axes, vmem_limit_bytes=VMEM_LIMIT)


def _bf(x):
    return x if x.dtype == BF else x.astype(BF)


def _dot(a, b):
    return jnp.dot(_bf(a), _bf(b), preferred_element_type=F32)


def _row_chunks(tm):
    chunk = min(tm, ROW_CHUNK)
    assert tm % chunk == 0
    return [pl.ds(r, chunk) for r in range(0, tm, chunk)]


def _staged(w_ref, stage_ref):
    stage_ref[...] = w_ref[...].astype(BF)
    return stage_ref


def _rows_of(ref, rows):
    return ref[...] if ref.shape[0] == 1 else ref[rows, :]


def _ada_body(c_ref, w_ref, b_ref, o_ref):
    c = c_ref[...]
    o_ref[...] = _dot(c * jax.nn.sigmoid(c), w_ref[...]) + b_ref[...]


def _mod_body(x_ref, sc_ref, sh_ref, o_ref):
    o_ref[...] = (x_ref[...] * (1.0 + sc_ref[...]) + sh_ref[...]).astype(o_ref.dtype)


def _proj_rope_body(h_ref, w_ref, cos_ref, sin_ref, o_ref, ws_ref, *, n_rope, scale):
    tn = o_ref.shape[1]
    w = _staged(w_ref, ws_ref)
    lane = lax.broadcasted_iota(jnp.int32, (min(h_ref.shape[0], ROW_CHUNK), LANES), 1)
    first_half = (lane & (HEAD_DIM - 1)) < HEAD_DIM // 2
    for rows in _row_chunks(h_ref.shape[0]):
        acc = _dot(h_ref[rows, :], w[...])
        cos = cos_ref[rows, :]
        sin = sin_ref[rows, :]
        for c in range(tn // LANES):
            ch = acc[:, c * LANES:(c + 1) * LANES]
            if c < n_rope:
                rot = jnp.where(first_half,
                                pltpu.roll(ch, LANES - HEAD_DIM // 2, axis=1),
                                pltpu.roll(ch, HEAD_DIM // 2, axis=1))
                ch = ch * cos + rot * sin
            if scale != 1.0:
                ch = ch * scale
            o_ref[rows, c * LANES:(c + 1) * LANES] = ch.astype(o_ref.dtype)


def _attend(q_ref, k2, v2, valid, sink_ref, o_ref):
    tq = q_ref.shape[0]
    nk = k2.shape[0]
    lo = lax.broadcasted_iota(jnp.int32, (nk, LANES), 1) < HEAD_DIM
    lo_q = lax.broadcasted_iota(jnp.int32, (tq, LANES), 1) < HEAD_DIM
    group = N_HEADS // N_KV_HEADS
    for m in range(KV_DIM // LANES):
        kc = k2[:, m * LANES:(m + 1) * LANES]
        vc = v2[:, m * LANES:(m + 1) * LANES]
        kr = pltpu.roll(kc, HEAD_DIM, axis=1)
        vr = pltpu.roll(vc, HEAD_DIM, axis=1)
        for hh in range(2):
            kv_head = 2 * m + hh
            k_lo, k_hi = (kc, kr) if hh == 0 else (kr, kc)
            v_lo, v_hi = (vc, vr) if hh == 0 else (vr, vc)
            kbd = jnp.concatenate([jnp.where(lo, k_lo, 0.0), jnp.where(lo, 0.0, k_hi)], axis=0).astype(BF)
            vbd = jnp.concatenate([jnp.where(lo, v_lo, 0.0), jnp.where(lo, 0.0, v_hi)], axis=0).astype(BF)
            pairs = [kv_head * (group // 2) + pp for pp in range(group // 2)]
            qs = _bf(jnp.concatenate([q_ref[:, p * LANES:(p + 1) * LANES] for p in pairs], axis=0))
            s_all = lax.dot_general(qs, kbd, (((1,), (1,)), ((), ())), preferred_element_type=F32)
            es, inv = [], []
            for pp, p in enumerate(pairs):
                s = s_all[pp * tq:(pp + 1) * tq, :]
                e_pair, d_pair = [], []
                for half in range(2):
                    sh = jnp.where(valid, s[:, half * nk:(half + 1) * nk], NEG)
                    sink = sink_ref[2 * p + half]
                    mx = jnp.maximum(jnp.max(sh, axis=-1, keepdims=True), sink)
                    e = jnp.exp(sh - mx)
                    e_pair.append(e)
                    d_pair.append(jnp.sum(e, axis=-1, keepdims=True) + jnp.exp(sink - mx))
                es.append(jnp.concatenate(e_pair, axis=1))
                inv.append(jnp.where(lo_q, 1.0 / d_pair[0], 1.0 / d_pair[1]))
            o_all = jnp.dot(_bf(jnp.concatenate(es, axis=0)), vbd, preferred_element_type=F32)
            for pp, p in enumerate(pairs):
                o = o_all[pp * tq:(pp + 1) * tq, :] * inv[pp]
                o_ref[:, p * LANES:(p + 1) * LANES] = o.astype(o_ref.dtype)


def _attn_prompt_body(sink_ref, q_ref, kvp_ref, kvc_ref, o_ref):
    n = pl.program_id(1)
    kvp = kvp_ref[...]
    kvc = kvc_ref[...]
    k2 = jnp.concatenate([kvp[:, :KV_DIM], kvc[:, :KV_DIM]], axis=0)
    v2 = jnp.concatenate([kvp[:, KV_DIM:], kvc[:, KV_DIM:]], axis=0)
    i = lax.broadcasted_iota(jnp.int32, (WINDOW, 2 * WINDOW), 0)
    j = lax.broadcasted_iota(jnp.int32, (WINDOW, 2 * WINDOW), 1)
    valid = (j >= i) & (j <= i + WINDOW) & ((j >= WINDOW) | (n > 0))
    _attend(q_ref, k2, v2, valid, sink_ref, o_ref)


def _attn_sample_body(sink_ref, q_ref, kvn_ref, ck_ref, cv_ref, o_ref, ks_ref, vs_ref):
    t = q_ref.shape[0]
    ck = ck_ref[...]
    cv = cv_ref[...]
    kvn = kvn_ref[...]
    pad = jnp.zeros((WINDOW - t, KV_DIM), F32)
    k2 = jnp.concatenate([ck, kvn[:, :KV_DIM], pad], axis=0)
    v2 = jnp.concatenate([cv, kvn[:, KV_DIM:], pad], axis=0)
    i = lax.broadcasted_iota(jnp.int32, (t, 2 * WINDOW), 0)
    j = lax.broadcasted_iota(jnp.int32, (t, 2 * WINDOW), 1)
    valid = (j >= i) & (j <= i + WINDOW)
    _attend(q_ref, k2, v2, valid, sink_ref, o_ref)
    ks_ref[0:WINDOW - t, :] = ck[t:, :]
    ks_ref[WINDOW - t:, :] = kvn[:, :KV_DIM]
    vs_ref[0:WINDOW - t, :] = cv[t:, :]
    vs_ref[WINDOW - t:, :] = kvn[:, KV_DIM:]


def _conv3_carry(u, w_ref, prev):
    ext = jnp.concatenate([prev, u], axis=0)
    x1 = pltpu.roll(ext, 1, axis=0)[SUBLANES:, :]
    x2 = pltpu.roll(ext, 2, axis=0)[SUBLANES:, :]
    return w_ref[0:1, :] * x2 + w_ref[1:2, :] * x1 + w_ref[2:3, :] * u


def _conv3_segments(u, w_ref, e, seg):
    t = lax.broadcasted_iota(jnp.int32, u.shape, 0) & (seg - 1)
    x1 = jnp.where(t >= 1, pltpu.roll(u, 1, axis=0), pltpu.roll(e, u.shape[0] - 1, axis=0))
    x2 = jnp.where(t >= 2, pltpu.roll(u, 2, axis=0), e)
    return w_ref[0:1, :] * x2 + w_ref[1:2, :] * x1 + w_ref[2:3, :] * u


def _carry_in(carry_ref, tiles_per_seq):
    i = pl.program_id(0)
    j = pl.program_id(1)

    @pl.when(i % tiles_per_seq == 0)
    def _():
        carry_ref[j] = jnp.zeros(carry_ref.shape[1:], F32)

    return carry_ref[j]


def _chunks_with_riders(h_ref, hs_ref):
    chunks = _row_chunks(h_ref.shape[0])
    out = []
    for c, rows in enumerate(chunks):
        lhs = h_ref[rows, :]
        if c == RIDER_CHUNK:
            lhs = jnp.concatenate([lhs, hs_ref[...]], axis=0)
        out.append((rows, lhs, min(h_ref.shape[0], ROW_CHUNK)))
    return out


def _inconv_body(h_ref, hs_ref, wb_ref, wc_ref, wx_ref, cw_ref, e_ref, s_ref, tail_ref, ss_ref, us_ref,
                 carry_ref, wbs_ref, wcs_ref, wxs_ref, *, tiles_per_seq, seg):
    j = pl.program_id(1)
    wb = _staged(wb_ref, wbs_ref)
    wc = _staged(wc_ref, wcs_ref)
    wx = _staged(wx_ref, wxs_ref)
    prev = _carry_in(carry_ref, tiles_per_seq)
    for rows, lhs, n in _chunks_with_riders(h_ref, hs_ref):
        u_all = _dot(lhs, wc[...]) * _dot(lhs, wx[...])
        b_all = _dot(lhs, wb[...])
        u = u_all[:n, :]
        y = _conv3_carry(u, cw_ref, prev)
        prev = u[n - SUBLANES:, :]
        s_ref[rows, :] = (b_all[:n, :] * y).astype(s_ref.dtype)
        if u_all.shape[0] > n:
            us = u_all[n:, :]
            us_ref[...] = us
            ss_ref[...] = (b_all[n:, :] * _conv3_segments(us, cw_ref, e_ref[...], seg)).astype(ss_ref.dtype)
    carry_ref[j] = prev
    tail_ref[...] = prev


def _merged_body(h_ref, a_ref, s_ref, hs_ref, as_ref, ss_ref, wga_ref, wgc_ref, wab_ref, wcb_ref, o_ref, os_ref):
    chunks = _row_chunks(h_ref.shape[0])
    n = min(h_ref.shape[0], ROW_CHUNK)
    for c, rows in enumerate(chunks):
        h, a, s = h_ref[rows, :], a_ref[rows, :], s_ref[rows, :]
        last = c == RIDER_CHUNK
        if last:
            h = jnp.concatenate([h, hs_ref[...]], axis=0)
            a = jnp.concatenate([a, _bf(as_ref[...])], axis=0)
            s = jnp.concatenate([s, ss_ref[...]], axis=0)
        attn_br = jax.nn.sigmoid(_dot(h, wga_ref[...])) * _dot(a, wab_ref[...])
        conv_br = jax.nn.sigmoid(_dot(h, wgc_ref[...])) * _dot(s, wcb_ref[...])
        m = attn_br + conv_br
        o_ref[rows, :] = m[:n, :].astype(o_ref.dtype)
        if last:
            os_ref[...] = m[n:, :].astype(os_ref.dtype)


def _mm_resid_body(a_ref, as_ref, w_ref, x_ref, xs_ref, g_ref, gs_ref, o_ref, os_ref):
    for rows, lhs, n in _chunks_with_riders(a_ref, as_ref):
        y = _dot(lhs, w_ref[...])
        o_ref[rows, :] = ALPHA * x_ref[rows, :] + _rows_of(g_ref, rows) * y[:n, :]
        if y.shape[0] > n:
            os_ref[...] = ALPHA * xs_ref[...] + gs_ref[...] * y[n:, :]


def _with_casts(body, n_in, n_out, n_casts):
    def wrapped(*refs):
        ins, refs_ = refs[:n_in], refs[n_in:]
        srcs, refs_ = refs_[:n_casts], refs_[n_casts:]
        outs, refs_ = refs_[:n_out], refs_[n_out:]
        dsts, scratch = refs_[:n_casts], refs_[n_casts:]
        for src, dst in zip(srcs, dsts):
            dst[...] = src[...].astype(BF)
        body(*ins, *outs, *scratch)
    return wrapped


def _layernorm(r, g, b):
    mu = jnp.mean(r, axis=-1, keepdims=True)
    d = r - mu
    var = jnp.mean(d * d, axis=-1, keepdims=True)
    return d * lax.rsqrt(var + LN_EPS) * g + b


def _ln_mod_body(r_ref, g_ref, b_ref, sc_ref, sh_ref, x_ref, h_ref):
    y = _layernorm(r_ref[...], g_ref[...], b_ref[...])
    x_ref[...] = y
    h_ref[...] = (y * (1.0 + sc_ref[...]) + sh_ref[...]).astype(h_ref.dtype)


def _ln_body(r_ref, g_ref, b_ref, x_ref):
    x_ref[...] = _layernorm(r_ref[...], g_ref[...], b_ref[...])


def _silu_mul(g, v):
    return g * jax.nn.sigmoid(g) * v


def _ffnup_body(h_ref, hs_ref, wg_ref, wv_ref, cwg_ref, cwv_ref, eg_ref, ev_ref, wd_ref,
                act_ref, tg_ref, tv_ref, acts_ref, ugs_ref, uvs_ref, wdb_ref,
                cg_ref, cv_ref, wgs_ref, wvs_ref, *, tiles_per_seq, seg):
    j = pl.program_id(1)
    wdb_ref[...] = wd_ref[...].astype(BF)
    wg = _staged(wg_ref, wgs_ref)
    wv = _staged(wv_ref, wvs_ref)
    prev_g = _carry_in(cg_ref, tiles_per_seq)
    prev_v = _carry_in(cv_ref, tiles_per_seq)
    for rows, lhs, n in _chunks_with_riders(h_ref, hs_ref):
        ug_all = _dot(lhs, wg[...])
        uv_all = _dot(lhs, wv[...])
        ug = ug_all[:n, :]
        uv = uv_all[:n, :]
        yg = _conv3_carry(ug, cwg_ref, prev_g)
        yv = _conv3_carry(uv, cwv_ref, prev_v)
        prev_g = ug[n - SUBLANES:, :]
        prev_v = uv[n - SUBLANES:, :]
        act_ref[rows, :] = _silu_mul(yg, yv).astype(act_ref.dtype)
        if ug_all.shape[0] > n:
            ugs = ug_all[n:, :]
            uvs = uv_all[n:, :]
            ugs_ref[...] = ugs
            uvs_ref[...] = uvs
            ygs = _conv3_segments(ugs, cwg_ref, eg_ref[...], seg)
            yvs = _conv3_segments(uvs, cwv_ref, ev_ref[...], seg)
            acts_ref[...] = _silu_mul(ygs, yvs).astype(acts_ref.dtype)
    cg_ref[j] = prev_g
    cv_ref[j] = prev_v
    tg_ref[...] = prev_g
    tv_ref[...] = prev_v


class _Group:
    def __init__(self, batch, seq, tm, tm_big, ada_rows):
        self.batch, self.seq, self.tm, self.tm_big = batch, seq, tm, tm_big
        self.m = batch * seq
        self.per_row = seq < tm
        self.ada = ada_rows

    def param_spec(self, chunk, tm, tn):
        cb = chunk * (D_MODEL // tn)
        if self.per_row:
            return pl.BlockSpec((tm, tn), lambda i, j: (i, cb + j))
        tps = self.seq // tm
        return pl.BlockSpec((None, 1, tn), lambda i, j: (i // tps, 0, cb + j))


def _resident_rows(tm, width):
    return pl.BlockSpec((tm, width), lambda i, j: (i, 0), pipeline_mode=pl.Buffered(1))


def _ada(c_all, w_ada, b_ada):
    tn = 512
    n = w_ada.shape[-1]
    return pl.pallas_call(
        _ada_body,
        out_shape=jax.ShapeDtypeStruct((DEPTH, ADA_ROWS, n), F32),
        grid=(DEPTH, n // tn),
        in_specs=[pl.BlockSpec((ADA_ROWS, D_MODEL), lambda l, j: (0, 0)),
                  pl.BlockSpec((None, D_MODEL, tn), lambda l, j: (l, 0, j)),
                  pl.BlockSpec((None, 1, tn), lambda l, j: (l, 0, j))],
        out_specs=pl.BlockSpec((None, ADA_ROWS, tn), lambda l, j: (l, 0, j)),
        compiler_params=_cp(2), name="ada",
    )(c_all, w_ada, b_ada.reshape(DEPTH, 1, n))


def _modulate(grp, x):
    tr = min(grp.tm, 512)
    return pl.pallas_call(
        _mod_body,
        out_shape=jax.ShapeDtypeStruct((grp.m, D_MODEL), BF),
        grid=(grp.m // tr,),
        in_specs=[pl.BlockSpec((tr, D_MODEL), lambda i: (i, 0)),
                  _row_param_spec(grp, SC1, tr), _row_param_spec(grp, SH1, tr)],
        out_specs=pl.BlockSpec((tr, D_MODEL), lambda i: (i, 0)),
        compiler_params=_cp(1), name="modulate",
    )(x, grp.ada, grp.ada)


def _row_param_spec(grp, chunk, tr):
    if grp.per_row:
        return pl.BlockSpec((tr, D_MODEL), lambda i: (i, chunk))
    tps = grp.seq // tr
    return pl.BlockSpec((None, 1, D_MODEL), lambda i: (i // tps, 0, chunk))


def _proj_rope(grp, h, w_in, l, col0, n_cols, tm, tn, cos, sin, n_rope, scale, out_dtype, name, cast_of=None):
    cb = col0 // tn
    t_tiles = cos.shape[0] // tm
    nj = n_cols // tn
    casts = [] if cast_of is None else [_row_cast(cast_of, l, (grp.m // tm) * nj, lambda i, j: i * nj + j)]
    out = pl.pallas_call(
        _with_casts(functools.partial(_proj_rope_body, n_rope=n_rope, scale=scale), 4, 1, len(casts)),
        out_shape=(jax.ShapeDtypeStruct((grp.m, n_cols), out_dtype), *[c[3] for c in casts]),
        grid=(grp.m // tm, nj),
        in_specs=[pl.BlockSpec((tm, D_MODEL), lambda i, j: (i, 0)),
                  pl.BlockSpec((None, D_MODEL, tn), lambda i, j: (l, 0, cb + j)),
                  pl.BlockSpec((tm, LANES), lambda i, j: (i % t_tiles, 0)),
                  pl.BlockSpec((tm, LANES), lambda i, j: (i % t_tiles, 0)),
                  *[c[1] for c in casts]],
        out_specs=(pl.BlockSpec((tm, tn), lambda i, j: (i, j)), *[c[2] for c in casts]),
        scratch_shapes=[pltpu.VMEM((D_MODEL, tn), BF)],
        compiler_params=_cp(2), name=name,
    )(h, w_in, cos, sin, *[c[0] for c in casts])
    return out[0] if cast_of is None else out


def _attn_prompt(sinks, q, kv, batch, seq, l, w_in, w_attn_br, w_conv_br):
    nb = seq // WINDOW
    step_of = lambda b, n: b * nb + n
    casts = [_col_cast(w_in, l, COL_GA, 2 * D_MODEL, batch * nb, step_of),
             _row_cast(w_attn_br, l, batch * nb, step_of), _row_cast(w_conv_br, l, batch * nb, step_of)]
    return pl.pallas_call(
        _with_casts(_attn_prompt_body, 4, 1, len(casts)),
        out_shape=(jax.ShapeDtypeStruct(q.shape, BF), *[c[3] for c in casts]),
        grid=(batch, nb),
        in_specs=[pl.BlockSpec(memory_space=pltpu.SMEM),
                  pl.BlockSpec((WINDOW, ATTN_DIM), lambda b, n: (b * nb + n, 0)),
                  pl.BlockSpec((WINDOW, 2 * KV_DIM), lambda b, n: (b * nb + jnp.maximum(n - 1, 0), 0)),
                  pl.BlockSpec((WINDOW, 2 * KV_DIM), lambda b, n: (b * nb + n, 0)),
                  *[c[1] for c in casts]],
        out_specs=(pl.BlockSpec((WINDOW, ATTN_DIM), lambda b, n: (b * nb + n, 0)), *[c[2] for c in casts]),
        compiler_params=_cp(2), name="attn_prompt",
    )(sinks, q, kv, kv, *[c[0] for c in casts])


def _attn_sample(sinks, q, kvn, cache_k, cache_v, l, batch, seq):
    return pl.pallas_call(
        _attn_sample_body,
        out_shape=(jax.ShapeDtypeStruct(q.shape, F32),
                   jax.ShapeDtypeStruct((batch, WINDOW, KV_DIM), F32),
                   jax.ShapeDtypeStruct((batch, WINDOW, KV_DIM), F32)),
        grid=(batch,),
        in_specs=[pl.BlockSpec(memory_space=pltpu.SMEM),
                  pl.BlockSpec((seq, ATTN_DIM), lambda b: (b, 0)),
                  pl.BlockSpec((seq, 2 * KV_DIM), lambda b: (b, 0)),
                  pl.BlockSpec((None, None, WINDOW, KV_DIM), lambda b: (l, b, 0, 0)),
                  pl.BlockSpec((None, None, WINDOW, KV_DIM), lambda b: (l, b, 0, 0))],
        out_specs=(pl.BlockSpec((seq, ATTN_DIM), lambda b: (b, 0)),
                   pl.BlockSpec((None, WINDOW, KV_DIM), lambda b: (b, 0, 0)),
                   pl.BlockSpec((None, WINDOW, KV_DIM), lambda b: (b, 0, 0))),
        compiler_params=_cp(1), name="attn_sample",
    )(sinks, q, kvn, cache_k, cache_v)


def _rider_rows(gp, gs, tm):
    rs = gs.m // (gp.m // tm)
    assert rs * (gp.m // tm) == gs.m and rs % gs.seq == 0 and rs % (2 * SUBLANES) == 0
    return rs


def _in_conv(gp, gs, h, hs, w_in, conv_w, l, past_rows):
    tm, tn = gp.tm_big, 256
    nj = CONV_DIM // tn
    rs = _rider_rows(gp, gs, tm)
    w_spec = lambda col0: pl.BlockSpec((None, D_MODEL, tn), lambda i, j: (l, 0, col0 // tn + j))
    tile = lambda rows: pl.BlockSpec((rows, tn), lambda i, j: (i, j))
    return pl.pallas_call(
        functools.partial(_inconv_body, tiles_per_seq=gp.seq // tm, seg=gs.seq),
        out_shape=(jax.ShapeDtypeStruct((gp.m, CONV_DIM), BF),
                   jax.ShapeDtypeStruct((gp.m // tm, SUBLANES, CONV_DIM), F32),
                   jax.ShapeDtypeStruct((gs.m, CONV_DIM), BF),
                   jax.ShapeDtypeStruct((gs.m, CONV_DIM), F32)),
        grid=(gp.m // tm, nj),
        in_specs=[_resident_rows(tm, D_MODEL), pl.BlockSpec((rs, D_MODEL), lambda i, j: (i, 0)),
                  w_spec(COL_CB), w_spec(COL_CC), w_spec(COL_CX),
                  pl.BlockSpec((None, 3, tn), lambda i, j: (l, 0, j)), tile(rs)],
        out_specs=(tile(tm), pl.BlockSpec((None, SUBLANES, tn), lambda i, j: (i, 0, j)), tile(rs), tile(rs)),
        scratch_shapes=[pltpu.VMEM((nj, SUBLANES, tn), F32)] + [pltpu.VMEM((D_MODEL, tn), BF)] * 3,
        compiler_params=_cp(2), name="in_conv",
    )(h, hs, w_in, w_in, w_in, conv_w, past_rows)


def _row_cast(w, l, steps, step_of):
    _, r, c = w.shape
    rows = r // steps
    assert rows * steps == r and rows % (2 * SUBLANES) == 0
    return (w, pl.BlockSpec((None, rows, c), lambda *g: (l, step_of(*g), 0)),
            pl.BlockSpec((rows, c), lambda *g: (step_of(*g), 0)), jax.ShapeDtypeStruct((r, c), BF))


def _col_cast(w, l, col0, n_cols, steps, step_of):
    _, r, _ = w.shape
    cols = n_cols // steps
    assert cols * steps == n_cols and cols % LANES == 0 and col0 % cols == 0
    return (w, pl.BlockSpec((None, r, cols), lambda *g: (l, 0, col0 // cols + step_of(*g))),
            pl.BlockSpec((r, cols), lambda *g: (0, step_of(*g))), jax.ShapeDtypeStruct((r, n_cols), BF))


def _merged(gp, gs, h, attn, s, hs, attn_s, s_s, w_gates, w_ab, w_cb):
    tm, tn = gp.tm, 256
    rs = _rider_rows(gp, gs, tm)
    row = lambda rows, width: pl.BlockSpec((rows, width), lambda i, j: (i, 0))
    w2d = lambda k, col0: pl.BlockSpec((k, tn), lambda i, j: (0, col0 // tn + j))
    tile = lambda rows: pl.BlockSpec((rows, tn), lambda i, j: (i, j))
    return pl.pallas_call(
        _merged_body,
        out_shape=(jax.ShapeDtypeStruct((gp.m, D_MODEL), BF), jax.ShapeDtypeStruct((gs.m, D_MODEL), BF)),
        grid=(gp.m // tm, D_MODEL // tn),
        in_specs=[row(tm, D_MODEL), row(tm, ATTN_DIM), row(tm, CONV_DIM),
                  row(rs, D_MODEL), row(rs, ATTN_DIM), row(rs, CONV_DIM),
                  w2d(D_MODEL, 0), w2d(D_MODEL, COL_GC - COL_GA), w2d(ATTN_DIM, 0), w2d(CONV_DIM, 0)],
        out_specs=(tile(tm), tile(rs)),
        compiler_params=_cp(2), name="merged",
    )(h, attn, s, hs, attn_s, s_s, w_gates, w_gates, w_ab, w_cb)


def _mm_resid(gp, gs, a, a_s, w_bf16, x, x_s, gate_chunk, tm, tn, name):
    k = a.shape[1]
    rs = _rider_rows(gp, gs, tm)
    tile = lambda rows: pl.BlockSpec((rows, tn), lambda i, j: (i, j))
    return pl.pallas_call(
        _mm_resid_body,
        out_shape=(jax.ShapeDtypeStruct((gp.m, D_MODEL), F32), jax.ShapeDtypeStruct((gs.m, D_MODEL), F32)),
        grid=(gp.m // tm, D_MODEL // tn),
        in_specs=[pl.BlockSpec((tm, k), lambda i, j: (i, 0)), pl.BlockSpec((rs, k), lambda i, j: (i, 0)),
                  pl.BlockSpec((k, tn), lambda i, j: (0, j)), tile(tm), tile(rs),
                  gp.param_spec(gate_chunk, tm, tn), gs.param_spec(gate_chunk, rs, tn)],
        out_specs=(tile(tm), tile(rs)),
        compiler_params=_cp(2), name=name,
    )(a, a_s, w_bf16, x, x_s, gp.ada, gs.ada)


def _ln(grp, r, g, b, l, mod_grp, mod_chunks, name):
    tr = min(grp.m, 512)
    row = pl.BlockSpec((tr, D_MODEL), lambda i: (i, 0))
    vec = pl.BlockSpec((None, 1, D_MODEL), lambda i: (l, 0, 0))
    g3 = g.reshape(DEPTH, 1, D_MODEL)
    b3 = b.reshape(DEPTH, 1, D_MODEL)
    if mod_chunks is None:
        return pl.pallas_call(
            _ln_body, out_shape=jax.ShapeDtypeStruct((grp.m, D_MODEL), F32), grid=(grp.m // tr,),
            in_specs=[row, vec, vec], out_specs=row, compiler_params=_cp(1), name=name,
        )(r, g3, b3), None
    sc, sh = mod_chunks
    return pl.pallas_call(
        _ln_mod_body,
        out_shape=(jax.ShapeDtypeStruct((grp.m, D_MODEL), F32), jax.ShapeDtypeStruct((grp.m, D_MODEL), BF)),
        grid=(grp.m // tr,),
        in_specs=[row, vec, vec, _row_param_spec(mod_grp, sc, tr), _row_param_spec(mod_grp, sh, tr)],
        out_specs=(row, row), compiler_params=_cp(1), name=name,
    )(r, g3, b3, mod_grp.ada, mod_grp.ada)


def _ffn_up(gp, gs, h, hs, w_up, ffn_conv_w, w_down, l, past_g, past_v):
    tm, tn = gp.tm_big, 256
    nj = D_FF // tn
    rs = _rider_rows(gp, gs, tm)
    steps = (gp.m // tm) * nj
    wd_rows = D_FF // steps
    assert wd_rows * steps == D_FF and wd_rows % (2 * SUBLANES) == 0
    tile = lambda rows: pl.BlockSpec((rows, tn), lambda i, j: (i, j))
    tail_shape = jax.ShapeDtypeStruct((gp.m // tm, SUBLANES, D_FF), F32)
    tail_spec = pl.BlockSpec((None, SUBLANES, tn), lambda i, j: (i, 0, j))
    full_s = jax.ShapeDtypeStruct((gs.m, D_FF), F32)
    return pl.pallas_call(
        functools.partial(_ffnup_body, tiles_per_seq=gp.seq // tm, seg=gs.seq),
        out_shape=(jax.ShapeDtypeStruct((gp.m, D_FF), BF), tail_shape, tail_shape,
                   jax.ShapeDtypeStruct((gs.m, D_FF), BF), full_s, full_s,
                   jax.ShapeDtypeStruct((D_FF, D_MODEL), BF)),
        grid=(gp.m // tm, nj),
        in_specs=[_resident_rows(tm, D_MODEL), pl.BlockSpec((rs, D_MODEL), lambda i, j: (i, 0)),
                  pl.BlockSpec((None, D_MODEL, tn), lambda i, j: (l, 0, j)),
                  pl.BlockSpec((None, D_MODEL, tn), lambda i, j: (l, 0, nj + j)),
                  pl.BlockSpec((None, 3, tn), lambda i, j: (l, 0, j)),
                  pl.BlockSpec((None, 3, tn), lambda i, j: (l, 0, nj + j)),
                  tile(rs), tile(rs),
                  pl.BlockSpec((None, wd_rows, D_MODEL), lambda i, j: (l, i * nj + j, 0))],
        out_specs=(tile(tm), tail_spec, tail_spec, tile(rs), tile(rs), tile(rs),
                   pl.BlockSpec((wd_rows, D_MODEL), lambda i, j: (i * nj + j, 0))),
        scratch_shapes=[pltpu.VMEM((nj, SUBLANES, tn), F32), pltpu.VMEM((nj, SUBLANES, tn), F32),
                        pltpu.VMEM((D_MODEL, tn), BF), pltpu.VMEM((D_MODEL, tn), BF)],
        compiler_params=_cp(2), name="ffn_up",
    )(h, hs, w_up, w_up, ffn_conv_w, ffn_conv_w, past_g, past_v, w_down)


def _rope_tables(pos):
    inv = ROPE_THETA ** (-jnp.arange(0, HEAD_DIM, 2, dtype=F32) / HEAD_DIM)
    ang = pos.astype(F32)[:, None] * inv[None, :]
    cos, sin = jnp.cos(ang), jnp.sin(ang)
    return jnp.concatenate([cos] * 4, axis=-1), jnp.concatenate([-sin, sin, -sin, sin], axis=-1)


def _expand_past(past, seq):
    batch, _, n = past.shape
    return jnp.concatenate([past, jnp.zeros((batch, seq - 2, n), past.dtype)], axis=1).reshape(batch * seq, n)


def _last_rows(x, batch, seq, n_rows):
    return x.reshape(batch, seq, x.shape[-1])[:, seq - n_rows:, :]


def _qkv(grp, l, h, rope, w_in, cast_of=None):
    cos, sin = rope
    q = _proj_rope(grp, h, w_in, l, COL_Q, ATTN_DIM, grp.tm_big, 256, cos, sin, 256 // LANES, HEAD_DIM ** -0.5,
                   F32 if grp.per_row else BF, "proj_q", cast_of)
    kv = _proj_rope(grp, h, w_in, l, COL_KV, 2 * KV_DIM, grp.tm, 2 * KV_DIM, cos, sin, KV_DIM // LANES, 1.0,
                    F32, "proj_kv")
    return q, kv


def _layer(gp, gs, next_gp, next_gs, l, xp, xs, hp, hs, rope_p, rope_s, weights, sinks, past):
    (w_in, conv_w, w_attn_br, w_conv_br, w_out, ln1_g, ln1_b, w_up, ffn_conv_w, w_down, ln2_g, ln2_b) = weights
    cache_k, cache_v, state_conv, state_ffn = past
    tps = gp.seq // gp.tm_big
    state_rows = lambda tails: tails[tps - 1::tps, SUBLANES - 2:, :]

    q_s, kv_s = _qkv(gs, l, hs, rope_s, w_in)
    attn_s, k_s, v_s = _attn_sample(sinks, q_s, kv_s, cache_k, cache_v, l, gs.batch, gs.seq)
    (q_p, w_out_b), kv_p = _qkv(gp, l, hp, rope_p, w_in, cast_of=w_out)
    attn_p, w_gates, w_ab, w_cb = _attn_prompt(sinks, q_p, kv_p, gp.batch, gp.seq, l, w_in, w_attn_br, w_conv_br)
    kv_tail = _last_rows(kv_p, gp.batch, gp.seq, WINDOW)
    k_p, v_p = kv_tail[..., :KV_DIM], kv_tail[..., KV_DIM:]

    s_p, tails, s_s, u_s = _in_conv(gp, gs, hp, hs, w_in, conv_w, l, _expand_past(state_conv[l], gs.seq))
    conv_p = state_rows(tails)
    conv_s = _last_rows(u_s, gs.batch, gs.seq, 2)

    merged_p, merged_s = _merged(gp, gs, hp, attn_p, s_p, hs, attn_s, s_s, w_gates, w_ab, w_cb)
    r1_p, r1_s = _mm_resid(gp, gs, merged_p, merged_s, w_out_b, xp, xs, G1, gp.tm, 1024, "out_proj")
    x1_s, h2_s = _ln(gs, r1_s, ln1_g, ln1_b, l, gs, (SC2, SH2), "ln1")
    x1_p, h2_p = _ln(gp, r1_p, ln1_g, ln1_b, l, gp, (SC2, SH2), "ln1")

    eg = _expand_past(state_ffn[l][..., :D_FF], gs.seq)
    ev = _expand_past(state_ffn[l][..., D_FF:], gs.seq)
    act_p, tg, tv, act_s, ug_s, uv_s, w_down_b = _ffn_up(gp, gs, h2_p, h2_s, w_up, ffn_conv_w, w_down, l, eg, ev)
    ffn_p = state_rows(jnp.concatenate([tg, tv], axis=-1))
    ffn_s = _last_rows(jnp.concatenate([ug_s, uv_s], axis=-1), gs.batch, gs.seq, 2)

    r2_p, r2_s = _mm_resid(gp, gs, act_p, act_s, w_down_b, x1_p, x1_s, G2, 512, 512, "ffn_down")
    mod = None if next_gp is None else (SC1, SH1)
    xs, hs = _ln(gs, r2_s, ln2_g, ln2_b, l, next_gs, mod, "ln2")
    xp, hp = _ln(gp, r2_p, ln2_g, ln2_b, l, next_gp, mod, "ln2")

    heads = lambda k, grp: k.reshape(grp.batch, WINDOW, N_KV_HEADS, HEAD_DIM)
    return (xp, xs, hp, hs, (heads(k_p, gp), heads(v_p, gp), conv_p, ffn_p),
            (heads(k_s, gs), heads(v_s, gs), conv_s, ffn_s))


def kernel(x_prompt, x_sample, cache_attn_k, cache_attn_v, state_conv, state_ffn_conv, c_prompt, c_sample, w_ada, b_ada, w_in, attn_sinks, conv_w, w_attn_br, w_conv_br, w_out, ln1_g, ln1_b, w_up, ffn_conv_w, w_down, ln2_g, ln2_b):
    bp, tp, _ = x_prompt.shape
    bs, ts, _ = x_sample.shape
    assert bp <= ADA_SAMPLE_ROW0 and ADA_SAMPLE_ROW0 + bs == ADA_ROWS

    c_all = jnp.concatenate([c_prompt, jnp.zeros((ADA_SAMPLE_ROW0 - bp, D_MODEL), F32), c_sample], axis=0)
    ada = _ada(c_all, w_ada, b_ada)

    weights = (w_in, conv_w, w_attn_br, w_conv_br, w_out, ln1_g, ln1_b, w_up, ffn_conv_w, w_down, ln2_g, ln2_b)
    rope_p = _rope_tables(jnp.arange(tp, dtype=jnp.int32))
    rope_s = tuple(jnp.tile(t, (bs, 1)) for t in _rope_tables(PAST_LEN + jnp.arange(ts, dtype=jnp.int32)))
    cache_k = cache_attn_k.reshape(DEPTH, bs, WINDOW, KV_DIM)
    cache_v = cache_attn_v.reshape(DEPTH, bs, WINDOW, KV_DIM)
    past = (cache_k, cache_v, state_conv, state_ffn_conv)

    xp = x_prompt.reshape(bp * tp, D_MODEL)
    xs = x_sample.reshape(bs * ts, D_MODEL)
    gps = [_Group(bp, tp, 1024, 2048, ada[l, :bp].reshape(bp, 1, -1)) for l in range(DEPTH)] + [None]
    gss = [_Group(bs, ts, bs * ts, bs * ts, jnp.repeat(ada[l, ADA_SAMPLE_ROW0:], ts, axis=0))
           for l in range(DEPTH)] + [None]
    hp = _modulate(gps[0], xp)
    hs = _modulate(gss[0], xs)
    outs_p, outs_s = [], []
    for l in range(DEPTH):
        xp, xs, hp, hs, st_p, st_s = _layer(gps[l], gss[l], gps[l + 1], gss[l + 1], l, xp, xs, hp, hs,
                                            rope_p, rope_s, weights, attn_sinks[l], past)
        outs_p.append(st_p)
        outs_s.append(st_s)
    stack = lambda outs, k: jnp.stack([o[k] for o in outs])
    return (xp.reshape(bp, tp, D_MODEL), xs.reshape(bs, ts, D_MODEL),
            stack(outs_p, 0), stack(outs_p, 1), stack(outs_p, 2), stack(outs_p, 3),
            stack(outs_s, 0), stack(outs_s, 1), stack(outs_s, 2), stack(outs_s, 3))
```

```python
import functools

import jax
import jax.numpy as jnp
from jax import lax
from jax.experimental import pallas as pl
from jax.experimental.pallas import tpu as pltpu

D_MODEL = 4096
DEPTH = 2
HEAD_DIM = 64
N_HEADS = 32
N_KV_HEADS = 4
ATTN_DIM = N_HEADS * HEAD_DIM
KV_DIM = N_KV_HEADS * HEAD_DIM
WINDOW = 128
ROPE_THETA = 10000.0
CONV_DIM = D_MODEL // 2
D_FF = 11008
PAST_LEN = 16384
LN_EPS = 1e-5
ALPHA = (2 * DEPTH) ** 0.25

COL_Q = 0
COL_KV = ATTN_DIM
COL_CB = ATTN_DIM + 2 * KV_DIM
COL_CC = COL_CB + CONV_DIM
COL_CX = COL_CC + CONV_DIM
COL_GA = COL_CX + CONV_DIM
COL_GC = COL_GA + D_MODEL

SH1, SC1, G1, SH2, SC2, G2 = range(6)

LANES = 128
SUBLANES = 8
ADA_ROWS = 40
ADA_SAMPLE_ROW0 = 8
ROW_CHUNK = 128
RIDER_CHUNK = 0
VMEM_LIMIT = 56 * 2**20
NEG = -1e30

BF = jnp.bfloat16
F32 = jnp.float32


def _cp(n_axes):
    return pltpu.CompilerParams(dimension_semantics=("arbitrary",) * n_axes, vmem_limit_bytes=VMEM_LIMIT)


def _bf(x):
    return x if x.dtype == BF else x.astype(BF)


def _dot(a, b):
    return jnp.dot(_bf(a), _bf(b), preferred_element_type=F32)


def _row_chunks(tm):
    chunk = min(tm, ROW_CHUNK)
    assert tm % chunk == 0
    return [pl.ds(r, chunk) for r in range(0, tm, chunk)]


def _staged(w_ref, stage_ref):
    stage_ref[...] = w_ref[...].astype(BF)
    return stage_ref


def _rows_of(ref, rows):
    return ref[...] if ref.shape[0] == 1 else ref[rows, :]


def _ada_body(c_ref, w_ref, b_ref, o_ref):
    c = c_ref[...]
    o_ref[...] = _dot(c * jax.nn.sigmoid(c), w_ref[...]) + b_ref[...]


def _mod_body(x_ref, sc_ref, sh_ref, o_ref):
    o_ref[...] = (x_ref[...] * (1.0 + sc_ref[...]) + sh_ref[...]).astype(o_ref.dtype)


def _proj_rope_body(h_ref, w_ref, cos_ref, sin_ref, o_ref, ws_ref, *, n_rope, scale):
    tn = o_ref.shape[1]
    w = _staged(w_ref, ws_ref)
    lane = lax.broadcasted_iota(jnp.int32, (min(h_ref.shape[0], ROW_CHUNK), LANES), 1)
    first_half = (lane & (HEAD_DIM - 1)) < HEAD_DIM // 2
    for rows in _row_chunks(h_ref.shape[0]):
        acc = _dot(h_ref[rows, :], w[...])
        cos = cos_ref[rows, :]
        sin = sin_ref[rows, :]
        for c in range(tn // LANES):
            ch = acc[:, c * LANES:(c + 1) * LANES]
            if c < n_rope:
                rot = jnp.where(first_half,
                                pltpu.roll(ch, LANES - HEAD_DIM // 2, axis=1),
                                pltpu.roll(ch, HEAD_DIM // 2, axis=1))
                ch = ch * cos + rot * sin
            if scale != 1.0:
                ch = ch * scale
            o_ref[rows, c * LANES:(c + 1) * LANES] = ch.astype(o_ref.dtype)


def _attend(q_ref, k2, v2, valid, sink_ref, o_ref):
    tq = q_ref.shape[0]
    nk = k2.shape[0]
    lo = lax.broadcasted_iota(jnp.int32, (nk, LANES), 1) < HEAD_DIM
    lo_q = lax.broadcasted_iota(jnp.int32, (tq, LANES), 1) < HEAD_DIM
    group = N_HEADS // N_KV_HEADS
    for m in range(KV_DIM // LANES):
        kc = k2[:, m * LANES:(m + 1) * LANES]
        vc = v2[:, m * LANES:(m + 1) * LANES]
        kr = pltpu.roll(kc, HEAD_DIM, axis=1)
        vr = pltpu.roll(vc, HEAD_DIM, axis=1)
        for hh in range(2):
            kv_head = 2 * m + hh
            k_lo, k_hi = (kc, kr) if hh == 0 else (kr, kc)
            v_lo, v_hi = (vc, vr) if hh == 0 else (vr, vc)
            kbd = jnp.concatenate([jnp.where(lo, k_lo, 0.0), jnp.where(lo, 0.0, k_hi)], axis=0).astype(BF)
            vbd = jnp.concatenate([jnp.where(lo, v_lo, 0.0), jnp.where(lo, 0.0, v_hi)], axis=0).astype(BF)
            pairs = [kv_head * (group // 2) + pp for pp in range(group // 2)]
            qs = _bf(jnp.concatenate([q_ref[:, p * LANES:(p + 1) * LANES] for p in pairs], axis=0))
            s_all = lax.dot_general(qs, kbd, (((1,), (1,)), ((), ())), preferred_element_type=F32)
            es, inv = [], []
            for pp, p in enumerate(pairs):
                s = s_all[pp * tq:(pp + 1) * tq, :]
                e_pair, d_pair = [], []
                for half in range(2):
                    sh = jnp.where(valid, s[:, half * nk:(half + 1) * nk], NEG)
                    sink = sink_ref[2 * p + half]
                    mx = jnp.maximum(jnp.max(sh, axis=-1, keepdims=True), sink)
                    e = jnp.exp(sh - mx)
                    e_pair.append(e)
                    d_pair.append(jnp.sum(e, axis=-1, keepdims=True) + jnp.exp(sink - mx))
                es.append(jnp.concatenate(e_pair, axis=1))
                inv.append(jnp.where(lo_q, 1.0 / d_pair[0], 1.0 / d_pair[1]))
            o_all = jnp.dot(_bf(jnp.concatenate(es, axis=0)), vbd, preferred_element_type=F32)
            for pp, p in enumerate(pairs):
                o = o_all[pp * tq:(pp + 1) * tq, :] * inv[pp]
                o_ref[:, p * LANES:(p + 1) * LANES] = o.astype(o_ref.dtype)


def _attn_prompt_body(sink_ref, q_ref, kvp_ref, kvc_ref, o_ref):
    n = pl.program_id(1)
    kvp = kvp_ref[...]
    kvc = kvc_ref[...]
    k2 = jnp.concatenate([kvp[:, :KV_DIM], kvc[:, :KV_DIM]], axis=0)
    v2 = jnp.concatenate([kvp[:, KV_DIM:], kvc[:, KV_DIM:]], axis=0)
    i = lax.broadcasted_iota(jnp.int32, (WINDOW, 2 * WINDOW), 0)
    j = lax.broadcasted_iota(jnp.int32, (WINDOW, 2 * WINDOW), 1)
    valid = (j >= i) & (j <= i + WINDOW) & ((j >= WINDOW) | (n > 0))
    _attend(q_ref, k2, v2, valid, sink_ref, o_ref)


def _attn_sample_body(sink_ref, q_ref, kvn_ref, ck_ref, cv_ref, o_ref, ks_ref, vs_ref):
    t = q_ref.shape[0]
    ck = ck_ref[...]
    cv = cv_ref[...]
    kvn = kvn_ref[...]
    pad = jnp.zeros((WINDOW - t, KV_DIM), F32)
    k2 = jnp.concatenate([ck, kvn[:, :KV_DIM], pad], axis=0)
    v2 = jnp.concatenate([cv, kvn[:, KV_DIM:], pad], axis=0)
    i = lax.broadcasted_iota(jnp.int32, (t, 2 * WINDOW), 0)
    j = lax.broadcasted_iota(jnp.int32, (t, 2 * WINDOW), 1)
    valid = (j >= i) & (j <= i + WINDOW)
    _attend(q_ref, k2, v2, valid, sink_ref, o_ref)
    ks_ref[0:WINDOW - t, :] = ck[t:, :]
    ks_ref[WINDOW - t:, :] = kvn[:, :KV_DIM]
    vs_ref[0:WINDOW - t, :] = cv[t:, :]
    vs_ref[WINDOW - t:, :] = kvn[:, KV_DIM:]


def _conv3_carry(u, w_ref, prev):
    ext = jnp.concatenate([prev, u], axis=0)
    x1 = pltpu.roll(ext, 1, axis=0)[SUBLANES:, :]
    x2 = pltpu.roll(ext, 2, axis=0)[SUBLANES:, :]
    return w_ref[0:1, :] * x2 + w_ref[1:2, :] * x1 + w_ref[2:3, :] * u


def _conv3_segments(u, w_ref, e, seg):
    t = lax.broadcasted_iota(jnp.int32, u.shape, 0) & (seg - 1)
    x1 = jnp.where(t >= 1, pltpu.roll(u, 1, axis=0), pltpu.roll(e, u.shape[0] - 1, axis=0))
    x2 = jnp.where(t >= 2, pltpu.roll(u, 2, axis=0), e)
    return w_ref[0:1, :] * x2 + w_ref[1:2, :] * x1 + w_ref[2:3, :] * u


def _carry_in(carry_ref, tiles_per_seq):
    i = pl.program_id(0)
    j = pl.program_id(1)

    @pl.when(i % tiles_per_seq == 0)
    def _():
        carry_ref[j] = jnp.zeros(carry_ref.shape[1:], F32)

    return carry_ref[j]


def _chunks_with_riders(h_ref, hs_ref):
    chunks = _row_chunks(h_ref.shape[0])
    out = []
    for c, rows in enumerate(chunks):
        lhs = h_ref[rows, :]
        if c == RIDER_CHUNK:
            lhs = jnp.concatenate([lhs, hs_ref[...]], axis=0)
        out.append((rows, lhs, min(h_ref.shape[0], ROW_CHUNK)))
    return out


def _inconv_body(h_ref, hs_ref, wb_ref, wc_ref, wx_ref, cw_ref, e_ref, s_ref, tail_ref, ss_ref, us_ref,
                 carry_ref, wbs_ref, wcs_ref, wxs_ref, *, tiles_per_seq, seg):
    j = pl.program_id(1)
    wb = _staged(wb_ref, wbs_ref)
    wc = _staged(wc_ref, wcs_ref)
    wx = _staged(wx_ref, wxs_ref)
    prev = _carry_in(carry_ref, tiles_per_seq)
    for rows, lhs, n in _chunks_with_riders(h_ref, hs_ref):
        u_all = _dot(lhs, wc[...]) * _dot(lhs, wx[...])
        b_all = _dot(lhs, wb[...])
        u = u_all[:n, :]
        y = _conv3_carry(u, cw_ref, prev)
        prev = u[n - SUBLANES:, :]
        s_ref[rows, :] = (b_all[:n, :] * y).astype(s_ref.dtype)
        if u_all.shape[0] > n:
            us = u_all[n:, :]
            us_ref[...] = us
            ss_ref[...] = (b_all[n:, :] * _conv3_segments(us, cw_ref, e_ref[...], seg)).astype(ss_ref.dtype)
    carry_ref[j] = prev
    tail_ref[...] = prev


def _merged_body(h_ref, a_ref, s_ref, hs_ref, as_ref, ss_ref, wga_ref, wgc_ref, wab_ref, wcb_ref, o_ref, os_ref):
    chunks = _row_chunks(h_ref.shape[0])
    n = min(h_ref.shape[0], ROW_CHUNK)
    for c, rows in enumerate(chunks):
        h, a, s = h_ref[rows, :], a_ref[rows, :], s_ref[rows, :]
        last = c == RIDER_CHUNK
        if last:
            h = jnp.concatenate([h, hs_ref[...]], axis=0)
            a = jnp.concatenate([a, _bf(as_ref[...])], axis=0)
            s = jnp.concatenate([s, ss_ref[...]], axis=0)
        attn_br = jax.nn.sigmoid(_dot(h, wga_ref[...])) * _dot(a, wab_ref[...])
        conv_br = jax.nn.sigmoid(_dot(h, wgc_ref[...])) * _dot(s, wcb_ref[...])
        m = attn_br + conv_br
        o_ref[rows, :] = m[:n, :].astype(o_ref.dtype)
        if last:
            os_ref[...] = m[n:, :].astype(os_ref.dtype)


def _mm_resid_body(a_ref, as_ref, w_ref, x_ref, xs_ref, g_ref, gs_ref, o_ref, os_ref):
    for rows, lhs, n in _chunks_with_riders(a_ref, as_ref):
        y = _dot(lhs, w_ref[...])
        o_ref[rows, :] = ALPHA * x_ref[rows, :] + _rows_of(g_ref, rows) * y[:n, :]
        if y.shape[0] > n:
            os_ref[...] = ALPHA * xs_ref[...] + gs_ref[...] * y[n:, :]


def _with_casts(body, n_in, n_out, n_casts):
    def wrapped(*refs):
        ins, refs_ = refs[:n_in], refs[n_in:]
        srcs, refs_ = refs_[:n_casts], refs_[n_casts:]
        outs, refs_ = refs_[:n_out], refs_[n_out:]
        dsts, scratch = refs_[:n_casts], refs_[n_casts:]
        for src, dst in zip(srcs, dsts):
            dst[...] = src[...].astype(BF)
        body(*ins, *outs, *scratch)
    return wrapped


def _layernorm(r, g, b):
    mu = jnp.mean(r, axis=-1, keepdims=True)
    d = r - mu
    var = jnp.mean(d * d, axis=-1, keepdims=True)
    return d * lax.rsqrt(var + LN_EPS) * g + b


def _ln_mod_body(r_ref, g_ref, b_ref, sc_ref, sh_ref, x_ref, h_ref):
    y = _layernorm(r_ref[...], g_ref[...], b_ref[...])
    x_ref[...] = y
    h_ref[...] = (y * (1.0 + sc_ref[...]) + sh_ref[...]).astype(h_ref.dtype)


def _ln_body(r_ref, g_ref, b_ref, x_ref):
    x_ref[...] = _layernorm(r_ref[...], g_ref[...], b_ref[...])


def _silu_mul(g, v):
    return g * jax.nn.sigmoid(g) * v


def _ffnup_body(h_ref, hs_ref, wg_ref, wv_ref, cwg_ref, cwv_ref, eg_ref, ev_ref, wd_ref,
                act_ref, tg_ref, tv_ref, acts_ref, ugs_ref, uvs_ref, wdb_ref,
                cg_ref, cv_ref, wgs_ref, wvs_ref, *, tiles_per_seq, seg):
    j = pl.program_id(1)
    wdb_ref[...] = wd_ref[...].astype(BF)
    wg = _staged(wg_ref, wgs_ref)
    wv = _staged(wv_ref, wvs_ref)
    prev_g = _carry_in(cg_ref, tiles_per_seq)
    prev_v = _carry_in(cv_ref, tiles_per_seq)
    for rows, lhs, n in _chunks_with_riders(h_ref, hs_ref):
        ug_all = _dot(lhs, wg[...])
        uv_all = _dot(lhs, wv[...])
        ug = ug_all[:n, :]
        uv = uv_all[:n, :]
        yg = _conv3_carry(ug, cwg_ref, prev_g)
        yv = _conv3_carry(uv, cwv_ref, prev_v)
        prev_g = ug[n - SUBLANES:, :]
        prev_v = uv[n - SUBLANES:, :]
        act_ref[rows, :] = _silu_mul(yg, yv).astype(act_ref.dtype)
        if ug_all.shape[0] > n:
            ugs = ug_all[n:, :]
            uvs = uv_all[n:, :]
            ugs_ref[...] = ugs
            uvs_ref[...] = uvs
            ygs = _conv3_segments(ugs, cwg_ref, eg_ref[...], seg)
            yvs = _conv3_segments(uvs, cwv_ref, ev_ref[...], seg)
            acts_ref[...] = _silu_mul(ygs, yvs).astype(acts_ref.dtype)
    cg_ref[j] = prev_g
    cv_ref[j] = prev_v
    tg_ref[...] = prev_g
    tv_ref[...] = prev_v


class _Group:
    def __init__(self, batch, seq, tm, tm_big, ada_rows):
        self.batch, self.seq, self.tm, self.tm_big = batch, seq, tm, tm_big
        self.m = batch * seq
        self.per_row = seq < tm
        self.ada = ada_rows

    def param_spec(self, chunk, tm, tn):
        cb = chunk * (D_MODEL // tn)
        if self.per_row:
            return pl.BlockSpec((tm, tn), lambda i, j: (i, cb + j))
        tps = self.seq // tm
        return pl.BlockSpec((None, 1, tn), lambda i, j: (i // tps, 0, cb + j))


def _resident_rows(tm, width):
    return pl.BlockSpec((tm, width), lambda i, j: (i, 0), pipeline_mode=pl.Buffered(1))


def _ada(c_all, w_ada, b_ada):
    tn = 512
    n = w_ada.shape[-1]
    return pl.pallas_call(
        _ada_body,
        out_shape=jax.ShapeDtypeStruct((DEPTH, ADA_ROWS, n), F32),
        grid=(DEPTH, n // tn),
        in_specs=[pl.BlockSpec((ADA_ROWS, D_MODEL), lambda l, j: (0, 0)),
                  pl.BlockSpec((None, D_MODEL, tn), lambda l, j: (l, 0, j)),
                  pl.BlockSpec((None, 1, tn), lambda l, j: (l, 0, j))],
        out_specs=pl.BlockSpec((None, ADA_ROWS, tn), lambda l, j: (l, 0, j)),
        compiler_params=_cp(2), name="ada",
    )(c_all, w_ada, b_ada.reshape(DEPTH, 1, n))


def _modulate(grp, x):
    tr = min(grp.tm, 512)
    return pl.pallas_call(
        _mod_body,
        out_shape=jax.ShapeDtypeStruct((grp.m, D_MODEL), BF),
        grid=(grp.m // tr,),
        in_specs=[pl.BlockSpec((tr, D_MODEL), lambda i: (i, 0)),
                  _row_param_spec(grp, SC1, tr), _row_param_spec(grp, SH1, tr)],
        out_specs=pl.BlockSpec((tr, D_MODEL), lambda i: (i, 0)),
        compiler_params=_cp(1), name="modulate",
    )(x, grp.ada, grp.ada)


def _row_param_spec(grp, chunk, tr):
    if grp.per_row:
        return pl.BlockSpec((tr, D_MODEL), lambda i: (i, chunk))
    tps = grp.seq // tr
    return pl.BlockSpec((None, 1, D_MODEL), lambda i: (i // tps, 0, chunk))


def _proj_rope(grp, h, w_in, l, col0, n_cols, tm, tn, cos, sin, n_rope, scale, out_dtype, name, cast_of=None):
    cb = col0 // tn
    t_tiles = cos.shape[0] // tm
    nj = n_cols // tn
    casts = [] if cast_of is None else [_row_cast(cast_of, l, (grp.m // tm) * nj, lambda i, j: i * nj + j)]
    out = pl.pallas_call(
        _with_casts(functools.partial(_proj_rope_body, n_rope=n_rope, scale=scale), 4, 1, len(casts)),
        out_shape=(jax.ShapeDtypeStruct((grp.m, n_cols), out_dtype), *[c[3] for c in casts]),
        grid=(grp.m // tm, nj),
        in_specs=[pl.BlockSpec((tm, D_MODEL), lambda i, j: (i, 0)),
                  pl.BlockSpec((None, D_MODEL, tn), lambda i, j: (l, 0, cb + j)),
                  pl.BlockSpec((tm, LANES), lambda i, j: (i % t_tiles, 0)),
                  pl.BlockSpec((tm, LANES), lambda i, j: (i % t_tiles, 0)),
                  *[c[1] for c in casts]],
        out_specs=(pl.BlockSpec((tm, tn), lambda i, j: (i, j)), *[c[2] for c in casts]),
        scratch_shapes=[pltpu.VMEM((D_MODEL, tn), BF)],
        compiler_params=_cp(2), name=name,
    )(h, w_in, cos, sin, *[c[0] for c in casts])
    return out[0] if cast_of is None else out


def _attn_prompt(sinks, q, kv, batch, seq, l, w_in, w_attn_br, w_conv_br):
    nb = seq // WINDOW
    step_of = lambda b, n: b * nb + n
    casts = [_col_cast(w_in, l, COL_GA, 2 * D_MODEL, batch * nb, step_of),
             _row_cast(w_attn_br, l, batch * nb, step_of), _row_cast(w_conv_br, l, batch * nb, step_of)]
    return pl.pallas_call(
        _with_casts(_attn_prompt_body, 4, 1, len(casts)),
        out_shape=(jax.ShapeDtypeStruct(q.shape, BF), *[c[3] for c in casts]),
        grid=(batch, nb),
        in_specs=[pl.BlockSpec(memory_space=pltpu.SMEM),
                  pl.BlockSpec((WINDOW, ATTN_DIM), lambda b, n: (b * nb + n, 0)),
                  pl.BlockSpec((WINDOW, 2 * KV_DIM), lambda b, n: (b * nb + jnp.maximum(n - 1, 0), 0)),
                  pl.BlockSpec((WINDOW, 2 * KV_DIM), lambda b, n: (b * nb + n, 0)),
                  *[c[1] for c in casts]],
        out_specs=(pl.BlockSpec((WINDOW, ATTN_DIM), lambda b, n: (b * nb + n, 0)), *[c[2] for c in casts]),
        compiler_params=_cp(2), name="attn_prompt",
    )(sinks, q, kv, kv, *[c[0] for c in casts])


def _attn_sample(sinks, q, kvn, cache_k, cache_v, l, batch, seq):
    return pl.pallas_call(
        _attn_sample_body,
        out_shape=(jax.ShapeDtypeStruct(q.shape, F32),
                   jax.ShapeDtypeStruct((batch, WINDOW, KV_DIM), F32),
                   jax.ShapeDtypeStruct((batch, WINDOW, KV_DIM), F32)),
        grid=(batch,),
        in_specs=[pl.BlockSpec(memory_space=pltpu.SMEM),
                  pl.BlockSpec((seq, ATTN_DIM), lambda b: (b, 0)),
                  pl.BlockSpec((seq, 2 * KV_DIM), lambda b: (b, 0)),
                  pl.BlockSpec((None, None, WINDOW, KV_DIM), lambda b: (l, b, 0, 0)),
                  pl.BlockSpec((None, None, WINDOW, KV_DIM), lambda b: (l, b, 0, 0))],
        out_specs=(pl.BlockSpec((seq, ATTN_DIM), lambda b: (b, 0)),
                   pl.BlockSpec((None, WINDOW, KV_DIM), lambda b: (b, 0, 0)),
                   pl.BlockSpec((None, WINDOW, KV_DIM), lambda b: (b, 0, 0))),
        compiler_params=_cp(1), name="attn_sample",
    )(sinks, q, kvn, cache_k, cache_v)


def _rider_rows(gp, gs, tm):
    rs = gs.m // (gp.m // tm)
    assert rs * (gp.m // tm) == gs.m and rs % gs.seq == 0 and rs % (2 * SUBLANES) == 0
    return rs


def _in_conv(gp, gs, h, hs, w_in, conv_w, l, past_rows):
    tm, tn = gp.tm_big, 256
    nj = CONV_DIM // tn
    rs = _rider_rows(gp, gs, tm)
    w_spec = lambda col0: pl.BlockSpec((None, D_MODEL, tn), lambda i, j: (l, 0, col0 // tn + j))
    tile = lambda rows: pl.BlockSpec((rows, tn), lambda i, j: (i, j))
    return pl.pallas_call(
        functools.partial(_inconv_body, tiles_per_seq=gp.seq // tm, seg=gs.seq),
        out_shape=(jax.ShapeDtypeStruct((gp.m, CONV_DIM), BF),
                   jax.ShapeDtypeStruct((gp.m // tm, SUBLANES, CONV_DIM), F32),
                   jax.ShapeDtypeStruct((gs.m, CONV_DIM), BF),
                   jax.ShapeDtypeStruct((gs.m, CONV_DIM), F32)),
        grid=(gp.m // tm, nj),
        in_specs=[_resident_rows(tm, D_MODEL), pl.BlockSpec((rs, D_MODEL), lambda i, j: (i, 0)),
                  w_spec(COL_CB), w_spec(COL_CC), w_spec(COL_CX),
                  pl.BlockSpec((None, 3, tn), lambda i, j: (l, 0, j)), tile(rs)],
        out_specs=(tile(tm), pl.BlockSpec((None, SUBLANES, tn), lambda i, j: (i, 0, j)), tile(rs), tile(rs)),
        scratch_shapes=[pltpu.VMEM((nj, SUBLANES, tn), F32)] + [pltpu.VMEM((D_MODEL, tn), BF)] * 3,
        compiler_params=_cp(2), name="in_conv",
    )(h, hs, w_in, w_in, w_in, conv_w, past_rows)


def _row_cast(w, l, steps, step_of):
    _, r, c = w.shape
    rows = r // steps
    assert rows * steps == r and rows % (2 * SUBLANES) == 0
    return (w, pl.BlockSpec((None, rows, c), lambda *g: (l, step_of(*g), 0)),
            pl.BlockSpec((rows, c), lambda *g: (step_of(*g), 0)), jax.ShapeDtypeStruct((r, c), BF))


def _col_cast(w, l, col0, n_cols, steps, step_of):
    _, r, _ = w.shape
    cols = n_cols // steps
    assert cols * steps == n_cols and cols % LANES == 0 and col0 % cols == 0
    return (w, pl.BlockSpec((None, r, cols), lambda *g: (l, 0, col0 // cols + step_of(*g))),
            pl.BlockSpec((r, cols), lambda *g: (0, step_of(*g))), jax.ShapeDtypeStruct((r, n_cols), BF))


def _merged(gp, gs, h, attn, s, hs, attn_s, s_s, w_gates, w_ab, w_cb):
    tm, tn = gp.tm, 256
    rs = _rider_rows(gp, gs, tm)
    row = lambda rows, width: pl.BlockSpec((rows, width), lambda i, j: (i, 0))
    w2d = lambda k, col0: pl.BlockSpec((k, tn), lambda i, j: (0, col0 // tn + j))
    tile = lambda rows: pl.BlockSpec((rows, tn), lambda i, j: (i, j))
    return pl.pallas_call(
        _merged_body,
        out_shape=(jax.ShapeDtypeStruct((gp.m, D_MODEL), BF), jax.ShapeDtypeStruct((gs.m, D_MODEL), BF)),
        grid=(gp.m // tm, D_MODEL // tn),
        in_specs=[row(tm, D_MODEL), row(tm, ATTN_DIM), row(tm, CONV_DIM),
                  row(rs, D_MODEL), row(rs, ATTN_DIM), row(rs, CONV_DIM),
                  w2d(D_MODEL, 0), w2d(D_MODEL, COL_GC - COL_GA), w2d(ATTN_DIM, 0), w2d(CONV_DIM, 0)],
        out_specs=(tile(tm), tile(rs)),
        compiler_params=_cp(2), name="merged",
    )(h, attn, s, hs, attn_s, s_s, w_gates, w_gates, w_ab, w_cb)


def _mm_resid(gp, gs, a, a_s, w_bf16, x, x_s, gate_chunk, tm, tn, name):
    k = a.shape[1]
    rs = _rider_rows(gp, gs, tm)
    tile = lambda rows: pl.BlockSpec((rows, tn), lambda i, j: (i, j))
    return pl.pallas_call(
        _mm_resid_body,
        out_shape=(jax.ShapeDtypeStruct((gp.m, D_MODEL), F32), jax.ShapeDtypeStruct((gs.m, D_MODEL), F32)),
        grid=(gp.m // tm, D_MODEL // tn),
        in_specs=[pl.BlockSpec((tm, k), lambda i, j: (i, 0)), pl.BlockSpec((rs, k), lambda i, j: (i, 0)),
                  pl.BlockSpec((k, tn), lambda i, j: (0, j)), tile(tm), tile(rs),
                  gp.param_spec(gate_chunk, tm, tn), gs.param_spec(gate_chunk, rs, tn)],
        out_specs=(tile(tm), tile(rs)),
        compiler_params=_cp(2), name=name,
    )(a, a_s, w_bf16, x, x_s, gp.ada, gs.ada)


def _ln(grp, r, g, b, l, mod_grp, mod_chunks, name):
    tr = min(grp.m, 512)
    row = pl.BlockSpec((tr, D_MODEL), lambda i: (i, 0))
    vec = pl.BlockSpec((None, 1, D_MODEL), lambda i: (l, 0, 0))
    g3 = g.reshape(DEPTH, 1, D_MODEL)
    b3 = b.reshape(DEPTH, 1, D_MODEL)
    if mod_chunks is None:
        return pl.pallas_call(
            _ln_body, out_shape=jax.ShapeDtypeStruct((grp.m, D_MODEL), F32), grid=(grp.m // tr,),
            in_specs=[row, vec, vec], out_specs=row, compiler_params=_cp(1), name=name,
        )(r, g3, b3), None
    sc, sh = mod_chunks
    return pl.pallas_call(
        _ln_mod_body,
        out_shape=(jax.ShapeDtypeStruct((grp.m, D_MODEL), F32), jax.ShapeDtypeStruct((grp.m, D_MODEL), BF)),
        grid=(grp.m // tr,),
        in_specs=[row, vec, vec, _row_param_spec(mod_grp, sc, tr), _row_param_spec(mod_grp, sh, tr)],
        out_specs=(row, row), compiler_params=_cp(1), name=name,
    )(r, g3, b3, mod_grp.ada, mod_grp.ada)


def _ffn_up(gp, gs, h, hs, w_up, ffn_conv_w, w_down, l, past_g, past_v):
    tm, tn = gp.tm_big, 256
    nj = D_FF // tn
    rs = _rider_rows(gp, gs, tm)
    steps = (gp.m // tm) * nj
    wd_rows = D_FF // steps
    assert wd_rows * steps == D_FF and wd_rows % (2 * SUBLANES) == 0
    tile = lambda rows: pl.BlockSpec((rows, tn), lambda i, j: (i, j))
    tail_shape = jax.ShapeDtypeStruct((gp.m // tm, SUBLANES, D_FF), F32)
    tail_spec = pl.BlockSpec((None, SUBLANES, tn), lambda i, j: (i, 0, j))
    full_s = jax.ShapeDtypeStruct((gs.m, D_FF), F32)
    return pl.pallas_call(
        functools.partial(_ffnup_body, tiles_per_seq=gp.seq // tm, seg=gs.seq),
        out_shape=(jax.ShapeDtypeStruct((gp.m, D_FF), BF), tail_shape, tail_shape,
                   jax.ShapeDtypeStruct((gs.m, D_FF), BF), full_s, full_s,
                   jax.ShapeDtypeStruct((D_FF, D_MODEL), BF)),
        grid=(gp.m // tm, nj),
        in_specs=[_resident_rows(tm, D_MODEL), pl.BlockSpec((rs, D_MODEL), lambda i, j: (i, 0)),
                  pl.BlockSpec((None, D_MODEL, tn), lambda i, j: (l, 0, j)),
                  pl.BlockSpec((None, D_MODEL, tn), lambda i, j: (l, 0, nj + j)),
                  pl.BlockSpec((None, 3, tn), lambda i, j: (l, 0, j)),
                  pl.BlockSpec((None, 3, tn), lambda i, j: (l, 0, nj + j)),
                  tile(rs), tile(rs),
                  pl.BlockSpec((None, wd_rows, D_MODEL), lambda i, j: (l, i * nj + j, 0))],
        out_specs=(tile(tm), tail_spec, tail_spec, tile(rs), tile(rs), tile(rs),
                   pl.BlockSpec((wd_rows, D_MODEL), lambda i, j: (i * nj + j, 0))),
        scratch_shapes=[pltpu.VMEM((nj, SUBLANES, tn), F32), pltpu.VMEM((nj, SUBLANES, tn), F32),
                        pltpu.VMEM((D_MODEL, tn), BF), pltpu.VMEM((D_MODEL, tn), BF)],
        compiler_params=_cp(2), name="ffn_up",
    )(h, hs, w_up, w_up, ffn_conv_w, ffn_conv_w, past_g, past_v, w_down)


def _rope_tables(pos):
    inv = ROPE_THETA ** (-jnp.arange(0, HEAD_DIM, 2, dtype=F32) / HEAD_DIM)
    ang = pos.astype(F32)[:, None] * inv[None, :]
    cos, sin = jnp.cos(ang), jnp.sin(ang)
    return jnp.concatenate([cos] * 4, axis=-1), jnp.concatenate([-sin, sin, -sin, sin], axis=-1)


def _expand_past(past, seq):
    batch, _, n = past.shape
    return jnp.concatenate([past, jnp.zeros((batch, seq - 2, n), past.dtype)], axis=1).reshape(batch * seq, n)


def _last_rows(x, batch, seq, n_rows):
    return x.reshape(batch, seq, x.shape[-1])[:, seq - n_rows:, :]


def _qkv(grp, l, h, rope, w_in, cast_of=None):
    cos, sin = rope
    q = _proj_rope(grp, h, w_in, l, COL_Q, ATTN_DIM, grp.tm_big, 256, cos, sin, 256 // LANES, HEAD_DIM ** -0.5,
                   F32 if grp.per_row else BF, "proj_q", cast_of)
    kv = _proj_rope(grp, h, w_in, l, COL_KV, 2 * KV_DIM, grp.tm, 2 * KV_DIM, cos, sin, KV_DIM // LANES, 1.0,
                    F32, "proj_kv")
    return q, kv


def _layer(gp, gs, next_gp, next_gs, l, xp, xs, hp, hs, rope_p, rope_s, weights, sinks, past):
    (w_in, conv_w, w_attn_br, w_conv_br, w_out, ln1_g, ln1_b, w_up, ffn_conv_w, w_down, ln2_g, ln2_b) = weights
    cache_k, cache_v, state_conv, state_ffn = past
    tps = gp.seq // gp.tm_big
    state_rows = lambda tails: tails[tps - 1::tps, SUBLANES - 2:, :]

    q_s, kv_s = _qkv(gs, l, hs, rope_s, w_in)
    attn_s, k_s, v_s = _attn_sample(sinks, q_s, kv_s, cache_k, cache_v, l, gs.batch, gs.seq)
    (q_p, w_out_b), kv_p = _qkv(gp, l, hp, rope_p, w_in, cast_of=w_out)
    attn_p, w_gates, w_ab, w_cb = _attn_prompt(sinks, q_p, kv_p, gp.batch, gp.seq, l, w_in, w_attn_br, w_conv_br)
    kv_tail = _last_rows(kv_p, gp.batch, gp.seq, WINDOW)
    k_p, v_p = kv_tail[..., :KV_DIM], kv_tail[..., KV_DIM:]

    s_p, tails, s_s, u_s = _in_conv(gp, gs, hp, hs, w_in, conv_w, l, _expand_past(state_conv[l], gs.seq))
    conv_p = state_rows(tails)
    conv_s = _last_rows(u_s, gs.batch, gs.seq, 2)

    merged_p, merged_s = _merged(gp, gs, hp, attn_p, s_p, hs, attn_s, s_s, w_gates, w_ab, w_cb)
    r1_p, r1_s = _mm_resid(gp, gs, merged_p, merged_s, w_out_b, xp, xs, G1, gp.tm, 1024, "out_proj")
    x1_s, h2_s = _ln(gs, r1_s, ln1_g, ln1_b, l, gs, (SC2, SH2), "ln1")
    x1_p, h2_p = _ln(gp, r1_p, ln1_g, ln1_b, l, gp, (SC2, SH2), "ln1")

    eg = _expand_past(state_ffn[l][..., :D_FF], gs.seq)
    ev = _expand_past(state_ffn[l][..., D_FF:], gs.seq)
    act_p, tg, tv, act_s, ug_s, uv_s, w_down_b = _ffn_up(gp, gs, h2_p, h2_s, w_up, ffn_conv_w, w_down, l, eg, ev)
    ffn_p = state_rows(jnp.concatenate([tg, tv], axis=-1))
    ffn_s = _last_rows(jnp.concatenate([ug_s, uv_s], axis=-1), gs.batch, gs.seq, 2)

    r2_p, r2_s = _mm_resid(gp, gs, act_p, act_s, w_down_b, x1_p, x1_s, G2, 512, 512, "ffn_down")
    mod = None if next_gp is None else (SC1, SH1)
    xs, hs = _ln(gs, r2_s, ln2_g, ln2_b, l, next_gs, mod, "ln2")
    xp, hp = _ln(gp, r2_p, ln2_g, ln2_b, l, next_gp, mod, "ln2")

    heads = lambda k, grp: k.reshape(grp.batch, WINDOW, N_KV_HEADS, HEAD_DIM)
    return (xp, xs, hp, hs, (heads(k_p, gp), heads(v_p, gp), conv_p, ffn_p),
            (heads(k_s, gs), heads(v_s, gs), conv_s, ffn_s))


def kernel(x_prompt, x_sample, cache_attn_k, cache_attn_v, state_conv, state_ffn_conv, c_prompt, c_sample, w_ada, b_ada, w_in, attn_sinks, conv_w, w_attn_br, w_conv_br, w_out, ln1_g, ln1_b, w_up, ffn_conv_w, w_down, ln2_g, ln2_b):
    bp, tp, _ = x_prompt.shape
    bs, ts, _ = x_sample.shape
    assert bp <= ADA_SAMPLE_ROW0 and ADA_SAMPLE_ROW0 + bs == ADA_ROWS

    c_all = jnp.concatenate([c_prompt, jnp.zeros((ADA_SAMPLE_ROW0 - bp, D_MODEL), F32), c_sample], axis=0)
    ada = _ada(c_all, w_ada, b_ada)

    weights = (w_in, conv_w, w_attn_br, w_conv_br, w_out, ln1_g, ln1_b, w_up, ffn_conv_w, w_down, ln2_g, ln2_b)
    rope_p = _rope_tables(jnp.arange(tp, dtype=jnp.int32))
    rope_s = tuple(jnp.tile(t, (bs, 1)) for t in _rope_tables(PAST_LEN + jnp.arange(ts, dtype=jnp.int32)))
    cache_k = cache_attn_k.reshape(DEPTH, bs, WINDOW, KV_DIM)
    cache_v = cache_attn_v.reshape(DEPTH, bs, WINDOW, KV_DIM)
    past = (cache_k, cache_v, state_conv, state_ffn_conv)

    xp = x_prompt.reshape(bp * tp, D_MODEL)
    xs = x_sample.reshape(bs * ts, D_MODEL)
    gps = [_Group(bp, tp, 1024, 2048, ada[l, :bp].reshape(bp, 1, -1)) for l in range(DEPTH)] + [None]
    gss = [_Group(bs, ts, bs * ts, bs * ts, jnp.repeat(ada[l, ADA_SAMPLE_ROW0:], ts, axis=0))
           for l in range(DEPTH)] + [None]
    hp = _modulate(gps[0], xp)
    hs = _modulate(gss[0], xs)
    outs_p, outs_s = [], []
    for l in range(DEPTH):
        xp, xs, hp, hs, st_p, st_s = _layer(gps[l], gss[l], gps[l + 1], gss[l + 1], l, xp, xs, hp, hs,
                                            rope_p, rope_s, weights, attn_sinks[l], past)
        outs_p.append(st_p)
        outs_s.append(st_s)
    stack = lambda outs, k: jnp.stack([o[k] for o in outs])
    return (xp.reshape(bp, tp, D_MODEL), xs.reshape(bs, ts, D_MODEL),
            stack(outs_p, 0), stack(outs_p, 1), stack(outs_p, 2), stack(outs_p, 3),
            stack(outs_s, 0), stack(outs_s, 1), stack(outs_s, 2), stack(outs_s, 3))
```

```python
import functools

import jax
import jax.numpy as jnp
from jax import lax
from jax.experimental import pallas as pl
from jax.experimental.pallas import tpu as pltpu

D_MODEL = 4096
DEPTH = 2
HEAD_DIM = 64
N_HEADS = 32
N_KV_HEADS = 4
ATTN_DIM = N_HEADS * HEAD_DIM
KV_DIM = N_KV_HEADS * HEAD_DIM
WINDOW = 128
ROPE_THETA = 10000.0
CONV_DIM = D_MODEL // 2
D_FF = 11008
PAST_LEN = 16384
LN_EPS = 1e-5
ALPHA = (2 * DEPTH) ** 0.25

COL_Q = 0
COL_KV = ATTN_DIM
COL_CB = ATTN_DIM + 2 * KV_DIM
COL_CC = COL_CB + CONV_DIM
COL_CX = COL_CC + CONV_DIM
COL_GA = COL_CX + CONV_DIM
COL_GC = COL_GA + D_MODEL

SH1, SC1, G1, SH2, SC2, G2 = range(6)

LANES = 128
SUBLANES = 8
ADA_ROWS = 40
ADA_SAMPLE_ROW0 = 8
ROW_CHUNK = 512
ROPE_CHUNK = 256
RIDER_CHUNK = 0
VMEM_LIMIT = 56 * 2**20
NEG = -1e30

BF = jnp.bfloat16
F32 = jnp.float32


def _cp(n_axes):
    return pltpu.CompilerParams(dimension_semantics=("arbitrary",) * n_axes, vmem_limit_bytes=VMEM_LIMIT)


def _bf(x):
    return x if x.dtype == BF else x.astype(BF)


def _dot(a, b):
    return jnp.dot(_bf(a), _bf(b), preferred_element_type=F32)


def _row_chunks(tm, chunk=None):
    chunk = min(tm, chunk or ROW_CHUNK)
    assert tm % chunk == 0
    return [pl.ds(r, chunk) for r in range(0, tm, chunk)]


def _staged(w_ref, stage_ref):
    stage_ref[...] = w_ref[...].astype(BF)
    return stage_ref


def _rows_of(ref, rows):
    return ref[...] if ref.shape[0] == 1 else ref[rows, :]


def _ada_body(c_ref, w_ref, b_ref, o_ref):
    c = c_ref[...]
    o_ref[...] = _dot(c * jax.nn.sigmoid(c), w_ref[...]) + b_ref[...]


def _mod_body(x_ref, sc_ref, sh_ref, o_ref):
    o_ref[...] = (x_ref[...] * (1.0 + sc_ref[...]) + sh_ref[...]).astype(o_ref.dtype)


def _proj_rope_body(h_ref, w_ref, cos_ref, sin_ref, o_ref, ws_ref, *, n_rope, scale):
    tn = o_ref.shape[1]
    w = _staged(w_ref, ws_ref)
    lane = lax.broadcasted_iota(jnp.int32, (min(h_ref.shape[0], ROPE_CHUNK), LANES), 1)
    first_half = (lane & (HEAD_DIM - 1)) < HEAD_DIM // 2
    for rows in _row_chunks(h_ref.shape[0], ROPE_CHUNK):
        acc = _dot(h_ref[rows, :], w[...])
        cos = cos_ref[rows, :]
        sin = sin_ref[rows, :]
        for c in range(tn // LANES):
            ch = acc[:, c * LANES:(c + 1) * LANES]
            if c < n_rope:
                rot = jnp.where(first_half,
                                pltpu.roll(ch, LANES - HEAD_DIM // 2, axis=1),
                                pltpu.roll(ch, HEAD_DIM // 2, axis=1))
                ch = ch * cos + rot * sin
            if scale != 1.0:
                ch = ch * scale
            o_ref[rows, c * LANES:(c + 1) * LANES] = ch.astype(o_ref.dtype)


def _attend(q_ref, k2, v2, valid, sink_ref, o_ref):
    tq = q_ref.shape[0]
    nk = k2.shape[0]
    lo = lax.broadcasted_iota(jnp.int32, (nk, LANES), 1) < HEAD_DIM
    lo_q = lax.broadcasted_iota(jnp.int32, (tq, LANES), 1) < HEAD_DIM
    group = N_HEADS // N_KV_HEADS
    for m in range(KV_DIM // LANES):
        kc = k2[:, m * LANES:(m + 1) * LANES]
        vc = v2[:, m * LANES:(m + 1) * LANES]
        kr = pltpu.roll(kc, HEAD_DIM, axis=1)
        vr = pltpu.roll(vc, HEAD_DIM, axis=1)
        for hh in range(2):
            kv_head = 2 * m + hh
            k_lo, k_hi = (kc, kr) if hh == 0 else (kr, kc)
            v_lo, v_hi = (vc, vr) if hh == 0 else (vr, vc)
            kbd = jnp.concatenate([jnp.where(lo, k_lo, 0.0), jnp.where(lo, 0.0, k_hi)], axis=0).astype(BF)
            vbd = jnp.concatenate([jnp.where(lo, v_lo, 0.0), jnp.where(lo, 0.0, v_hi)], axis=0).astype(BF)
            pairs = [kv_head * (group // 2) + pp for pp in range(group // 2)]
            qs = _bf(jnp.concatenate([q_ref[:, p * LANES:(p + 1) * LANES] for p in pairs], axis=0))
            s_all = lax.dot_general(qs, kbd, (((1,), (1,)), ((), ())), preferred_element_type=F32)
            es, inv = [], []
            for pp, p in enumerate(pairs):
                s = s_all[pp * tq:(pp + 1) * tq, :]
                e_pair, d_pair = [], []
                for half in range(2):
                    sh = jnp.where(valid, s[:, half * nk:(half + 1) * nk], NEG)
                    sink = sink_ref[2 * p + half]
                    mx = jnp.maximum(jnp.max(sh, axis=-1, keepdims=True), sink)
                    e = jnp.exp(sh - mx)
                    e_pair.append(e)
                    d_pair.append(jnp.sum(e, axis=-1, keepdims=True) + jnp.exp(sink - mx))
                es.append(jnp.concatenate(e_pair, axis=1))
                inv.append(jnp.where(lo_q, 1.0 / d_pair[0], 1.0 / d_pair[1]))
            o_all = jnp.dot(_bf(jnp.concatenate(es, axis=0)), vbd, preferred_element_type=F32)
            for pp, p in enumerate(pairs):
                o = o_all[pp * tq:(pp + 1) * tq, :] * inv[pp]
                o_ref[:, p * LANES:(p + 1) * LANES] = o.astype(o_ref.dtype)


def _attn_prompt_body(sink_ref, q_ref, kvp_ref, kvc_ref, o_ref):
    n = pl.program_id(1)
    kvp = kvp_ref[...]
    kvc = kvc_ref[...]
    k2 = jnp.concatenate([kvp[:, :KV_DIM], kvc[:, :KV_DIM]], axis=0)
    v2 = jnp.concatenate([kvp[:, KV_DIM:], kvc[:, KV_DIM:]], axis=0)
    i = lax.broadcasted_iota(jnp.int32, (WINDOW, 2 * WINDOW), 0)
    j = lax.broadcasted_iota(jnp.int32, (WINDOW, 2 * WINDOW), 1)
    valid = (j >= i) & (j <= i + WINDOW) & ((j >= WINDOW) | (n > 0))
    _attend(q_ref, k2, v2, valid, sink_ref, o_ref)


def _attn_sample_body(sink_ref, q_ref, kvn_ref, ck_ref, cv_ref, o_ref, ks_ref, vs_ref):
    t = q_ref.shape[0]
    ck = ck_ref[...]
    cv = cv_ref[...]
    kvn = kvn_ref[...]
    pad = jnp.zeros((WINDOW - t, KV_DIM), F32)
    k2 = jnp.concatenate([ck, kvn[:, :KV_DIM], pad], axis=0)
    v2 = jnp.concatenate([cv, kvn[:, KV_DIM:], pad], axis=0)
    i = lax.broadcasted_iota(jnp.int32, (t, 2 * WINDOW), 0)
    j = lax.broadcasted_iota(jnp.int32, (t, 2 * WINDOW), 1)
    valid = (j >= i) & (j <= i + WINDOW)
    _attend(q_ref, k2, v2, valid, sink_ref, o_ref)
    ks_ref[0:WINDOW - t, :] = ck[t:, :]
    ks_ref[WINDOW - t:, :] = kvn[:, :KV_DIM]
    vs_ref[0:WINDOW - t, :] = cv[t:, :]
    vs_ref[WINDOW - t:, :] = kvn[:, KV_DIM:]


def _conv3_carry(u, w_ref, prev):
    ext = jnp.concatenate([prev, u], axis=0)
    x1 = pltpu.roll(ext, 1, axis=0)[SUBLANES:, :]
    x2 = pltpu.roll(ext, 2, axis=0)[SUBLANES:, :]
    return w_ref[0:1, :] * x2 + w_ref[1:2, :] * x1 + w_ref[2:3, :] * u


def _conv3_segments(u, w_ref, e, seg):
    t = lax.broadcasted_iota(jnp.int32, u.shape, 0) & (seg - 1)
    x1 = jnp.where(t >= 1, pltpu.roll(u, 1, axis=0), pltpu.roll(e, u.shape[0] - 1, axis=0))
    x2 = jnp.where(t >= 2, pltpu.roll(u, 2, axis=0), e)
    return w_ref[0:1, :] * x2 + w_ref[1:2, :] * x1 + w_ref[2:3, :] * u


def _carry_in(carry_ref, tiles_per_seq):
    i = pl.program_id(0)
    j = pl.program_id(1)

    @pl.when(i % tiles_per_seq == 0)
    def _():
        carry_ref[j] = jnp.zeros(carry_ref.shape[1:], F32)

    return carry_ref[j]


def _chunks_with_riders(h_ref, hs_ref):
    chunks = _row_chunks(h_ref.shape[0])
    out = []
    for c, rows in enumerate(chunks):
        lhs = h_ref[rows, :]
        if c == RIDER_CHUNK:
            lhs = jnp.concatenate([lhs, hs_ref[...]], axis=0)
        out.append((rows, lhs, min(h_ref.shape[0], ROW_CHUNK)))
    return out


def _inconv_body(h_ref, hs_ref, wb_ref, wc_ref, wx_ref, cw_ref, e_ref, s_ref, tail_ref, ss_ref, us_ref,
                 carry_ref, wbs_ref, wcs_ref, wxs_ref, *, tiles_per_seq, seg):
    j = pl.program_id(1)
    wb = _staged(wb_ref, wbs_ref)
    wc = _staged(wc_ref, wcs_ref)
    wx = _staged(wx_ref, wxs_ref)
    prev = _carry_in(carry_ref, tiles_per_seq)
    for rows, lhs, n in _chunks_with_riders(h_ref, hs_ref):
        u_all = _dot(lhs, wc[...]) * _dot(lhs, wx[...])
        b_all = _dot(lhs, wb[...])
        u = u_all[:n, :]
        y = _conv3_carry(u, cw_ref, prev)
        prev = u[n - SUBLANES:, :]
        s_ref[rows, :] = (b_all[:n, :] * y).astype(s_ref.dtype)
        if u_all.shape[0] > n:
            us = u_all[n:, :]
            us_ref[...] = us
            ss_ref[...] = (b_all[n:, :] * _conv3_segments(us, cw_ref, e_ref[...], seg)).astype(ss_ref.dtype)
    carry_ref[j] = prev
    tail_ref[...] = prev


def _merged_body(h_ref, a_ref, s_ref, hs_ref, as_ref, ss_ref, wga_ref, wgc_ref, wab_ref, wcb_ref, o_ref, os_ref):
    chunks = _row_chunks(h_ref.shape[0])
    n = min(h_ref.shape[0], ROW_CHUNK)
    for c, rows in enumerate(chunks):
        h, a, s = h_ref[rows, :], a_ref[rows, :], s_ref[rows, :]
        last = c == RIDER_CHUNK
        if last:
            h = jnp.concatenate([h, hs_ref[...]], axis=0)
            a = jnp.concatenate([a, _bf(as_ref[...])], axis=0)
            s = jnp.concatenate([s, ss_ref[...]], axis=0)
        attn_br = jax.nn.sigmoid(_dot(h, wga_ref[...])) * _dot(a, wab_ref[...])
        conv_br = jax.nn.sigmoid(_dot(h, wgc_ref[...])) * _dot(s, wcb_ref[...])
        m = attn_br + conv_br
        o_ref[rows, :] = m[:n, :].astype(o_ref.dtype)
        if last:
            os_ref[...] = m[n:, :].astype(os_ref.dtype)


def _mm_resid_body(a_ref, as_ref, w_ref, x_ref, xs_ref, g_ref, gs_ref, o_ref, os_ref):
    for rows, lhs, n in _chunks_with_riders(a_ref, as_ref):
        y = _dot(lhs, w_ref[...])
        o_ref[rows, :] = ALPHA * x_ref[rows, :] + _rows_of(g_ref, rows) * y[:n, :]
        if y.shape[0] > n:
            os_ref[...] = ALPHA * xs_ref[...] + gs_ref[...] * y[n:, :]


def _with_casts(body, n_in, n_out, n_casts):
    def wrapped(*refs):
        ins, refs_ = refs[:n_in], refs[n_in:]
        srcs, refs_ = refs_[:n_casts], refs_[n_casts:]
        outs, refs_ = refs_[:n_out], refs_[n_out:]
        dsts, scratch = refs_[:n_casts], refs_[n_casts:]
        for src, dst in zip(srcs, dsts):
            dst[...] = src[...].astype(BF)
        body(*ins, *outs, *scratch)
    return wrapped


def _layernorm(r, g, b):
    mu = jnp.mean(r, axis=-1, keepdims=True)
    d = r - mu
    var = jnp.mean(d * d, axis=-1, keepdims=True)
    return d * lax.rsqrt(var + LN_EPS) * g + b


def _ln_mod_body(r_ref, g_ref, b_ref, sc_ref, sh_ref, x_ref, h_ref):
    y = _layernorm(r_ref[...], g_ref[...], b_ref[...])
    x_ref[...] = y
    h_ref[...] = (y * (1.0 + sc_ref[...]) + sh_ref[...]).astype(h_ref.dtype)


def _ln_body(r_ref, g_ref, b_ref, x_ref):
    x_ref[...] = _layernorm(r_ref[...], g_ref[...], b_ref[...])


def _silu_mul(g, v):
    return g * jax.nn.sigmoid(g) * v


def _ffnup_body(h_ref, hs_ref, wg_ref, wv_ref, cwg_ref, cwv_ref, eg_ref, ev_ref, wd_ref,
                act_ref, tg_ref, tv_ref, acts_ref, ugs_ref, uvs_ref, wdb_ref,
                cg_ref, cv_ref, wgs_ref, wvs_ref, *, tiles_per_seq, seg):
    j = pl.program_id(1)
    wdb_ref[...] = wd_ref[...].astype(BF)
    wg = _staged(wg_ref, wgs_ref)
    wv = _staged(wv_ref, wvs_ref)
    prev_g = _carry_in(cg_ref, tiles_per_seq)
    prev_v = _carry_in(cv_ref, tiles_per_seq)
    for rows, lhs, n in _chunks_with_riders(h_ref, hs_ref):
        ug_all = _dot(lhs, wg[...])
        uv_all = _dot(lhs, wv[...])
        ug = ug_all[:n, :]
        uv = uv_all[:n, :]
        yg = _conv3_carry(ug, cwg_ref, prev_g)
        yv = _conv3_carry(uv, cwv_ref, prev_v)
        prev_g = ug[n - SUBLANES:, :]
        prev_v = uv[n - SUBLANES:, :]
        act_ref[rows, :] = _silu_mul(yg, yv).astype(act_ref.dtype)
        if ug_all.shape[0] > n:
            ugs = ug_all[n:, :]
            uvs = uv_all[n:, :]
            ugs_ref[...] = ugs
            uvs_ref[...] = uvs
            ygs = _conv3_segments(ugs, cwg_ref, eg_ref[...], seg)
            yvs = _conv3_segments(uvs, cwv_ref, ev_ref[...], seg)
            acts_ref[...] = _silu_mul(ygs, yvs).astype(acts_ref.dtype)
    cg_ref[j] = prev_g
    cv_ref[j] = prev_v
    tg_ref[...] = prev_g
    tv_ref[...] = prev_v


class _Group:
    def __init__(self, batch, seq, tm, tm_big, ada_rows):
        self.batch, self.seq, self.tm, self.tm_big = batch, seq, tm, tm_big
        self.m = batch * seq
        self.per_row = seq < tm
        self.ada = ada_rows

    def param_spec(self, chunk, tm, tn):
        cb = chunk * (D_MODEL // tn)
        if self.per_row:
            return pl.BlockSpec((tm, tn), lambda i, j: (i, cb + j))
        tps = self.seq // tm
        return pl.BlockSpec((None, 1, tn), lambda i, j: (i // tps, 0, cb + j))


def _resident_rows(tm, width):
    return pl.BlockSpec((tm, width), lambda i, j: (i, 0), pipeline_mode=pl.Buffered(1))


def _ada(c_all, w_ada, b_ada):
    tn = 512
    n = w_ada.shape[-1]
    return pl.pallas_call(
        _ada_body,
        out_shape=jax.ShapeDtypeStruct((DEPTH, ADA_ROWS, n), F32),
        grid=(DEPTH, n // tn),
        in_specs=[pl.BlockSpec((ADA_ROWS, D_MODEL), lambda l, j: (0, 0)),
                  pl.BlockSpec((None, D_MODEL, tn), lambda l, j: (l, 0, j)),
                  pl.BlockSpec((None, 1, tn), lambda l, j: (l, 0, j))],
        out_specs=pl.BlockSpec((None, ADA_ROWS, tn), lambda l, j: (l, 0, j)),
        compiler_params=_cp(2), name="ada",
    )(c_all, w_ada, b_ada.reshape(DEPTH, 1, n))


def _modulate(grp, x):
    tr = min(grp.tm, 512)
    return pl.pallas_call(
        _mod_body,
        out_shape=jax.ShapeDtypeStruct((grp.m, D_MODEL), BF),
        grid=(grp.m // tr,),
        in_specs=[pl.BlockSpec((tr, D_MODEL), lambda i: (i, 0)),
                  _row_param_spec(grp, SC1, tr), _row_param_spec(grp, SH1, tr)],
        out_specs=pl.BlockSpec((tr, D_MODEL), lambda i: (i, 0)),
        compiler_params=_cp(1), name="modulate",
    )(x, grp.ada, grp.ada)


def _row_param_spec(grp, chunk, tr):
    if grp.per_row:
        return pl.BlockSpec((tr, D_MODEL), lambda i: (i, chunk))
    tps = grp.seq // tr
    return pl.BlockSpec((None, 1, D_MODEL), lambda i: (i // tps, 0, chunk))


def _proj_rope(grp, h, w_in, l, col0, n_cols, tm, tn, cos, sin, n_rope, scale, out_dtype, name, cast_of=None):
    cb = col0 // tn
    t_tiles = cos.shape[0] // tm
    nj = n_cols // tn
    casts = [] if cast_of is None else [_row_cast(cast_of, l, (grp.m // tm) * nj, lambda i, j: i * nj + j)]
    out = pl.pallas_call(
        _with_casts(functools.partial(_proj_rope_body, n_rope=n_rope, scale=scale), 4, 1, len(casts)),
        out_shape=(jax.ShapeDtypeStruct((grp.m, n_cols), out_dtype), *[c[3] for c in casts]),
        grid=(grp.m // tm, nj),
        in_specs=[pl.BlockSpec((tm, D_MODEL), lambda i, j: (i, 0)),
                  pl.BlockSpec((None, D_MODEL, tn), lambda i, j: (l, 0, cb + j)),
                  pl.BlockSpec((tm, LANES), lambda i, j: (i % t_tiles, 0)),
                  pl.BlockSpec((tm, LANES), lambda i, j: (i % t_tiles, 0)),
                  *[c[1] for c in casts]],
        out_specs=(pl.BlockSpec((tm, tn), lambda i, j: (i, j)), *[c[2] for c in casts]),
        scratch_shapes=[pltpu.VMEM((D_MODEL, tn), BF)],
        compiler_params=_cp(2), name=name,
    )(h, w_in, cos, sin, *[c[0] for c in casts])
    return out[0] if cast_of is None else out


def _attn_prompt(sinks, q, kv, batch, seq, l, w_in, w_attn_br, w_conv_br):
    nb = seq // WINDOW
    step_of = lambda b, n: b * nb + n
    casts = [_col_cast(w_in, l, COL_GA, 2 * D_MODEL, batch * nb, step_of),
             _row_cast(w_attn_br, l, batch * nb, step_of), _row_cast(w_conv_br, l, batch * nb, step_of)]
    return pl.pallas_call(
        _with_casts(_attn_prompt_body, 4, 1, len(casts)),
        out_shape=(jax.ShapeDtypeStruct(q.shape, BF), *[c[3] for c in casts]),
        grid=(batch, nb),
        in_specs=[pl.BlockSpec(memory_space=pltpu.SMEM),
                  pl.BlockSpec((WINDOW, ATTN_DIM), lambda b, n: (b * nb + n, 0)),
                  pl.BlockSpec((WINDOW, 2 * KV_DIM), lambda b, n: (b * nb + jnp.maximum(n - 1, 0), 0)),
                  pl.BlockSpec((WINDOW, 2 * KV_DIM), lambda b, n: (b * nb + n, 0)),
                  *[c[1] for c in casts]],
        out_specs=(pl.BlockSpec((WINDOW, ATTN_DIM), lambda b, n: (b * nb + n, 0)), *[c[2] for c in casts]),
        compiler_params=_cp(2), name="attn_prompt",
    )(sinks, q, kv, kv, *[c[0] for c in casts])


def _attn_sample(sinks, q, kvn, cache_k, cache_v, l, batch, seq):
    return pl.pallas_call(
        _attn_sample_body,
        out_shape=(jax.ShapeDtypeStruct(q.shape, F32),
                   jax.ShapeDtypeStruct((batch, WINDOW, KV_DIM), F32),
                   jax.ShapeDtypeStruct((batch, WINDOW, KV_DIM), F32)),
        grid=(batch,),
        in_specs=[pl.BlockSpec(memory_space=pltpu.SMEM),
                  pl.BlockSpec((seq, ATTN_DIM), lambda b: (b, 0)),
                  pl.BlockSpec((seq, 2 * KV_DIM), lambda b: (b, 0)),
                  pl.BlockSpec((None, None, WINDOW, KV_DIM), lambda b: (l, b, 0, 0)),
                  pl.BlockSpec((None, None, WINDOW, KV_DIM), lambda b: (l, b, 0, 0))],
        out_specs=(pl.BlockSpec((seq, ATTN_DIM), lambda b: (b, 0)),
                   pl.BlockSpec((None, WINDOW, KV_DIM), lambda b: (b, 0, 0)),
                   pl.BlockSpec((None, WINDOW, KV_DIM), lambda b: (b, 0, 0))),
        compiler_params=_cp(1), name="attn_sample",
    )(sinks, q, kvn, cache_k, cache_v)


def _rider_rows(gp, gs, tm):
    rs = gs.m // (gp.m // tm)
    assert rs * (gp.m // tm) == gs.m and rs % gs.seq == 0 and rs % (2 * SUBLANES) == 0
    return rs


def _in_conv(gp, gs, h, hs, w_in, conv_w, l, past_rows):
    tm, tn = gp.tm_big, 256
    nj = CONV_DIM // tn
    rs = _rider_rows(gp, gs, tm)
    w_spec = lambda col0: pl.BlockSpec((None, D_MODEL, tn), lambda i, j: (l, 0, col0 // tn + j))
    tile = lambda rows: pl.BlockSpec((rows, tn), lambda i, j: (i, j))
    return pl.pallas_call(
        functools.partial(_inconv_body, tiles_per_seq=gp.seq // tm, seg=gs.seq),
        out_shape=(jax.ShapeDtypeStruct((gp.m, CONV_DIM), BF),
                   jax.ShapeDtypeStruct((gp.m // tm, SUBLANES, CONV_DIM), F32),
                   jax.ShapeDtypeStruct((gs.m, CONV_DIM), BF),
                   jax.ShapeDtypeStruct((gs.m, CONV_DIM), F32)),
        grid=(gp.m // tm, nj),
        in_specs=[_resident_rows(tm, D_MODEL), pl.BlockSpec((rs, D_MODEL), lambda i, j: (i, 0)),
                  w_spec(COL_CB), w_spec(COL_CC), w_spec(COL_CX),
                  pl.BlockSpec((None, 3, tn), lambda i, j: (l, 0, j)), tile(rs)],
        out_specs=(tile(tm), pl.BlockSpec((None, SUBLANES, tn), lambda i, j: (i, 0, j)), tile(rs), tile(rs)),
        scratch_shapes=[pltpu.VMEM((nj, SUBLANES, tn), F32)] + [pltpu.VMEM((D_MODEL, tn), BF)] * 3,
        compiler_params=_cp(2), name="in_conv",
    )(h, hs, w_in, w_in, w_in, conv_w, past_rows)


def _row_cast(w, l, steps, step_of):
    _, r, c = w.shape
    rows = r // steps
    assert rows * steps == r and rows % (2 * SUBLANES) == 0
    return (w, pl.BlockSpec((None, rows, c), lambda *g: (l, step_of(*g), 0)),
            pl.BlockSpec((rows, c), lambda *g: (step_of(*g), 0)), jax.ShapeDtypeStruct((r, c), BF))


def _col_cast(w, l, col0, n_cols, steps, step_of):
    _, r, _ = w.shape
    cols = n_cols // steps
    assert cols * steps == n_cols and cols % LANES == 0 and col0 % cols == 0
    return (w, pl.BlockSpec((None, r, cols), lambda *g: (l, 0, col0 // cols + step_of(*g))),
            pl.BlockSpec((r, cols), lambda *g: (0, step_of(*g))), jax.ShapeDtypeStruct((r, n_cols), BF))


def _merged(gp, gs, h, attn, s, hs, attn_s, s_s, w_gates, w_ab, w_cb):
    tm, tn = gp.tm, 256
    rs = _rider_rows(gp, gs, tm)
    row = lambda rows, width: pl.BlockSpec((rows, width), lambda i, j: (i, 0))
    w2d = lambda k, col0: pl.BlockSpec((k, tn), lambda i, j: (0, col0 // tn + j))
    tile = lambda rows: pl.BlockSpec((rows, tn), lambda i, j: (i, j))
    return pl.pallas_call(
        _merged_body,
        out_shape=(jax.ShapeDtypeStruct((gp.m, D_MODEL), BF), jax.ShapeDtypeStruct((gs.m, D_MODEL), BF)),
        grid=(gp.m // tm, D_MODEL // tn),
        in_specs=[row(tm, D_MODEL), row(tm, ATTN_DIM), row(tm, CONV_DIM),
                  row(rs, D_MODEL), row(rs, ATTN_DIM), row(rs, CONV_DIM),
                  w2d(D_MODEL, 0), w2d(D_MODEL, COL_GC - COL_GA), w2d(ATTN_DIM, 0), w2d(CONV_DIM, 0)],
        out_specs=(tile(tm), tile(rs)),
        compiler_params=_cp(2), name="merged",
    )(h, attn, s, hs, attn_s, s_s, w_gates, w_gates, w_ab, w_cb)


def _mm_resid(gp, gs, a, a_s, w_bf16, x, x_s, gate_chunk, tm, tn, name):
    k = a.shape[1]
    rs = _rider_rows(gp, gs, tm)
    tile = lambda rows: pl.BlockSpec((rows, tn), lambda i, j: (i, j))
    return pl.pallas_call(
        _mm_resid_body,
        out_shape=(jax.ShapeDtypeStruct((gp.m, D_MODEL), F32), jax.ShapeDtypeStruct((gs.m, D_MODEL), F32)),
        grid=(gp.m // tm, D_MODEL // tn),
        in_specs=[pl.BlockSpec((tm, k), lambda i, j: (i, 0)), pl.BlockSpec((rs, k), lambda i, j: (i, 0)),
                  pl.BlockSpec((k, tn), lambda i, j: (0, j)), tile(tm), tile(rs),
                  gp.param_spec(gate_chunk, tm, tn), gs.param_spec(gate_chunk, rs, tn)],
        out_specs=(tile(tm), tile(rs)),
        compiler_params=_cp(2), name=name,
    )(a, a_s, w_bf16, x, x_s, gp.ada, gs.ada)


def _ln(grp, r, g, b, l, mod_grp, mod_chunks, name):
    tr = min(grp.m, 512)
    row = pl.BlockSpec((tr, D_MODEL), lambda i: (i, 0))
    vec = pl.BlockSpec((None, 1, D_MODEL), lambda i: (l, 0, 0))
    g3 = g.reshape(DEPTH, 1, D_MODEL)
    b3 = b.reshape(DEPTH, 1, D_MODEL)
    if mod_chunks is None:
        return pl.pallas_call(
            _ln_body, out_shape=jax.ShapeDtypeStruct((grp.m, D_MODEL), F32), grid=(grp.m // tr,),
            in_specs=[row, vec, vec], out_specs=row, compiler_params=_cp(1), name=name,
        )(r, g3, b3), None
    sc, sh = mod_chunks
    return pl.pallas_call(
        _ln_mod_body,
        out_shape=(jax.ShapeDtypeStruct((grp.m, D_MODEL), F32), jax.ShapeDtypeStruct((grp.m, D_MODEL), BF)),
        grid=(grp.m // tr,),
        in_specs=[row, vec, vec, _row_param_spec(mod_grp, sc, tr), _row_param_spec(mod_grp, sh, tr)],
        out_specs=(row, row), compiler_params=_cp(1), name=name,
    )(r, g3, b3, mod_grp.ada, mod_grp.ada)


def _ffn_up(gp, gs, h, hs, w_up, ffn_conv_w, w_down, l, past_g, past_v):
    tm, tn = gp.tm_big, 256
    nj = D_FF // tn
    rs = _rider_rows(gp, gs, tm)
    steps = (gp.m // tm) * nj
    wd_rows = D_FF // steps
    assert wd_rows * steps == D_FF and wd_rows % (2 * SUBLANES) == 0
    tile = lambda rows: pl.BlockSpec((rows, tn), lambda i, j: (i, j))
    tail_shape = jax.ShapeDtypeStruct((gp.m // tm, SUBLANES, D_FF), F32)
    tail_spec = pl.BlockSpec((None, SUBLANES, tn), lambda i, j: (i, 0, j))
    full_s = jax.ShapeDtypeStruct((gs.m, D_FF), F32)
    return pl.pallas_call(
        functools.partial(_ffnup_body, tiles_per_seq=gp.seq // tm, seg=gs.seq),
        out_shape=(jax.ShapeDtypeStruct((gp.m, D_FF), BF), tail_shape, tail_shape,
                   jax.ShapeDtypeStruct((gs.m, D_FF), BF), full_s, full_s,
                   jax.ShapeDtypeStruct((D_FF, D_MODEL), BF)),
        grid=(gp.m // tm, nj),
        in_specs=[_resident_rows(tm, D_MODEL), pl.BlockSpec((rs, D_MODEL), lambda i, j: (i, 0)),
                  pl.BlockSpec((None, D_MODEL, tn), lambda i, j: (l, 0, j)),
                  pl.BlockSpec((None, D_MODEL, tn), lambda i, j: (l, 0, nj + j)),
                  pl.BlockSpec((None, 3, tn), lambda i, j: (l, 0, j)),
                  pl.BlockSpec((None, 3, tn), lambda i, j: (l, 0, nj + j)),
                  tile(rs), tile(rs),
                  pl.BlockSpec((None, wd_rows, D_MODEL), lambda i, j: (l, i * nj + j, 0))],
        out_specs=(tile(tm), tail_spec, tail_spec, tile(rs), tile(rs), tile(rs),
                   pl.BlockSpec((wd_rows, D_MODEL), lambda i, j: (i * nj + j, 0))),
        scratch_shapes=[pltpu.VMEM((nj, SUBLANES, tn), F32), pltpu.VMEM((nj, SUBLANES, tn), F32),
                        pltpu.VMEM((D_MODEL, tn), BF), pltpu.VMEM((D_MODEL, tn), BF)],
        compiler_params=_cp(2), name="ffn_up",
    )(h, hs, w_up, w_up, ffn_conv_w, ffn_conv_w, past_g, past_v, w_down)


def _rope_tables(pos):
    inv = ROPE_THETA ** (-jnp.arange(0, HEAD_DIM, 2, dtype=F32) / HEAD_DIM)
    ang = pos.astype(F32)[:, None] * inv[None, :]
    cos, sin = jnp.cos(ang), jnp.sin(ang)
    return jnp.concatenate([cos] * 4, axis=-1), jnp.concatenate([-sin, sin, -sin, sin], axis=-1)


def _expand_past(past, seq):
    batch, _, n = past.shape
    return jnp.concatenate([past, jnp.zeros((batch, seq - 2, n), past.dtype)], axis=1).reshape(batch * seq, n)


def _last_rows(x, batch, seq, n_rows):
    return x.reshape(batch, seq, x.shape[-1])[:, seq - n_rows:, :]


def _qkv(grp, l, h, rope, w_in, cast_of=None):
    cos, sin = rope
    q = _proj_rope(grp, h, w_in, l, COL_Q, ATTN_DIM, grp.tm_big, 256, cos, sin, 256 // LANES, HEAD_DIM ** -0.5,
                   F32 if grp.per_row else BF, "proj_q", cast_of)
    kv = _proj_rope(grp, h, w_in, l, COL_KV, 2 * KV_DIM, grp.tm, 2 * KV_DIM, cos, sin, KV_DIM // LANES, 1.0,
                    F32, "proj_kv")
    return q, kv


def _layer(gp, gs, next_gp, next_gs, l, xp, xs, hp, hs, rope_p, rope_s, weights, sinks, past):
    (w_in, conv_w, w_attn_br, w_conv_br, w_out, ln1_g, ln1_b, w_up, ffn_conv_w, w_down, ln2_g, ln2_b) = weights
    cache_k, cache_v, state_conv, state_ffn = past
    tps = gp.seq // gp.tm_big
    state_rows = lambda tails: tails[tps - 1::tps, SUBLANES - 2:, :]

    q_s, kv_s = _qkv(gs, l, hs, rope_s, w_in)
    attn_s, k_s, v_s = _attn_sample(sinks, q_s, kv_s, cache_k, cache_v, l, gs.batch, gs.seq)
    (q_p, w_out_b), kv_p = _qkv(gp, l, hp, rope_p, w_in, cast_of=w_out)
    attn_p, w_gates, w_ab, w_cb = _attn_prompt(sinks, q_p, kv_p, gp.batch, gp.seq, l, w_in, w_attn_br, w_conv_br)
    kv_tail = _last_rows(kv_p, gp.batch, gp.seq, WINDOW)
    k_p, v_p = kv_tail[..., :KV_DIM], kv_tail[..., KV_DIM:]

    s_p, tails, s_s, u_s = _in_conv(gp, gs, hp, hs, w_in, conv_w, l, _expand_past(state_conv[l], gs.seq))
    conv_p = state_rows(tails)
    conv_s = _last_rows(u_s, gs.batch, gs.seq, 2)

    merged_p, merged_s = _merged(gp, gs, hp, attn_p, s_p, hs, attn_s, s_s, w_gates, w_ab, w_cb)
    r1_p, r1_s = _mm_resid(gp, gs, merged_p, merged_s, w_out_b, xp, xs, G1, gp.tm, 1024, "out_proj")
    x1_s, h2_s = _ln(gs, r1_s, ln1_g, ln1_b, l, gs, (SC2, SH2), "ln1")
    x1_p, h2_p = _ln(gp, r1_p, ln1_g, ln1_b, l, gp, (SC2, SH2), "ln1")

    eg = _expand_past(state_ffn[l][..., :D_FF], gs.seq)
    ev = _expand_past(state_ffn[l][..., D_FF:], gs.seq)
    act_p, tg, tv, act_s, ug_s, uv_s, w_down_b = _ffn_up(gp, gs, h2_p, h2_s, w_up, ffn_conv_w, w_down, l, eg, ev)
    ffn_p = state_rows(jnp.concatenate([tg, tv], axis=-1))
    ffn_s = _last_rows(jnp.concatenate([ug_s, uv_s], axis=-1), gs.batch, gs.seq, 2)

    r2_p, r2_s = _mm_resid(gp, gs, act_p, act_s, w_down_b, x1_p, x1_s, G2, 512, 512, "ffn_down")
    mod = None if next_gp is None else (SC1, SH1)
    xs, hs = _ln(gs, r2_s, ln2_g, ln2_b, l, next_gs, mod, "ln2")
    xp, hp = _ln(gp, r2_p, ln2_g, ln2_b, l, next_gp, mod, "ln2")

    heads = lambda k, grp: k.reshape(grp.batch, WINDOW, N_KV_HEADS, HEAD_DIM)
    return (xp, xs, hp, hs, (heads(k_p, gp), heads(v_p, gp), conv_p, ffn_p),
            (heads(k_s, gs), heads(v_s, gs), conv_s, ffn_s))


def kernel(x_prompt, x_sample, cache_attn_k, cache_attn_v, state_conv, state_ffn_conv, c_prompt, c_sample, w_ada, b_ada, w_in, attn_sinks, conv_w, w_attn_br, w_conv_br, w_out, ln1_g, ln1_b, w_up, ffn_conv_w, w_down, ln2_g, ln2_b):
    bp, tp, _ = x_prompt.shape
    bs, ts, _ = x_sample.shape
    assert bp <= ADA_SAMPLE_ROW0 and ADA_SAMPLE_ROW0 + bs == ADA_ROWS

    c_all = jnp.concatenate([c_prompt, jnp.zeros((ADA_SAMPLE_ROW0 - bp, D_MODEL), F32), c_sample], axis=0)
    ada = _ada(c_all, w_ada, b_ada)

    weights = (w_in, conv_w, w_attn_br, w_conv_br, w_out, ln1_g, ln1_b, w_up, ffn_conv_w, w_down, ln2_g, ln2_b)
    rope_p = _rope_tables(jnp.arange(tp, dtype=jnp.int32))
    rope_s = tuple(jnp.tile(t, (bs, 1)) for t in _rope_tables(PAST_LEN + jnp.arange(ts, dtype=jnp.int32)))
    cache_k = cache_attn_k.reshape(DEPTH, bs, WINDOW, KV_DIM)
    cache_v = cache_attn_v.reshape(DEPTH, bs, WINDOW, KV_DIM)
    past = (cache_k, cache_v, state_conv, state_ffn_conv)

    xp = x_prompt.reshape(bp * tp, D_MODEL)
    xs = x_sample.reshape(bs * ts, D_MODEL)
    gps = [_Group(bp, tp, 1024, 2048, ada[l, :bp].reshape(bp, 1, -1)) for l in range(DEPTH)] + [None]
    gss = [_Group(bs, ts, bs * ts, bs * ts, jnp.repeat(ada[l, ADA_SAMPLE_ROW0:], ts, axis=0))
           for l in range(DEPTH)] + [None]
    hp = _modulate(gps[0], xp)
    hs = _modulate(gss[0], xs)
    outs_p, outs_s = [], []
    for l in range(DEPTH):
        xp, xs, hp, hs, st_p, st_s = _layer(gps[l], gss[l], gps[l + 1], gss[l + 1], l, xp, xs, hp, hs,
                                            rope_p, rope_s, weights, attn_sinks[l], past)
        outs_p.append(st_p)
        outs_s.append(st_s)
    stack = lambda outs, k: jnp.stack([o[k] for o in outs])
    return (xp.reshape(bp, tp, D_MODEL), xs.reshape(bs, ts, D_MODEL),
            stack(outs_p, 0), stack(outs_p, 1), stack(outs_p, 2), stack(outs_p, 3),
            stack(outs_s, 0), stack(outs_s, 1), stack(outs_s, 2), stack(outs_s, 3))
```

```python
import functools

import jax
import jax.numpy as jnp
from jax import lax
from jax.experimental import pallas as pl
from jax.experimental.pallas import tpu as pltpu

D_MODEL = 4096
DEPTH = 2
HEAD_DIM = 64
N_HEADS = 32
N_KV_HEADS = 4
ATTN_DIM = N_HEADS * HEAD_DIM
KV_DIM = N_KV_HEADS * HEAD_DIM
WINDOW = 128
ROPE_THETA = 10000.0
CONV_DIM = D_MODEL // 2
D_FF = 11008
PAST_LEN = 16384
LN_EPS = 1e-5
ALPHA = (2 * DEPTH) ** 0.25

COL_Q = 0
COL_KV = ATTN_DIM
COL_CB = ATTN_DIM + 2 * KV_DIM
COL_CC = COL_CB + CONV_DIM
COL_CX = COL_CC + CONV_DIM
COL_GA = COL_CX + CONV_DIM
COL_GC = COL_GA + D_MODEL

SH1, SC1, G1, SH2, SC2, G2 = range(6)

LANES = 128
SUBLANES = 8
ADA_ROWS = 40
ADA_SAMPLE_ROW0 = 8
ROW_CHUNK = 512
FFN_CHUNK = 1024
ROPE_CHUNK = 256
RIDER_CHUNK = 0
VMEM_LIMIT = 56 * 2**20
NEG = -1e30

BF = jnp.bfloat16
F32 = jnp.float32


def _cp(n_axes):
    return pltpu.CompilerParams(dimension_semantics=("arbitrary",) * n_axes, vmem_limit_bytes=VMEM_LIMIT)


def _bf(x):
    return x if x.dtype == BF else x.astype(BF)


def _dot(a, b):
    return jnp.dot(_bf(a), _bf(b), preferred_element_type=F32)


def _row_chunks(tm, chunk=None):
    chunk = min(tm, chunk or ROW_CHUNK)
    assert tm % chunk == 0
    return [pl.ds(r, chunk) for r in range(0, tm, chunk)]


def _staged(w_ref, stage_ref):
    stage_ref[...] = w_ref[...].astype(BF)
    return stage_ref


def _rows_of(ref, rows):
    return ref[...] if ref.shape[0] == 1 else ref[rows, :]


def _ada_body(c_ref, w_ref, b_ref, o_ref):
    c = c_ref[...]
    o_ref[...] = _dot(c * jax.nn.sigmoid(c), w_ref[...]) + b_ref[...]


def _mod_body(x_ref, sc_ref, sh_ref, o_ref):
    o_ref[...] = (x_ref[...] * (1.0 + sc_ref[...]) + sh_ref[...]).astype(o_ref.dtype)


def _proj_rope_body(h_ref, w_ref, cos_ref, sin_ref, o_ref, ws_ref, *, n_rope, scale):
    tn = o_ref.shape[1]
    w = _staged(w_ref, ws_ref)
    lane = lax.broadcasted_iota(jnp.int32, (min(h_ref.shape[0], ROPE_CHUNK), LANES), 1)
    first_half = (lane & (HEAD_DIM - 1)) < HEAD_DIM // 2
    for rows in _row_chunks(h_ref.shape[0], ROPE_CHUNK):
        acc = _dot(h_ref[rows, :], w[...])
        cos = cos_ref[rows, :]
        sin = sin_ref[rows, :]
        for c in range(tn // LANES):
            ch = acc[:, c * LANES:(c + 1) * LANES]
            if c < n_rope:
                rot = jnp.where(first_half,
                                pltpu.roll(ch, LANES - HEAD_DIM // 2, axis=1),
                                pltpu.roll(ch, HEAD_DIM // 2, axis=1))
                ch = ch * cos + rot * sin
            if scale != 1.0:
                ch = ch * scale
            o_ref[rows, c * LANES:(c + 1) * LANES] = ch.astype(o_ref.dtype)


def _attend(q_ref, k2, v2, valid, sink_ref, o_ref):
    tq = q_ref.shape[0]
    nk = k2.shape[0]
    lo = lax.broadcasted_iota(jnp.int32, (nk, LANES), 1) < HEAD_DIM
    lo_q = lax.broadcasted_iota(jnp.int32, (tq, LANES), 1) < HEAD_DIM
    group = N_HEADS // N_KV_HEADS
    for m in range(KV_DIM // LANES):
        kc = k2[:, m * LANES:(m + 1) * LANES]
        vc = v2[:, m * LANES:(m + 1) * LANES]
        kr = pltpu.roll(kc, HEAD_DIM, axis=1)
        vr = pltpu.roll(vc, HEAD_DIM, axis=1)
        for hh in range(2):
            kv_head = 2 * m + hh
            k_lo, k_hi = (kc, kr) if hh == 0 else (kr, kc)
            v_lo, v_hi = (vc, vr) if hh == 0 else (vr, vc)
            kbd = jnp.concatenate([jnp.where(lo, k_lo, 0.0), jnp.where(lo, 0.0, k_hi)], axis=0).astype(BF)
            vbd = jnp.concatenate([jnp.where(lo, v_lo, 0.0), jnp.where(lo, 0.0, v_hi)], axis=0).astype(BF)
            pairs = [kv_head * (group // 2) + pp for pp in range(group // 2)]
            qs = _bf(jnp.concatenate([q_ref[:, p * LANES:(p + 1) * LANES] for p in pairs], axis=0))
            s_all = lax.dot_general(qs, kbd, (((1,), (1,)), ((), ())), preferred_element_type=F32)
            es, inv = [], []
            for pp, p in enumerate(pairs):
                s = s_all[pp * tq:(pp + 1) * tq, :]
                e_pair, d_pair = [], []
                for half in range(2):
                    sh = jnp.where(valid, s[:, half * nk:(half + 1) * nk], NEG)
                    sink = sink_ref[2 * p + half]
                    mx = jnp.maximum(jnp.max(sh, axis=-1, keepdims=True), sink)
                    e = jnp.exp(sh - mx)
                    e_pair.append(e)
                    d_pair.append(jnp.sum(e, axis=-1, keepdims=True) + jnp.exp(sink - mx))
                es.append(jnp.concatenate(e_pair, axis=1))
                inv.append(jnp.where(lo_q, 1.0 / d_pair[0], 1.0 / d_pair[1]))
            o_all = jnp.dot(_bf(jnp.concatenate(es, axis=0)), vbd, preferred_element_type=F32)
            for pp, p in enumerate(pairs):
                o = o_all[pp * tq:(pp + 1) * tq, :] * inv[pp]
                o_ref[:, p * LANES:(p + 1) * LANES] = o.astype(o_ref.dtype)


def _attn_prompt_body(sink_ref, q_ref, kvp_ref, kvc_ref, o_ref):
    n = pl.program_id(1)
    kvp = kvp_ref[...]
    kvc = kvc_ref[...]
    k2 = jnp.concatenate([kvp[:, :KV_DIM], kvc[:, :KV_DIM]], axis=0)
    v2 = jnp.concatenate([kvp[:, KV_DIM:], kvc[:, KV_DIM:]], axis=0)
    i = lax.broadcasted_iota(jnp.int32, (WINDOW, 2 * WINDOW), 0)
    j = lax.broadcasted_iota(jnp.int32, (WINDOW, 2 * WINDOW), 1)
    valid = (j >= i) & (j <= i + WINDOW) & ((j >= WINDOW) | (n > 0))
    _attend(q_ref, k2, v2, valid, sink_ref, o_ref)


def _attn_sample_body(sink_ref, q_ref, kvn_ref, ck_ref, cv_ref, o_ref, ks_ref, vs_ref):
    t = q_ref.shape[0]
    ck = ck_ref[...]
    cv = cv_ref[...]
    kvn = kvn_ref[...]
    pad = jnp.zeros((WINDOW - t, KV_DIM), F32)
    k2 = jnp.concatenate([ck, kvn[:, :KV_DIM], pad], axis=0)
    v2 = jnp.concatenate([cv, kvn[:, KV_DIM:], pad], axis=0)
    i = lax.broadcasted_iota(jnp.int32, (t, 2 * WINDOW), 0)
    j = lax.broadcasted_iota(jnp.int32, (t, 2 * WINDOW), 1)
    valid = (j >= i) & (j <= i + WINDOW)
    _attend(q_ref, k2, v2, valid, sink_ref, o_ref)
    ks_ref[0:WINDOW - t, :] = ck[t:, :]
    ks_ref[WINDOW - t:, :] = kvn[:, :KV_DIM]
    vs_ref[0:WINDOW - t, :] = cv[t:, :]
    vs_ref[WINDOW - t:, :] = kvn[:, KV_DIM:]


def _conv3_carry(u, w_ref, prev):
    ext = jnp.concatenate([prev, u], axis=0)
    x1 = pltpu.roll(ext, 1, axis=0)[SUBLANES:, :]
    x2 = pltpu.roll(ext, 2, axis=0)[SUBLANES:, :]
    return w_ref[0:1, :] * x2 + w_ref[1:2, :] * x1 + w_ref[2:3, :] * u


def _conv3_segments(u, w_ref, e, seg):
    t = lax.broadcasted_iota(jnp.int32, u.shape, 0) & (seg - 1)
    x1 = jnp.where(t >= 1, pltpu.roll(u, 1, axis=0), pltpu.roll(e, u.shape[0] - 1, axis=0))
    x2 = jnp.where(t >= 2, pltpu.roll(u, 2, axis=0), e)
    return w_ref[0:1, :] * x2 + w_ref[1:2, :] * x1 + w_ref[2:3, :] * u


def _carry_in(carry_ref, tiles_per_seq):
    i = pl.program_id(0)
    j = pl.program_id(1)

    @pl.when(i % tiles_per_seq == 0)
    def _():
        carry_ref[j] = jnp.zeros(carry_ref.shape[1:], F32)

    return carry_ref[j]


def _chunks_with_riders(h_ref, hs_ref, chunk=None):
    chunks = _row_chunks(h_ref.shape[0], chunk)
    out = []
    for c, rows in enumerate(chunks):
        lhs = h_ref[rows, :]
        if c == RIDER_CHUNK:
            lhs = jnp.concatenate([lhs, hs_ref[...]], axis=0)
        out.append((rows, lhs, min(h_ref.shape[0], chunk or ROW_CHUNK)))
    return out


def _inconv_body(h_ref, hs_ref, wb_ref, wc_ref, wx_ref, cw_ref, e_ref, s_ref, tail_ref, ss_ref, us_ref,
                 carry_ref, wbs_ref, wcs_ref, wxs_ref, *, tiles_per_seq, seg):
    j = pl.program_id(1)
    wb = _staged(wb_ref, wbs_ref)
    wc = _staged(wc_ref, wcs_ref)
    wx = _staged(wx_ref, wxs_ref)
    prev = _carry_in(carry_ref, tiles_per_seq)
    for rows, lhs, n in _chunks_with_riders(h_ref, hs_ref):
        u_all = _dot(lhs, wc[...]) * _dot(lhs, wx[...])
        b_all = _dot(lhs, wb[...])
        u = u_all[:n, :]
        y = _conv3_carry(u, cw_ref, prev)
        prev = u[n - SUBLANES:, :]
        s_ref[rows, :] = (b_all[:n, :] * y).astype(s_ref.dtype)
        if u_all.shape[0] > n:
            us = u_all[n:, :]
            us_ref[...] = us
            ss_ref[...] = (b_all[n:, :] * _conv3_segments(us, cw_ref, e_ref[...], seg)).astype(ss_ref.dtype)
    carry_ref[j] = prev
    tail_ref[...] = prev


def _merged_body(h_ref, a_ref, s_ref, hs_ref, as_ref, ss_ref, wga_ref, wgc_ref, wab_ref, wcb_ref, o_ref, os_ref):
    chunks = _row_chunks(h_ref.shape[0])
    n = min(h_ref.shape[0], ROW_CHUNK)
    for c, rows in enumerate(chunks):
        h, a, s = h_ref[rows, :], a_ref[rows, :], s_ref[rows, :]
        last = c == RIDER_CHUNK
        if last:
            h = jnp.concatenate([h, hs_ref[...]], axis=0)
            a = jnp.concatenate([a, _bf(as_ref[...])], axis=0)
            s = jnp.concatenate([s, ss_ref[...]], axis=0)
        attn_br = jax.nn.sigmoid(_dot(h, wga_ref[...])) * _dot(a, wab_ref[...])
        conv_br = jax.nn.sigmoid(_dot(h, wgc_ref[...])) * _dot(s, wcb_ref[...])
        m = attn_br + conv_br
        o_ref[rows, :] = m[:n, :].astype(o_ref.dtype)
        if last:
            os_ref[...] = m[n:, :].astype(os_ref.dtype)


def _mm_resid_body(a_ref, as_ref, w_ref, x_ref, xs_ref, g_ref, gs_ref, o_ref, os_ref):
    for rows, lhs, n in _chunks_with_riders(a_ref, as_ref):
        y = _dot(lhs, w_ref[...])
        o_ref[rows, :] = ALPHA * x_ref[rows, :] + _rows_of(g_ref, rows) * y[:n, :]
        if y.shape[0] > n:
            os_ref[...] = ALPHA * xs_ref[...] + gs_ref[...] * y[n:, :]


def _with_casts(body, n_in, n_out, n_casts):
    def wrapped(*refs):
        ins, refs_ = refs[:n_in], refs[n_in:]
        srcs, refs_ = refs_[:n_casts], refs_[n_casts:]
        outs, refs_ = refs_[:n_out], refs_[n_out:]
        dsts, scratch = refs_[:n_casts], refs_[n_casts:]
        for src, dst in zip(srcs, dsts):
            dst[...] = src[...].astype(BF)
        body(*ins, *outs, *scratch)
    return wrapped


def _layernorm(r, g, b):
    mu = jnp.mean(r, axis=-1, keepdims=True)
    d = r - mu
    var = jnp.mean(d * d, axis=-1, keepdims=True)
    return d * lax.rsqrt(var + LN_EPS) * g + b


def _ln_mod_body(r_ref, g_ref, b_ref, sc_ref, sh_ref, x_ref, h_ref):
    y = _layernorm(r_ref[...], g_ref[...], b_ref[...])
    x_ref[...] = y
    h_ref[...] = (y * (1.0 + sc_ref[...]) + sh_ref[...]).astype(h_ref.dtype)


def _ln_body(r_ref, g_ref, b_ref, x_ref):
    x_ref[...] = _layernorm(r_ref[...], g_ref[...], b_ref[...])


def _silu_mul(g, v):
    return g * jax.nn.sigmoid(g) * v


def _ffnup_body(h_ref, hs_ref, wg_ref, wv_ref, cwg_ref, cwv_ref, eg_ref, ev_ref, wd_ref,
                act_ref, tg_ref, tv_ref, acts_ref, ugs_ref, uvs_ref, wdb_ref,
                cg_ref, cv_ref, wgs_ref, wvs_ref, *, tiles_per_seq, seg):
    j = pl.program_id(1)
    wdb_ref[...] = wd_ref[...].astype(BF)
    wg = _staged(wg_ref, wgs_ref)
    wv = _staged(wv_ref, wvs_ref)
    prev_g = _carry_in(cg_ref, tiles_per_seq)
    prev_v = _carry_in(cv_ref, tiles_per_seq)
    for rows, lhs, n in _chunks_with_riders(h_ref, hs_ref, FFN_CHUNK):
        ug_all = _dot(lhs, wg[...])
        uv_all = _dot(lhs, wv[...])
        ug = ug_all[:n, :]
        uv = uv_all[:n, :]
        yg = _conv3_carry(ug, cwg_ref, prev_g)
        yv = _conv3_carry(uv, cwv_ref, prev_v)
        prev_g = ug[n - SUBLANES:, :]
        prev_v = uv[n - SUBLANES:, :]
        act_ref[rows, :] = _silu_mul(yg, yv).astype(act_ref.dtype)
        if ug_all.shape[0] > n:
            ugs = ug_all[n:, :]
            uvs = uv_all[n:, :]
            ugs_ref[...] = ugs
            uvs_ref[...] = uvs
            ygs = _conv3_segments(ugs, cwg_ref, eg_ref[...], seg)
            yvs = _conv3_segments(uvs, cwv_ref, ev_ref[...], seg)
            acts_ref[...] = _silu_mul(ygs, yvs).astype(acts_ref.dtype)
    cg_ref[j] = prev_g
    cv_ref[j] = prev_v
    tg_ref[...] = prev_g
    tv_ref[...] = prev_v


class _Group:
    def __init__(self, batch, seq, tm, tm_big, ada_rows):
        self.batch, self.seq, self.tm, self.tm_big = batch, seq, tm, tm_big
        self.m = batch * seq
        self.per_row = seq < tm
        self.ada = ada_rows

    def param_spec(self, chunk, tm, tn):
        cb = chunk * (D_MODEL // tn)
        if self.per_row:
            return pl.BlockSpec((tm, tn), lambda i, j: (i, cb + j))
        tps = self.seq // tm
        return pl.BlockSpec((None, 1, tn), lambda i, j: (i // tps, 0, cb + j))


def _resident_rows(tm, width):
    return pl.BlockSpec((tm, width), lambda i, j: (i, 0), pipeline_mode=pl.Buffered(1))


def _ada(c_all, w_ada, b_ada):
    tn = 512
    n = w_ada.shape[-1]
    return pl.pallas_call(
        _ada_body,
        out_shape=jax.ShapeDtypeStruct((DEPTH, ADA_ROWS, n), F32),
        grid=(DEPTH, n // tn),
        in_specs=[pl.BlockSpec((ADA_ROWS, D_MODEL), lambda l, j: (0, 0)),
                  pl.BlockSpec((None, D_MODEL, tn), lambda l, j: (l, 0, j)),
                  pl.BlockSpec((None, 1, tn), lambda l, j: (l, 0, j))],
        out_specs=pl.BlockSpec((None, ADA_ROWS, tn), lambda l, j: (l, 0, j)),
        compiler_params=_cp(2), name="ada",
    )(c_all, w_ada, b_ada.reshape(DEPTH, 1, n))


def _modulate(grp, x):
    tr = min(grp.tm, 512)
    return pl.pallas_call(
        _mod_body,
        out_shape=jax.ShapeDtypeStruct((grp.m, D_MODEL), BF),
        grid=(grp.m // tr,),
        in_specs=[pl.BlockSpec((tr, D_MODEL), lambda i: (i, 0)),
                  _row_param_spec(grp, SC1, tr), _row_param_spec(grp, SH1, tr)],
        out_specs=pl.BlockSpec((tr, D_MODEL), lambda i: (i, 0)),
        compiler_params=_cp(1), name="modulate",
    )(x, grp.ada, grp.ada)


def _row_param_spec(grp, chunk, tr):
    if grp.per_row:
        return pl.BlockSpec((tr, D_MODEL), lambda i: (i, chunk))
    tps = grp.seq // tr
    return pl.BlockSpec((None, 1, D_MODEL), lambda i: (i // tps, 0, chunk))


def _proj_rope(grp, h, w_in, l, col0, n_cols, tm, tn, cos, sin, n_rope, scale, out_dtype, name, cast_of=None):
    cb = col0 // tn
    t_tiles = cos.shape[0] // tm
    nj = n_cols // tn
    casts = [] if cast_of is None else [_row_cast(cast_of, l, (grp.m // tm) * nj, lambda i, j: i * nj + j)]
    out = pl.pallas_call(
        _with_casts(functools.partial(_proj_rope_body, n_rope=n_rope, scale=scale), 4, 1, len(casts)),
        out_shape=(jax.ShapeDtypeStruct((grp.m, n_cols), out_dtype), *[c[3] for c in casts]),
        grid=(grp.m // tm, nj),
        in_specs=[pl.BlockSpec((tm, D_MODEL), lambda i, j: (i, 0)),
                  pl.BlockSpec((None, D_MODEL, tn), lambda i, j: (l, 0, cb + j)),
                  pl.BlockSpec((tm, LANES), lambda i, j: (i % t_tiles, 0)),
                  pl.BlockSpec((tm, LANES), lambda i, j: (i % t_tiles, 0)),
                  *[c[1] for c in casts]],
        out_specs=(pl.BlockSpec((tm, tn), lambda i, j: (i, j)), *[c[2] for c in casts]),
        scratch_shapes=[pltpu.VMEM((D_MODEL, tn), BF)],
        compiler_params=_cp(2), name=name,
    )(h, w_in, cos, sin, *[c[0] for c in casts])
    return out[0] if cast_of is None else out


def _attn_prompt(sinks, q, kv, batch, seq, l, w_in, w_attn_br, w_conv_br):
    nb = seq // WINDOW
    step_of = lambda b, n: b * nb + n
    casts = [_col_cast(w_in, l, COL_GA, 2 * D_MODEL, batch * nb, step_of),
             _row_cast(w_attn_br, l, batch * nb, step_of), _row_cast(w_conv_br, l, batch * nb, step_of)]
    return pl.pallas_call(
        _with_casts(_attn_prompt_body, 4, 1, len(casts)),
        out_shape=(jax.ShapeDtypeStruct(q.shape, BF), *[c[3] for c in casts]),
        grid=(batch, nb),
        in_specs=[pl.BlockSpec(memory_space=pltpu.SMEM),
                  pl.BlockSpec((WINDOW, ATTN_DIM), lambda b, n: (b * nb + n, 0)),
                  pl.BlockSpec((WINDOW, 2 * KV_DIM), lambda b, n: (b * nb + jnp.maximum(n - 1, 0), 0)),
                  pl.BlockSpec((WINDOW, 2 * KV_DIM), lambda b, n: (b * nb + n, 0)),
                  *[c[1] for c in casts]],
        out_specs=(pl.BlockSpec((WINDOW, ATTN_DIM), lambda b, n: (b * nb + n, 0)), *[c[2] for c in casts]),
        compiler_params=_cp(2), name="attn_prompt",
    )(sinks, q, kv, kv, *[c[0] for c in casts])


def _attn_sample(sinks, q, kvn, cache_k, cache_v, l, batch, seq):
    return pl.pallas_call(
        _attn_sample_body,
        out_shape=(jax.ShapeDtypeStruct(q.shape, F32),
                   jax.ShapeDtypeStruct((batch, WINDOW, KV_DIM), F32),
                   jax.ShapeDtypeStruct((batch, WINDOW, KV_DIM), F32)),
        grid=(batch,),
        in_specs=[pl.BlockSpec(memory_space=pltpu.SMEM),
                  pl.BlockSpec((seq, ATTN_DIM), lambda b: (b, 0)),
                  pl.BlockSpec((seq, 2 * KV_DIM), lambda b: (b, 0)),
                  pl.BlockSpec((None, None, WINDOW, KV_DIM), lambda b: (l, b, 0, 0)),
                  pl.BlockSpec((None, None, WINDOW, KV_DIM), lambda b: (l, b, 0, 0))],
        out_specs=(pl.BlockSpec((seq, ATTN_DIM), lambda b: (b, 0)),
                   pl.BlockSpec((None, WINDOW, KV_DIM), lambda b: (b, 0, 0)),
                   pl.BlockSpec((None, WINDOW, KV_DIM), lambda b: (b, 0, 0))),
        compiler_params=_cp(1), name="attn_sample",
    )(sinks, q, kvn, cache_k, cache_v)


def _rider_rows(gp, gs, tm):
    rs = gs.m // (gp.m // tm)
    assert rs * (gp.m // tm) == gs.m and rs % gs.seq == 0 and rs % (2 * SUBLANES) == 0
    return rs


def _in_conv(gp, gs, h, hs, w_in, conv_w, l, past_rows):
    tm, tn = gp.tm_big, 256
    nj = CONV_DIM // tn
    rs = _rider_rows(gp, gs, tm)
    w_spec = lambda col0: pl.BlockSpec((None, D_MODEL, tn), lambda i, j: (l, 0, col0 // tn + j))
    tile = lambda rows: pl.BlockSpec((rows, tn), lambda i, j: (i, j))
    return pl.pallas_call(
        functools.partial(_inconv_body, tiles_per_seq=gp.seq // tm, seg=gs.seq),
        out_shape=(jax.ShapeDtypeStruct((gp.m, CONV_DIM), BF),
                   jax.ShapeDtypeStruct((gp.m // tm, SUBLANES, CONV_DIM), F32),
                   jax.ShapeDtypeStruct((gs.m, CONV_DIM), BF),
                   jax.ShapeDtypeStruct((gs.m, CONV_DIM), F32)),
        grid=(gp.m // tm, nj),
        in_specs=[_resident_rows(tm, D_MODEL), pl.BlockSpec((rs, D_MODEL), lambda i, j: (i, 0)),
                  w_spec(COL_CB), w_spec(COL_CC), w_spec(COL_CX),
                  pl.BlockSpec((None, 3, tn), lambda i, j: (l, 0, j)), tile(rs)],
        out_specs=(tile(tm), pl.BlockSpec((None, SUBLANES, tn), lambda i, j: (i, 0, j)), tile(rs), tile(rs)),
        scratch_shapes=[pltpu.VMEM((nj, SUBLANES, tn), F32)] + [pltpu.VMEM((D_MODEL, tn), BF)] * 3,
        compiler_params=_cp(2), name="in_conv",
    )(h, hs, w_in, w_in, w_in, conv_w, past_rows)


def _row_cast(w, l, steps, step_of):
    _, r, c = w.shape
    rows = r // steps
    assert rows * steps == r and rows % (2 * SUBLANES) == 0
    return (w, pl.BlockSpec((None, rows, c), lambda *g: (l, step_of(*g), 0)),
            pl.BlockSpec((rows, c), lambda *g: (step_of(*g), 0)), jax.ShapeDtypeStruct((r, c), BF))


def _col_cast(w, l, col0, n_cols, steps, step_of):
    _, r, _ = w.shape
    cols = n_cols // steps
    assert cols * steps == n_cols and cols % LANES == 0 and col0 % cols == 0
    return (w, pl.BlockSpec((None, r, cols), lambda *g: (l, 0, col0 // cols + step_of(*g))),
            pl.BlockSpec((r, cols), lambda *g: (0, step_of(*g))), jax.ShapeDtypeStruct((r, n_cols), BF))


def _merged(gp, gs, h, attn, s, hs, attn_s, s_s, w_gates, w_ab, w_cb):
    tm, tn = gp.tm, 256
    rs = _rider_rows(gp, gs, tm)
    row = lambda rows, width: pl.BlockSpec((rows, width), lambda i, j: (i, 0))
    w2d = lambda k, col0: pl.BlockSpec((k, tn), lambda i, j: (0, col0 // tn + j))
    tile = lambda rows: pl.BlockSpec((rows, tn), lambda i, j: (i, j))
    return pl.pallas_call(
        _merged_body,
        out_shape=(jax.ShapeDtypeStruct((gp.m, D_MODEL), BF), jax.ShapeDtypeStruct((gs.m, D_MODEL), BF)),
        grid=(gp.m // tm, D_MODEL // tn),
        in_specs=[row(tm, D_MODEL), row(tm, ATTN_DIM), row(tm, CONV_DIM),
                  row(rs, D_MODEL), row(rs, ATTN_DIM), row(rs, CONV_DIM),
                  w2d(D_MODEL, 0), w2d(D_MODEL, COL_GC - COL_GA), w2d(ATTN_DIM, 0), w2d(CONV_DIM, 0)],
        out_specs=(tile(tm), tile(rs)),
        compiler_params=_cp(2), name="merged",
    )(h, attn, s, hs, attn_s, s_s, w_gates, w_gates, w_ab, w_cb)


def _mm_resid(gp, gs, a, a_s, w_bf16, x, x_s, gate_chunk, tm, tn, name):
    k = a.shape[1]
    rs = _rider_rows(gp, gs, tm)
    tile = lambda rows: pl.BlockSpec((rows, tn), lambda i, j: (i, j))
    return pl.pallas_call(
        _mm_resid_body,
        out_shape=(jax.ShapeDtypeStruct((gp.m, D_MODEL), F32), jax.ShapeDtypeStruct((gs.m, D_MODEL), F32)),
        grid=(gp.m // tm, D_MODEL // tn),
        in_specs=[pl.BlockSpec((tm, k), lambda i, j: (i, 0)), pl.BlockSpec((rs, k), lambda i, j: (i, 0)),
                  pl.BlockSpec((k, tn), lambda i, j: (0, j)), tile(tm), tile(rs),
                  gp.param_spec(gate_chunk, tm, tn), gs.param_spec(gate_chunk, rs, tn)],
        out_specs=(tile(tm), tile(rs)),
        compiler_params=_cp(2), name=name,
    )(a, a_s, w_bf16, x, x_s, gp.ada, gs.ada)


def _ln(grp, r, g, b, l, mod_grp, mod_chunks, name):
    tr = min(grp.m, 512)
    row = pl.BlockSpec((tr, D_MODEL), lambda i: (i, 0))
    vec = pl.BlockSpec((None, 1, D_MODEL), lambda i: (l, 0, 0))
    g3 = g.reshape(DEPTH, 1, D_MODEL)
    b3 = b.reshape(DEPTH, 1, D_MODEL)
    if mod_chunks is None:
        return pl.pallas_call(
            _ln_body, out_shape=jax.ShapeDtypeStruct((grp.m, D_MODEL), F32), grid=(grp.m // tr,),
            in_specs=[row, vec, vec], out_specs=row, compiler_params=_cp(1), name=name,
        )(r, g3, b3), None
    sc, sh = mod_chunks
    return pl.pallas_call(
        _ln_mod_body,
        out_shape=(jax.ShapeDtypeStruct((grp.m, D_MODEL), F32), jax.ShapeDtypeStruct((grp.m, D_MODEL), BF)),
        grid=(grp.m // tr,),
        in_specs=[row, vec, vec, _row_param_spec(mod_grp, sc, tr), _row_param_spec(mod_grp, sh, tr)],
        out_specs=(row, row), compiler_params=_cp(1), name=name,
    )(r, g3, b3, mod_grp.ada, mod_grp.ada)


def _ffn_up(gp, gs, h, hs, w_up, ffn_conv_w, w_down, l, past_g, past_v):
    tm, tn = gp.tm_big, 256
    nj = D_FF // tn
    rs = _rider_rows(gp, gs, tm)
    steps = (gp.m // tm) * nj
    wd_rows = D_FF // steps
    assert wd_rows * steps == D_FF and wd_rows % (2 * SUBLANES) == 0
    tile = lambda rows: pl.BlockSpec((rows, tn), lambda i, j: (i, j))
    tail_shape = jax.ShapeDtypeStruct((gp.m // tm, SUBLANES, D_FF), F32)
    tail_spec = pl.BlockSpec((None, SUBLANES, tn), lambda i, j: (i, 0, j))
    full_s = jax.ShapeDtypeStruct((gs.m, D_FF), F32)
    return pl.pallas_call(
        functools.partial(_ffnup_body, tiles_per_seq=gp.seq // tm, seg=gs.seq),
        out_shape=(jax.ShapeDtypeStruct((gp.m, D_FF), BF), tail_shape, tail_shape,
                   jax.ShapeDtypeStruct((gs.m, D_FF), BF), full_s, full_s,
                   jax.ShapeDtypeStruct((D_FF, D_MODEL), BF)),
        grid=(gp.m // tm, nj),
        in_specs=[_resident_rows(tm, D_MODEL), pl.BlockSpec((rs, D_MODEL), lambda i, j: (i, 0)),
                  pl.BlockSpec((None, D_MODEL, tn), lambda i, j: (l, 0, j)),
                  pl.BlockSpec((None, D_MODEL, tn), lambda i, j: (l, 0, nj + j)),
                  pl.BlockSpec((None, 3, tn), lambda i, j: (l, 0, j)),
                  pl.BlockSpec((None, 3, tn), lambda i, j: (l, 0, nj + j)),
                  tile(rs), tile(rs),
                  pl.BlockSpec((None, wd_rows, D_MODEL), lambda i, j: (l, i * nj + j, 0))],
        out_specs=(tile(tm), tail_spec, tail_spec, tile(rs), tile(rs), tile(rs),
                   pl.BlockSpec((wd_rows, D_MODEL), lambda i, j: (i * nj + j, 0))),
        scratch_shapes=[pltpu.VMEM((nj, SUBLANES, tn), F32), pltpu.VMEM((nj, SUBLANES, tn), F32),
                        pltpu.VMEM((D_MODEL, tn), BF), pltpu.VMEM((D_MODEL, tn), BF)],
        compiler_params=_cp(2), name="ffn_up",
    )(h, hs, w_up, w_up, ffn_conv_w, ffn_conv_w, past_g, past_v, w_down)


def _rope_tables(pos):
    inv = ROPE_THETA ** (-jnp.arange(0, HEAD_DIM, 2, dtype=F32) / HEAD_DIM)
    ang = pos.astype(F32)[:, None] * inv[None, :]
    cos, sin = jnp.cos(ang), jnp.sin(ang)
    return jnp.concatenate([cos] * 4, axis=-1), jnp.concatenate([-sin, sin, -sin, sin], axis=-1)


def _expand_past(past, seq):
    batch, _, n = past.shape
    return jnp.concatenate([past, jnp.zeros((batch, seq - 2, n), past.dtype)], axis=1).reshape(batch * seq, n)


def _last_rows(x, batch, seq, n_rows):
    return x.reshape(batch, seq, x.shape[-1])[:, seq - n_rows:, :]


def _qkv(grp, l, h, rope, w_in, cast_of=None):
    cos, sin = rope
    q = _proj_rope(grp, h, w_in, l, COL_Q, ATTN_DIM, grp.tm_big, 256, cos, sin, 256 // LANES, HEAD_DIM ** -0.5,
                   F32 if grp.per_row else BF, "proj_q", cast_of)
    kv = _proj_rope(grp, h, w_in, l, COL_KV, 2 * KV_DIM, grp.tm, 2 * KV_DIM, cos, sin, KV_DIM // LANES, 1.0,
                    F32, "proj_kv")
    return q, kv


def _layer(gp, gs, next_gp, next_gs, l, xp, xs, hp, hs, rope_p, rope_s, weights, sinks, past):
    (w_in, conv_w, w_attn_br, w_conv_br, w_out, ln1_g, ln1_b, w_up, ffn_conv_w, w_down, ln2_g, ln2_b) = weights
    cache_k, cache_v, state_conv, state_ffn = past
    tps = gp.seq // gp.tm_big
    state_rows = lambda tails: tails[tps - 1::tps, SUBLANES - 2:, :]

    q_s, kv_s = _qkv(gs, l, hs, rope_s, w_in)
    attn_s, k_s, v_s = _attn_sample(sinks, q_s, kv_s, cache_k, cache_v, l, gs.batch, gs.seq)
    (q_p, w_out_b), kv_p = _qkv(gp, l, hp, rope_p, w_in, cast_of=w_out)
    attn_p, w_gates, w_ab, w_cb = _attn_prompt(sinks, q_p, kv_p, gp.batch, gp.seq, l, w_in, w_attn_br, w_conv_br)
    kv_tail = _last_rows(kv_p, gp.batch, gp.seq, WINDOW)
    k_p, v_p = kv_tail[..., :KV_DIM], kv_tail[..., KV_DIM:]

    s_p, tails, s_s, u_s = _in_conv(gp, gs, hp, hs, w_in, conv_w, l, _expand_past(state_conv[l], gs.seq))
    conv_p = state_rows(tails)
    conv_s = _last_rows(u_s, gs.batch, gs.seq, 2)

    merged_p, merged_s = _merged(gp, gs, hp, attn_p, s_p, hs, attn_s, s_s, w_gates, w_ab, w_cb)
    r1_p, r1_s = _mm_resid(gp, gs, merged_p, merged_s, w_out_b, xp, xs, G1, gp.tm, 1024, "out_proj")
    x1_s, h2_s = _ln(gs, r1_s, ln1_g, ln1_b, l, gs, (SC2, SH2), "ln1")
    x1_p, h2_p = _ln(gp, r1_p, ln1_g, ln1_b, l, gp, (SC2, SH2), "ln1")

    eg = _expand_past(state_ffn[l][..., :D_FF], gs.seq)
    ev = _expand_past(state_ffn[l][..., D_FF:], gs.seq)
    act_p, tg, tv, act_s, ug_s, uv_s, w_down_b = _ffn_up(gp, gs, h2_p, h2_s, w_up, ffn_conv_w, w_down, l, eg, ev)
    ffn_p = state_rows(jnp.concatenate([tg, tv], axis=-1))
    ffn_s = _last_rows(jnp.concatenate([ug_s, uv_s], axis=-1), gs.batch, gs.seq, 2)

    r2_p, r2_s = _mm_resid(gp, gs, act_p, act_s, w_down_b, x1_p, x1_s, G2, 512, 512, "ffn_down")
    mod = None if next_gp is None else (SC1, SH1)
    xs, hs = _ln(gs, r2_s, ln2_g, ln2_b, l, next_gs, mod, "ln2")
    xp, hp = _ln(gp, r2_p, ln2_g, ln2_b, l, next_gp, mod, "ln2")

    heads = lambda k, grp: k.reshape(grp.batch, WINDOW, N_KV_HEADS, HEAD_DIM)
    return (xp, xs, hp, hs, (heads(k_p, gp), heads(v_p, gp), conv_p, ffn_p),
            (heads(k_s, gs), heads(v_s, gs), conv_s, ffn_s))


def kernel(x_prompt, x_sample, cache_attn_k, cache_attn_v, state_conv, state_ffn_conv, c_prompt, c_sample, w_ada, b_ada, w_in, attn_sinks, conv_w, w_attn_br, w_conv_br, w_out, ln1_g, ln1_b, w_up, ffn_conv_w, w_down, ln2_g, ln2_b):
    bp, tp, _ = x_prompt.shape
    bs, ts, _ = x_sample.shape
    assert bp <= ADA_SAMPLE_ROW0 and ADA_SAMPLE_ROW0 + bs == ADA_ROWS

    c_all = jnp.concatenate([c_prompt, jnp.zeros((ADA_SAMPLE_ROW0 - bp, D_MODEL), F32), c_sample], axis=0)
    ada = _ada(c_all, w_ada, b_ada)

    weights = (w_in, conv_w, w_attn_br, w_conv_br, w_out, ln1_g, ln1_b, w_up, ffn_conv_w, w_down, ln2_g, ln2_b)
    rope_p = _rope_tables(jnp.arange(tp, dtype=jnp.int32))
    rope_s = tuple(jnp.tile(t, (bs, 1)) for t in _rope_tables(PAST_LEN + jnp.arange(ts, dtype=jnp.int32)))
    cache_k = cache_attn_k.reshape(DEPTH, bs, WINDOW, KV_DIM)
    cache_v = cache_attn_v.reshape(DEPTH, bs, WINDOW, KV_DIM)
    past = (cache_k, cache_v, state_conv, state_ffn_conv)

    xp = x_prompt.reshape(bp * tp, D_MODEL)
    xs = x_sample.reshape(bs * ts, D_MODEL)
    gps = [_Group(bp, tp, 1024, 2048, ada[l, :bp].reshape(bp, 1, -1)) for l in range(DEPTH)] + [None]
    gss = [_Group(bs, ts, bs * ts, bs * ts, jnp.repeat(ada[l, ADA_SAMPLE_ROW0:], ts, axis=0))
           for l in range(DEPTH)] + [None]
    hp = _modulate(gps[0], xp)
    hs = _modulate(gss[0], xs)
    outs_p, outs_s = [], []
    for l in range(DEPTH):
        xp, xs, hp, hs, st_p, st_s = _layer(gps[l], gss[l], gps[l + 1], gss[l + 1], l, xp, xs, hp, hs,
                                            rope_p, rope_s, weights, attn_sinks[l], past)
        outs_p.append(st_p)
        outs_s.append(st_s)
    stack = lambda outs, k: jnp.stack([o[k] for o in outs])
    return (xp.reshape(bp, tp, D_MODEL), xs.reshape(bs, ts, D_MODEL),
            stack(outs_p, 0), stack(outs_p, 1), stack(outs_p, 2), stack(outs_p, 3),
            stack(outs_s, 0), stack(outs_s, 1), stack(outs_s, 2), stack(outs_s, 3))
```

```python
import functools

import jax
import jax.numpy as jnp
from jax import lax
from jax.experimental import pallas as pl
from jax.experimental.pallas import tpu as pltpu

D_MODEL = 4096
DEPTH = 2
HEAD_DIM = 64
N_HEADS = 32
N_KV_HEADS = 4
ATTN_DIM = N_HEADS * HEAD_DIM
KV_DIM = N_KV_HEADS * HEAD_DIM
WINDOW = 128
ROPE_THETA = 10000.0
CONV_DIM = D_MODEL // 2
D_FF = 11008
PAST_LEN = 16384
LN_EPS = 1e-5
ALPHA = (2 * DEPTH) ** 0.25

COL_Q = 0
COL_KV = ATTN_DIM
COL_CB = ATTN_DIM + 2 * KV_DIM
COL_CC = COL_CB + CONV_DIM
COL_CX = COL_CC + CONV_DIM
COL_GA = COL_CX + CONV_DIM
COL_GC = COL_GA + D_MODEL

SH1, SC1, G1, SH2, SC2, G2 = range(6)

LANES = 128
SUBLANES = 8
ADA_ROWS = 40
ADA_SAMPLE_ROW0 = 8
ROW_CHUNK = 512
ATTN_BLOCKS = 2
ROPE_CHUNK = 256
RIDER_CHUNK = 0
VMEM_LIMIT = 56 * 2**20
NEG = -1e30

BF = jnp.bfloat16
F32 = jnp.float32


def _cp(n_axes):
    return pltpu.CompilerParams(dimension_semantics=("arbitrary",) * n_axes, vmem_limit_bytes=VMEM_LIMIT)


def _bf(x):
    return x if x.dtype == BF else x.astype(BF)


def _dot(a, b):
    return jnp.dot(_bf(a), _bf(b), preferred_element_type=F32)


def _row_chunks(tm, chunk=None):
    chunk = min(tm, chunk or ROW_CHUNK)
    assert tm % chunk == 0
    return [pl.ds(r, chunk) for r in range(0, tm, chunk)]


def _staged(w_ref, stage_ref):
    stage_ref[...] = w_ref[...].astype(BF)
    return stage_ref


def _rows_of(ref, rows):
    return ref[...] if ref.shape[0] == 1 else ref[rows, :]


def _ada_body(c_ref, w_ref, b_ref, o_ref):
    c = c_ref[...]
    o_ref[...] = _dot(c * jax.nn.sigmoid(c), w_ref[...]) + b_ref[...]


def _mod_body(x_ref, sc_ref, sh_ref, o_ref):
    o_ref[...] = (x_ref[...] * (1.0 + sc_ref[...]) + sh_ref[...]).astype(o_ref.dtype)


def _proj_rope_body(h_ref, w_ref, cos_ref, sin_ref, o_ref, ws_ref, *, n_rope, scale):
    tn = o_ref.shape[1]
    w = _staged(w_ref, ws_ref)
    lane = lax.broadcasted_iota(jnp.int32, (min(h_ref.shape[0], ROPE_CHUNK), LANES), 1)
    first_half = (lane & (HEAD_DIM - 1)) < HEAD_DIM // 2
    for rows in _row_chunks(h_ref.shape[0], ROPE_CHUNK):
        acc = _dot(h_ref[rows, :], w[...])
        cos = cos_ref[rows, :]
        sin = sin_ref[rows, :]
        for c in range(tn // LANES):
            ch = acc[:, c * LANES:(c + 1) * LANES]
            if c < n_rope:
                rot = jnp.where(first_half,
                                pltpu.roll(ch, LANES - HEAD_DIM // 2, axis=1),
                                pltpu.roll(ch, HEAD_DIM // 2, axis=1))
                ch = ch * cos + rot * sin
            if scale != 1.0:
                ch = ch * scale
            o_ref[rows, c * LANES:(c + 1) * LANES] = ch.astype(o_ref.dtype)


def _attend(q_ref, k2, v2, valid, sink_ref, o_ref):
    tq = q_ref.shape[0]
    nk = k2.shape[0]
    lo = lax.broadcasted_iota(jnp.int32, (nk, LANES), 1) < HEAD_DIM
    lo_q = lax.broadcasted_iota(jnp.int32, (tq, LANES), 1) < HEAD_DIM
    group = N_HEADS // N_KV_HEADS
    for m in range(KV_DIM // LANES):
        kc = k2[:, m * LANES:(m + 1) * LANES]
        vc = v2[:, m * LANES:(m + 1) * LANES]
        kr = pltpu.roll(kc, HEAD_DIM, axis=1)
        vr = pltpu.roll(vc, HEAD_DIM, axis=1)
        for hh in range(2):
            kv_head = 2 * m + hh
            k_lo, k_hi = (kc, kr) if hh == 0 else (kr, kc)
            v_lo, v_hi = (vc, vr) if hh == 0 else (vr, vc)
            kbd = jnp.concatenate([jnp.where(lo, k_lo, 0.0), jnp.where(lo, 0.0, k_hi)], axis=0).astype(BF)
            vbd = jnp.concatenate([jnp.where(lo, v_lo, 0.0), jnp.where(lo, 0.0, v_hi)], axis=0).astype(BF)
            pairs = [kv_head * (group // 2) + pp for pp in range(group // 2)]
            qs = _bf(jnp.concatenate([q_ref[:, p * LANES:(p + 1) * LANES] for p in pairs], axis=0))
            s_all = lax.dot_general(qs, kbd, (((1,), (1,)), ((), ())), preferred_element_type=F32)
            es, inv = [], []
            for pp, p in enumerate(pairs):
                s = s_all[pp * tq:(pp + 1) * tq, :]
                e_pair, d_pair = [], []
                for half in range(2):
                    sh = jnp.where(valid, s[:, half * nk:(half + 1) * nk], NEG)
                    sink = sink_ref[2 * p + half]
                    mx = jnp.maximum(jnp.max(sh, axis=-1, keepdims=True), sink)
                    e = jnp.exp(sh - mx)
                    e_pair.append(e)
                    d_pair.append(jnp.sum(e, axis=-1, keepdims=True) + jnp.exp(sink - mx))
                es.append(jnp.concatenate(e_pair, axis=1))
                inv.append(jnp.where(lo_q, 1.0 / d_pair[0], 1.0 / d_pair[1]))
            o_all = jnp.dot(_bf(jnp.concatenate(es, axis=0)), vbd, preferred_element_type=F32)
            for pp, p in enumerate(pairs):
                o = o_all[pp * tq:(pp + 1) * tq, :] * inv[pp]
                o_ref[:, p * LANES:(p + 1) * LANES] = o.astype(o_ref.dtype)


def _attn_prompt_body(sink_ref, q_ref, kvp_ref, kvc_ref, o_ref):
    m = pl.program_id(1)
    kv_all = jnp.concatenate([kvp_ref[...], kvc_ref[...]], axis=0)
    i = lax.broadcasted_iota(jnp.int32, (WINDOW, 2 * WINDOW), 0)
    j = lax.broadcasted_iota(jnp.int32, (WINDOW, 2 * WINDOW), 1)
    band = (j >= i) & (j <= i + WINDOW)
    for sub in range(ATTN_BLOCKS):
        kv2 = kv_all[sub * WINDOW:(sub + 2) * WINDOW, :]
        valid = band if sub > 0 else band & ((j >= WINDOW) | (m > 0))
        rows = pl.ds(sub * WINDOW, WINDOW)
        _attend(q_ref.at[rows], kv2[:, :KV_DIM], kv2[:, KV_DIM:], valid, sink_ref, o_ref.at[rows])


def _attn_sample_body(sink_ref, q_ref, kvn_ref, ck_ref, cv_ref, o_ref, ks_ref, vs_ref):
    t = q_ref.shape[0]
    ck = ck_ref[...]
    cv = cv_ref[...]
    kvn = kvn_ref[...]
    pad = jnp.zeros((WINDOW - t, KV_DIM), F32)
    k2 = jnp.concatenate([ck, kvn[:, :KV_DIM], pad], axis=0)
    v2 = jnp.concatenate([cv, kvn[:, KV_DIM:], pad], axis=0)
    i = lax.broadcasted_iota(jnp.int32, (t, 2 * WINDOW), 0)
    j = lax.broadcasted_iota(jnp.int32, (t, 2 * WINDOW), 1)
    valid = (j >= i) & (j <= i + WINDOW)
    _attend(q_ref, k2, v2, valid, sink_ref, o_ref)
    ks_ref[0:WINDOW - t, :] = ck[t:, :]
    ks_ref[WINDOW - t:, :] = kvn[:, :KV_DIM]
    vs_ref[0:WINDOW - t, :] = cv[t:, :]
    vs_ref[WINDOW - t:, :] = kvn[:, KV_DIM:]


def _conv3_carry(u, w_ref, prev):
    ext = jnp.concatenate([prev, u], axis=0)
    x1 = pltpu.roll(ext, 1, axis=0)[SUBLANES:, :]
    x2 = pltpu.roll(ext, 2, axis=0)[SUBLANES:, :]
    return w_ref[0:1, :] * x2 + w_ref[1:2, :] * x1 + w_ref[2:3, :] * u


def _conv3_segments(u, w_ref, e, seg):
    t = lax.broadcasted_iota(jnp.int32, u.shape, 0) & (seg - 1)
    x1 = jnp.where(t >= 1, pltpu.roll(u, 1, axis=0), pltpu.roll(e, u.shape[0] - 1, axis=0))
    x2 = jnp.where(t >= 2, pltpu.roll(u, 2, axis=0), e)
    return w_ref[0:1, :] * x2 + w_ref[1:2, :] * x1 + w_ref[2:3, :] * u


def _carry_in(carry_ref, tiles_per_seq):
    i = pl.program_id(0)
    j = pl.program_id(1)

    @pl.when(i % tiles_per_seq == 0)
    def _():
        carry_ref[j] = jnp.zeros(carry_ref.shape[1:], F32)

    return carry_ref[j]


def _chunks_with_riders(h_ref, hs_ref, chunk=None):
    chunks = _row_chunks(h_ref.shape[0], chunk)
    out = []
    for c, rows in enumerate(chunks):
        lhs = h_ref[rows, :]
        if c == RIDER_CHUNK:
            lhs = jnp.concatenate([lhs, hs_ref[...]], axis=0)
        out.append((rows, lhs, min(h_ref.shape[0], chunk or ROW_CHUNK)))
    return out


def _inconv_body(h_ref, hs_ref, wb_ref, wc_ref, wx_ref, cw_ref, e_ref, s_ref, tail_ref, ss_ref, us_ref,
                 carry_ref, wbs_ref, wcs_ref, wxs_ref, *, tiles_per_seq, seg):
    j = pl.program_id(1)
    wb = _staged(wb_ref, wbs_ref)
    wc = _staged(wc_ref, wcs_ref)
    wx = _staged(wx_ref, wxs_ref)
    prev = _carry_in(carry_ref, tiles_per_seq)
    for rows, lhs, n in _chunks_with_riders(h_ref, hs_ref):
        u_all = _dot(lhs, wc[...]) * _dot(lhs, wx[...])
        b_all = _dot(lhs, wb[...])
        u = u_all[:n, :]
        y = _conv3_carry(u, cw_ref, prev)
        prev = u[n - SUBLANES:, :]
        s_ref[rows, :] = (b_all[:n, :] * y).astype(s_ref.dtype)
        if u_all.shape[0] > n:
            us = u_all[n:, :]
            us_ref[...] = us
            ss_ref[...] = (b_all[n:, :] * _conv3_segments(us, cw_ref, e_ref[...], seg)).astype(ss_ref.dtype)
    carry_ref[j] = prev
    tail_ref[...] = prev


def _merged_body(h_ref, a_ref, s_ref, hs_ref, as_ref, ss_ref, wga_ref, wgc_ref, wab_ref, wcb_ref, o_ref, os_ref):
    chunks = _row_chunks(h_ref.shape[0])
    n = min(h_ref.shape[0], ROW_CHUNK)
    for c, rows in enumerate(chunks):
        h, a, s = h_ref[rows, :], a_ref[rows, :], s_ref[rows, :]
        last = c == RIDER_CHUNK
        if last:
            h = jnp.concatenate([h, hs_ref[...]], axis=0)
            a = jnp.concatenate([a, _bf(as_ref[...])], axis=0)
            s = jnp.concatenate([s, ss_ref[...]], axis=0)
        attn_br = jax.nn.sigmoid(_dot(h, wga_ref[...])) * _dot(a, wab_ref[...])
        conv_br = jax.nn.sigmoid(_dot(h, wgc_ref[...])) * _dot(s, wcb_ref[...])
        m = attn_br + conv_br
        o_ref[rows, :] = m[:n, :].astype(o_ref.dtype)
        if last:
            os_ref[...] = m[n:, :].astype(os_ref.dtype)


def _mm_resid_body(a_ref, as_ref, w_ref, x_ref, xs_ref, g_ref, gs_ref, o_ref, os_ref):
    for rows, lhs, n in _chunks_with_riders(a_ref, as_ref):
        y = _dot(lhs, w_ref[...])
        o_ref[rows, :] = ALPHA * x_ref[rows, :] + _rows_of(g_ref, rows) * y[:n, :]
        if y.shape[0] > n:
            os_ref[...] = ALPHA * xs_ref[...] + gs_ref[...] * y[n:, :]


def _with_casts(body, n_in, n_out, n_casts):
    def wrapped(*refs):
        ins, refs_ = refs[:n_in], refs[n_in:]
        srcs, refs_ = refs_[:n_casts], refs_[n_casts:]
        outs, refs_ = refs_[:n_out], refs_[n_out:]
        dsts, scratch = refs_[:n_casts], refs_[n_casts:]
        for src, dst in zip(srcs, dsts):
            dst[...] = src[...].astype(BF)
        body(*ins, *outs, *scratch)
    return wrapped


def _layernorm(r, g, b):
    mu = jnp.mean(r, axis=-1, keepdims=True)
    d = r - mu
    var = jnp.mean(d * d, axis=-1, keepdims=True)
    return d * lax.rsqrt(var + LN_EPS) * g + b


def _ln_mod_body(r_ref, g_ref, b_ref, sc_ref, sh_ref, x_ref, h_ref):
    y = _layernorm(r_ref[...], g_ref[...], b_ref[...])
    x_ref[...] = y
    h_ref[...] = (y * (1.0 + sc_ref[...]) + sh_ref[...]).astype(h_ref.dtype)


def _ln_body(r_ref, g_ref, b_ref, x_ref):
    x_ref[...] = _layernorm(r_ref[...], g_ref[...], b_ref[...])


def _silu_mul(g, v):
    return g * jax.nn.sigmoid(g) * v


def _ffnup_body(h_ref, hs_ref, wg_ref, wv_ref, cwg_ref, cwv_ref, eg_ref, ev_ref, wd_ref,
                act_ref, tg_ref, tv_ref, acts_ref, ugs_ref, uvs_ref, wdb_ref,
                cg_ref, cv_ref, wgs_ref, wvs_ref, *, tiles_per_seq, seg):
    j = pl.program_id(1)
    wdb_ref[...] = wd_ref[...].astype(BF)
    wg = _staged(wg_ref, wgs_ref)
    wv = _staged(wv_ref, wvs_ref)
    prev_g = _carry_in(cg_ref, tiles_per_seq)
    prev_v = _carry_in(cv_ref, tiles_per_seq)
    for rows, lhs, n in _chunks_with_riders(h_ref, hs_ref):
        ug_all = _dot(lhs, wg[...])
        uv_all = _dot(lhs, wv[...])
        ug = ug_all[:n, :]
        uv = uv_all[:n, :]
        yg = _conv3_carry(ug, cwg_ref, prev_g)
        yv = _conv3_carry(uv, cwv_ref, prev_v)
        prev_g = ug[n - SUBLANES:, :]
        prev_v = uv[n - SUBLANES:, :]
        act_ref[rows, :] = _silu_mul(yg, yv).astype(act_ref.dtype)
        if ug_all.shape[0] > n:
            ugs = ug_all[n:, :]
            uvs = uv_all[n:, :]
            ugs_ref[...] = ugs
            uvs_ref[...] = uvs
            ygs = _conv3_segments(ugs, cwg_ref, eg_ref[...], seg)
            yvs = _conv3_segments(uvs, cwv_ref, ev_ref[...], seg)
            acts_ref[...] = _silu_mul(ygs, yvs).astype(acts_ref.dtype)
    cg_ref[j] = prev_g
    cv_ref[j] = prev_v
    tg_ref[...] = prev_g
    tv_ref[...] = prev_v


class _Group:
    def __init__(self, batch, seq, tm, tm_big, ada_rows):
        self.batch, self.seq, self.tm, self.tm_big = batch, seq, tm, tm_big
        self.m = batch * seq
        self.per_row = seq < tm
        self.ada = ada_rows

    def param_spec(self, chunk, tm, tn):
        cb = chunk * (D_MODEL // tn)
        if self.per_row:
            return pl.BlockSpec((tm, tn), lambda i, j: (i, cb + j))
        tps = self.seq // tm
        return pl.BlockSpec((None, 1, tn), lambda i, j: (i // tps, 0, cb + j))


def _resident_rows(tm, width):
    return pl.BlockSpec((tm, width), lambda i, j: (i, 0), pipeline_mode=pl.Buffered(1))


def _ada(c_all, w_ada, b_ada):
    tn = 512
    n = w_ada.shape[-1]
    return pl.pallas_call(
        _ada_body,
        out_shape=jax.ShapeDtypeStruct((DEPTH, ADA_ROWS, n), F32),
        grid=(DEPTH, n // tn),
        in_specs=[pl.BlockSpec((ADA_ROWS, D_MODEL), lambda l, j: (0, 0)),
                  pl.BlockSpec((None, D_MODEL, tn), lambda l, j: (l, 0, j)),
                  pl.BlockSpec((None, 1, tn), lambda l, j: (l, 0, j))],
        out_specs=pl.BlockSpec((None, ADA_ROWS, tn), lambda l, j: (l, 0, j)),
        compiler_params=_cp(2), name="ada",
    )(c_all, w_ada, b_ada.reshape(DEPTH, 1, n))


def _modulate(grp, x):
    tr = min(grp.tm, 512)
    return pl.pallas_call(
        _mod_body,
        out_shape=jax.ShapeDtypeStruct((grp.m, D_MODEL), BF),
        grid=(grp.m // tr,),
        in_specs=[pl.BlockSpec((tr, D_MODEL), lambda i: (i, 0)),
                  _row_param_spec(grp, SC1, tr), _row_param_spec(grp, SH1, tr)],
        out_specs=pl.BlockSpec((tr, D_MODEL), lambda i: (i, 0)),
        compiler_params=_cp(1), name="modulate",
    )(x, grp.ada, grp.ada)


def _row_param_spec(grp, chunk, tr):
    if grp.per_row:
        return pl.BlockSpec((tr, D_MODEL), lambda i: (i, chunk))
    tps = grp.seq // tr
    return pl.BlockSpec((None, 1, D_MODEL), lambda i: (i // tps, 0, chunk))


def _proj_rope(grp, h, w_in, l, col0, n_cols, tm, tn, cos, sin, n_rope, scale, out_dtype, name, cast_of=None):
    cb = col0 // tn
    t_tiles = cos.shape[0] // tm
    nj = n_cols // tn
    casts = [] if cast_of is None else [_row_cast(cast_of, l, (grp.m // tm) * nj, lambda i, j: i * nj + j)]
    out = pl.pallas_call(
        _with_casts(functools.partial(_proj_rope_body, n_rope=n_rope, scale=scale), 4, 1, len(casts)),
        out_shape=(jax.ShapeDtypeStruct((grp.m, n_cols), out_dtype), *[c[3] for c in casts]),
        grid=(grp.m // tm, nj),
        in_specs=[pl.BlockSpec((tm, D_MODEL), lambda i, j: (i, 0)),
                  pl.BlockSpec((None, D_MODEL, tn), lambda i, j: (l, 0, cb + j)),
                  pl.BlockSpec((tm, LANES), lambda i, j: (i % t_tiles, 0)),
                  pl.BlockSpec((tm, LANES), lambda i, j: (i % t_tiles, 0)),
                  *[c[1] for c in casts]],
        out_specs=(pl.BlockSpec((tm, tn), lambda i, j: (i, j)), *[c[2] for c in casts]),
        scratch_shapes=[pltpu.VMEM((D_MODEL, tn), BF)],
        compiler_params=_cp(2), name=name,
    )(h, w_in, cos, sin, *[c[0] for c in casts])
    return out[0] if cast_of is None else out


def _attn_prompt(sinks, q, kv, batch, seq, l, w_in, w_attn_br, w_conv_br):
    tq = ATTN_BLOCKS * WINDOW
    nb = seq // tq
    step_of = lambda b, m: b * nb + m
    casts = [_col_cast(w_in, l, COL_GA, 2 * D_MODEL, batch * nb, step_of),
             _row_cast(w_attn_br, l, batch * nb, step_of), _row_cast(w_conv_br, l, batch * nb, step_of)]
    prev_block = lambda b, m: (ATTN_BLOCKS * b * nb + jnp.maximum(ATTN_BLOCKS * m - 1, 0), 0)
    return pl.pallas_call(
        _with_casts(_attn_prompt_body, 4, 1, len(casts)),
        out_shape=(jax.ShapeDtypeStruct(q.shape, BF), *[c[3] for c in casts]),
        grid=(batch, nb),
        in_specs=[pl.BlockSpec(memory_space=pltpu.SMEM),
                  pl.BlockSpec((tq, ATTN_DIM), lambda b, m: (b * nb + m, 0)),
                  pl.BlockSpec((WINDOW, 2 * KV_DIM), prev_block),
                  pl.BlockSpec((tq, 2 * KV_DIM), lambda b, m: (b * nb + m, 0)),
                  *[c[1] for c in casts]],
        out_specs=(pl.BlockSpec((tq, ATTN_DIM), lambda b, m: (b * nb + m, 0)), *[c[2] for c in casts]),
        compiler_params=_cp(2), name="attn_prompt",
    )(sinks, q, kv, kv, *[c[0] for c in casts])


def _attn_sample(sinks, q, kvn, cache_k, cache_v, l, batch, seq):
    return pl.pallas_call(
        _attn_sample_body,
        out_shape=(jax.ShapeDtypeStruct(q.shape, F32),
                   jax.ShapeDtypeStruct((batch, WINDOW, KV_DIM), F32),
                   jax.ShapeDtypeStruct((batch, WINDOW, KV_DIM), F32)),
        grid=(batch,),
        in_specs=[pl.BlockSpec(memory_space=pltpu.SMEM),
                  pl.BlockSpec((seq, ATTN_DIM), lambda b: (b, 0)),
                  pl.BlockSpec((seq, 2 * KV_DIM), lambda b: (b, 0)),
                  pl.BlockSpec((None, None, WINDOW, KV_DIM), lambda b: (l, b, 0, 0)),
                  pl.BlockSpec((None, None, WINDOW, KV_DIM), lambda b: (l, b, 0, 0))],
        out_specs=(pl.BlockSpec((seq, ATTN_DIM), lambda b: (b, 0)),
                   pl.BlockSpec((None, WINDOW, KV_DIM), lambda b: (b, 0, 0)),
                   pl.BlockSpec((None, WINDOW, KV_DIM), lambda b: (b, 0, 0))),
        compiler_params=_cp(1), name="attn_sample",
    )(sinks, q, kvn, cache_k, cache_v)


def _rider_rows(gp, gs, tm):
    rs = gs.m // (gp.m // tm)
    assert rs * (gp.m // tm) == gs.m and rs % gs.seq == 0 and rs % (2 * SUBLANES) == 0
    return rs


def _in_conv(gp, gs, h, hs, w_in, conv_w, l, past_rows):
    tm, tn = gp.tm_big, 256
    nj = CONV_DIM // tn
    rs = _rider_rows(gp, gs, tm)
    w_spec = lambda col0: pl.BlockSpec((None, D_MODEL, tn), lambda i, j: (l, 0, col0 // tn + j))
    tile = lambda rows: pl.BlockSpec((rows, tn), lambda i, j: (i, j))
    return pl.pallas_call(
        functools.partial(_inconv_body, tiles_per_seq=gp.seq // tm, seg=gs.seq),
        out_shape=(jax.ShapeDtypeStruct((gp.m, CONV_DIM), BF),
                   jax.ShapeDtypeStruct((gp.m // tm, SUBLANES, CONV_DIM), F32),
                   jax.ShapeDtypeStruct((gs.m, CONV_DIM), BF),
                   jax.ShapeDtypeStruct((gs.m, CONV_DIM), F32)),
        grid=(gp.m // tm, nj),
        in_specs=[_resident_rows(tm, D_MODEL), pl.BlockSpec((rs, D_MODEL), lambda i, j: (i, 0)),
                  w_spec(COL_CB), w_spec(COL_CC), w_spec(COL_CX),
                  pl.BlockSpec((None, 3, tn), lambda i, j: (l, 0, j)), tile(rs)],
        out_specs=(tile(tm), pl.BlockSpec((None, SUBLANES, tn), lambda i, j: (i, 0, j)), tile(rs), tile(rs)),
        scratch_shapes=[pltpu.VMEM((nj, SUBLANES, tn), F32)] + [pltpu.VMEM((D_MODEL, tn), BF)] * 3,
        compiler_params=_cp(2), name="in_conv",
    )(h, hs, w_in, w_in, w_in, conv_w, past_rows)


def _row_cast(w, l, steps, step_of):
    _, r, c = w.shape
    rows = r // steps
    assert rows * steps == r and rows % (2 * SUBLANES) == 0
    return (w, pl.BlockSpec((None, rows, c), lambda *g: (l, step_of(*g), 0)),
            pl.BlockSpec((rows, c), lambda *g: (step_of(*g), 0)), jax.ShapeDtypeStruct((r, c), BF))


def _col_cast(w, l, col0, n_cols, steps, step_of):
    _, r, _ = w.shape
    cols = n_cols // steps
    assert cols * steps == n_cols and cols % LANES == 0 and col0 % cols == 0
    return (w, pl.BlockSpec((None, r, cols), lambda *g: (l, 0, col0 // cols + step_of(*g))),
            pl.BlockSpec((r, cols), lambda *g: (0, step_of(*g))), jax.ShapeDtypeStruct((r, n_cols), BF))


def _merged(gp, gs, h, attn, s, hs, attn_s, s_s, w_gates, w_ab, w_cb):
    tm, tn = gp.tm, 256
    rs = _rider_rows(gp, gs, tm)
    row = lambda rows, width: pl.BlockSpec((rows, width), lambda i, j: (i, 0))
    w2d = lambda k, col0: pl.BlockSpec((k, tn), lambda i, j: (0, col0 // tn + j))
    tile = lambda rows: pl.BlockSpec((rows, tn), lambda i, j: (i, j))
    return pl.pallas_call(
        _merged_body,
        out_shape=(jax.ShapeDtypeStruct((gp.m, D_MODEL), BF), jax.ShapeDtypeStruct((gs.m, D_MODEL), BF)),
        grid=(gp.m // tm, D_MODEL // tn),
        in_specs=[row(tm, D_MODEL), row(tm, ATTN_DIM), row(tm, CONV_DIM),
                  row(rs, D_MODEL), row(rs, ATTN_DIM), row(rs, CONV_DIM),
                  w2d(D_MODEL, 0), w2d(D_MODEL, COL_GC - COL_GA), w2d(ATTN_DIM, 0), w2d(CONV_DIM, 0)],
        out_specs=(tile(tm), tile(rs)),
        compiler_params=_cp(2), name="merged",
    )(h, attn, s, hs, attn_s, s_s, w_gates, w_gates, w_ab, w_cb)


def _mm_resid(gp, gs, a, a_s, w_bf16, x, x_s, gate_chunk, tm, tn, name):
    k = a.shape[1]
    rs = _rider_rows(gp, gs, tm)
    tile = lambda rows: pl.BlockSpec((rows, tn), lambda i, j: (i, j))
    return pl.pallas_call(
        _mm_resid_body,
        out_shape=(jax.ShapeDtypeStruct((gp.m, D_MODEL), F32), jax.ShapeDtypeStruct((gs.m, D_MODEL), F32)),
        grid=(gp.m // tm, D_MODEL // tn),
        in_specs=[pl.BlockSpec((tm, k), lambda i, j: (i, 0)), pl.BlockSpec((rs, k), lambda i, j: (i, 0)),
                  pl.BlockSpec((k, tn), lambda i, j: (0, j)), tile(tm), tile(rs),
                  gp.param_spec(gate_chunk, tm, tn), gs.param_spec(gate_chunk, rs, tn)],
        out_specs=(tile(tm), tile(rs)),
        compiler_params=_cp(2), name=name,
    )(a, a_s, w_bf16, x, x_s, gp.ada, gs.ada)


def _ln(grp, r, g, b, l, mod_grp, mod_chunks, name):
    tr = min(grp.m, 512)
    row = pl.BlockSpec((tr, D_MODEL), lambda i: (i, 0))
    vec = pl.BlockSpec((None, 1, D_MODEL), lambda i: (l, 0, 0))
    g3 = g.reshape(DEPTH, 1, D_MODEL)
    b3 = b.reshape(DEPTH, 1, D_MODEL)
    if mod_chunks is None:
        return pl.pallas_call(
            _ln_body, out_shape=jax.ShapeDtypeStruct((grp.m, D_MODEL), F32), grid=(grp.m // tr,),
            in_specs=[row, vec, vec], out_specs=row, compiler_params=_cp(1), name=name,
        )(r, g3, b3), None
    sc, sh = mod_chunks
    return pl.pallas_call(
        _ln_mod_body,
        out_shape=(jax.ShapeDtypeStruct((grp.m, D_MODEL), F32), jax.ShapeDtypeStruct((grp.m, D_MODEL), BF)),
        grid=(grp.m // tr,),
        in_specs=[row, vec, vec, _row_param_spec(mod_grp, sc, tr), _row_param_spec(mod_grp, sh, tr)],
        out_specs=(row, row), compiler_params=_cp(1), name=name,
    )(r, g3, b3, mod_grp.ada, mod_grp.ada)


def _ffn_up(gp, gs, h, hs, w_up, ffn_conv_w, w_down, l, past_g, past_v):
    tm, tn = gp.tm_big, 256
    nj = D_FF // tn
    rs = _rider_rows(gp, gs, tm)
    steps = (gp.m // tm) * nj
    wd_rows = D_FF // steps
    assert wd_rows * steps == D_FF and wd_rows % (2 * SUBLANES) == 0
    tile = lambda rows: pl.BlockSpec((rows, tn), lambda i, j: (i, j))
    tail_shape = jax.ShapeDtypeStruct((gp.m // tm, SUBLANES, D_FF), F32)
    tail_spec = pl.BlockSpec((None, SUBLANES, tn), lambda i, j: (i, 0, j))
    full_s = jax.ShapeDtypeStruct((gs.m, D_FF), F32)
    return pl.pallas_call(
        functools.partial(_ffnup_body, tiles_per_seq=gp.seq // tm, seg=gs.seq),
        out_shape=(jax.ShapeDtypeStruct((gp.m, D_FF), BF), tail_shape, tail_shape,
                   jax.ShapeDtypeStruct((gs.m, D_FF), BF), full_s, full_s,
                   jax.ShapeDtypeStruct((D_FF, D_MODEL), BF)),
        grid=(gp.m // tm, nj),
        in_specs=[_resident_rows(tm, D_MODEL), pl.BlockSpec((rs, D_MODEL), lambda i, j: (i, 0)),
                  pl.BlockSpec((None, D_MODEL, tn), lambda i, j: (l, 0, j)),
                  pl.BlockSpec((None, D_MODEL, tn), lambda i, j: (l, 0, nj + j)),
                  pl.BlockSpec((None, 3, tn), lambda i, j: (l, 0, j)),
                  pl.BlockSpec((None, 3, tn), lambda i, j: (l, 0, nj + j)),
                  tile(rs), tile(rs),
                  pl.BlockSpec((None, wd_rows, D_MODEL), lambda i, j: (l, i * nj + j, 0))],
        out_specs=(tile(tm), tail_spec, tail_spec, tile(rs), tile(rs), tile(rs),
                   pl.BlockSpec((wd_rows, D_MODEL), lambda i, j: (i * nj + j, 0))),
        scratch_shapes=[pltpu.VMEM((nj, SUBLANES, tn), F32), pltpu.VMEM((nj, SUBLANES, tn), F32),
                        pltpu.VMEM((D_MODEL, tn), BF), pltpu.VMEM((D_MODEL, tn), BF)],
        compiler_params=_cp(2), name="ffn_up",
    )(h, hs, w_up, w_up, ffn_conv_w, ffn_conv_w, past_g, past_v, w_down)


def _rope_tables(pos):
    inv = ROPE_THETA ** (-jnp.arange(0, HEAD_DIM, 2, dtype=F32) / HEAD_DIM)
    ang = pos.astype(F32)[:, None] * inv[None, :]
    cos, sin = jnp.cos(ang), jnp.sin(ang)
    return jnp.concatenate([cos] * 4, axis=-1), jnp.concatenate([-sin, sin, -sin, sin], axis=-1)


def _expand_past(past, seq):
    batch, _, n = past.shape
    return jnp.concatenate([past, jnp.zeros((batch, seq - 2, n), past.dtype)], axis=1).reshape(batch * seq, n)


def _last_rows(x, batch, seq, n_rows):
    return x.reshape(batch, seq, x.shape[-1])[:, seq - n_rows:, :]


def _qkv(grp, l, h, rope, w_in, cast_of=None):
    cos, sin = rope
    q = _proj_rope(grp, h, w_in, l, COL_Q, ATTN_DIM, grp.tm_big, 256, cos, sin, 256 // LANES, HEAD_DIM ** -0.5,
                   F32 if grp.per_row else BF, "proj_q", cast_of)
    kv = _proj_rope(grp, h, w_in, l, COL_KV, 2 * KV_DIM, grp.tm, 2 * KV_DIM, cos, sin, KV_DIM // LANES, 1.0,
                    F32, "proj_kv")
    return q, kv


def _layer(gp, gs, next_gp, next_gs, l, xp, xs, hp, hs, rope_p, rope_s, weights, sinks, past):
    (w_in, conv_w, w_attn_br, w_conv_br, w_out, ln1_g, ln1_b, w_up, ffn_conv_w, w_down, ln2_g, ln2_b) = weights
    cache_k, cache_v, state_conv, state_ffn = past
    tps = gp.seq // gp.tm_big
    state_rows = lambda tails: tails[tps - 1::tps, SUBLANES - 2:, :]

    q_s, kv_s = _qkv(gs, l, hs, rope_s, w_in)
    attn_s, k_s, v_s = _attn_sample(sinks, q_s, kv_s, cache_k, cache_v, l, gs.batch, gs.seq)
    (q_p, w_out_b), kv_p = _qkv(gp, l, hp, rope_p, w_in, cast_of=w_out)
    attn_p, w_gates, w_ab, w_cb = _attn_prompt(sinks, q_p, kv_p, gp.batch, gp.seq, l, w_in, w_attn_br, w_conv_br)
    kv_tail = _last_rows(kv_p, gp.batch, gp.seq, WINDOW)
    k_p, v_p = kv_tail[..., :KV_DIM], kv_tail[..., KV_DIM:]

    s_p, tails, s_s, u_s = _in_conv(gp, gs, hp, hs, w_in, conv_w, l, _expand_past(state_conv[l], gs.seq))
    conv_p = state_rows(tails)
    conv_s = _last_rows(u_s, gs.batch, gs.seq, 2)

    merged_p, merged_s = _merged(gp, gs, hp, attn_p, s_p, hs, attn_s, s_s, w_gates, w_ab, w_cb)
    r1_p, r1_s = _mm_resid(gp, gs, merged_p, merged_s, w_out_b, xp, xs, G1, gp.tm, 1024, "out_proj")
    x1_s, h2_s = _ln(gs, r1_s, ln1_g, ln1_b, l, gs, (SC2, SH2), "ln1")
    x1_p, h2_p = _ln(gp, r1_p, ln1_g, ln1_b, l, gp, (SC2, SH2), "ln1")

    eg = _expand_past(state_ffn[l][..., :D_FF], gs.seq)
    ev = _expand_past(state_ffn[l][..., D_FF:], gs.seq)
    act_p, tg, tv, act_s, ug_s, uv_s, w_down_b = _ffn_up(gp, gs, h2_p, h2_s, w_up, ffn_conv_w, w_down, l, eg, ev)
    ffn_p = state_rows(jnp.concatenate([tg, tv], axis=-1))
    ffn_s = _last_rows(jnp.concatenate([ug_s, uv_s], axis=-1), gs.batch, gs.seq, 2)

    r2_p, r2_s = _mm_resid(gp, gs, act_p, act_s, w_down_b, x1_p, x1_s, G2, 512, 512, "ffn_down")
    mod = None if next_gp is None else (SC1, SH1)
    xs, hs = _ln(gs, r2_s, ln2_g, ln2_b, l, next_gs, mod, "ln2")
    xp, hp = _ln(gp, r2_p, ln2_g, ln2_b, l, next_gp, mod, "ln2")

    heads = lambda k, grp: k.reshape(grp.batch, WINDOW, N_KV_HEADS, HEAD_DIM)
    return (xp, xs, hp, hs, (heads(k_p, gp), heads(v_p, gp), conv_p, ffn_p),
            (heads(k_s, gs), heads(v_s, gs), conv_s, ffn_s))


def kernel(x_prompt, x_sample, cache_attn_k, cache_attn_v, state_conv, state_ffn_conv, c_prompt, c_sample, w_ada, b_ada, w_in, attn_sinks, conv_w, w_attn_br, w_conv_br, w_out, ln1_g, ln1_b, w_up, ffn_conv_w, w_down, ln2_g, ln2_b):
    bp, tp, _ = x_prompt.shape
    bs, ts, _ = x_sample.shape
    assert bp <= ADA_SAMPLE_ROW0 and ADA_SAMPLE_ROW0 + bs == ADA_ROWS

    c_all = jnp.concatenate([c_prompt, jnp.zeros((ADA_SAMPLE_ROW0 - bp, D_MODEL), F32), c_sample], axis=0)
    ada = _ada(c_all, w_ada, b_ada)

    weights = (w_in, conv_w, w_attn_br, w_conv_br, w_out, ln1_g, ln1_b, w_up, ffn_conv_w, w_down, ln2_g, ln2_b)
    rope_p = _rope_tables(jnp.arange(tp, dtype=jnp.int32))
    rope_s = tuple(jnp.tile(t, (bs, 1)) for t in _rope_tables(PAST_LEN + jnp.arange(ts, dtype=jnp.int32)))
    cache_k = cache_attn_k.reshape(DEPTH, bs, WINDOW, KV_DIM)
    cache_v = cache_attn_v.reshape(DEPTH, bs, WINDOW, KV_DIM)
    past = (cache_k, cache_v, state_conv, state_ffn_conv)

    xp = x_prompt.reshape(bp * tp, D_MODEL)
    xs = x_sample.reshape(bs * ts, D_MODEL)
    gps = [_Group(bp, tp, 1024, 2048, ada[l, :bp].reshape(bp, 1, -1)) for l in range(DEPTH)] + [None]
    gss = [_Group(bs, ts, bs * ts, bs * ts, jnp.repeat(ada[l, ADA_SAMPLE_ROW0:], ts, axis=0))
           for l in range(DEPTH)] + [None]
    hp = _modulate(gps[0], xp)
    hs = _modulate(gss[0], xs)
    outs_p, outs_s = [], []
    for l in range(DEPTH):
        xp, xs, hp, hs, st_p, st_s = _layer(gps[l], gss[l], gps[l + 1], gss[l + 1], l, xp, xs, hp, hs,
                                            rope_p, rope_s, weights, attn_sinks[l], past)
        outs_p.append(st_p)
        outs_s.append(st_s)
    stack = lambda outs, k: jnp.stack([o[k] for o in outs])
    return (xp.reshape(bp, tp, D_MODEL), xs.reshape(bs, ts, D_MODEL),
            stack(outs_p, 0), stack(outs_p, 1), stack(outs_p, 2), stack(outs_p, 3),
            stack(outs_s, 0), stack(outs_s, 1), stack(outs_s, 2), stack(outs_s, 3))
```
